```python
import jax, jax.numpy as jnp
from jax import lax
import numpy as np

D_MODEL = 1024
BATCH = 16
SEQ = 256
DEPTH = 4
DEC_BATCH = 4
DEC_SEQ = 1024
PAST_LEN = 512

GRID_W = 64
D_RNN = 1024
H_RNN = 16
BS_RNN = D_RNN // H_RNN
RNN_CONV = 4
C_RG = 8.0
D_SG = 1024
SG_GROUPS = 8
SG_GD = D_SG // SG_GROUPS
CHUNK = 128
D_CF = 1024
CF_CONV = 31
D_FF = 4096
FFN_CONV = 3
N_BRANCH = 3
N_IN = 2 * D_RNN + 2 * D_SG + 2 * D_CF + N_BRANCH * D_MODEL
EPS = 1e-6
POS_BASE = 10000.0

kernel_name = 'hybrid_rglru_gmlp_conformer_dit_step'


def _rms(x, g):
    xf = x.astype(jnp.float32)
    y = xf * lax.rsqrt(jnp.mean(xf * xf, axis=-1, keepdims=True) + EPS)
    return (y * g.astype(jnp.float32)).astype(x.dtype)


def _ln(x, g, b=None):
    xf = x.astype(jnp.float32)
    mu = jnp.mean(xf, axis=-1, keepdims=True)
    xc = xf - mu
    y = xc * lax.rsqrt(jnp.mean(xc * xc, axis=-1, keepdims=True) + EPS) * g.astype(jnp.float32)
    if b is not None:
        y = y + b.astype(jnp.float32)
    return y.astype(x.dtype)


def _dwconv(x, w, b, pad):
    C = x.shape[-1]
    y = lax.conv_general_dilated(x, w[:, None, :].astype(x.dtype), window_strides=(1,), padding=[pad],
                                 dimension_numbers=('NWC', 'WIO', 'NWC'), feature_group_count=C)
    return y + b.astype(x.dtype)


def _pos_2d(seq_len, d, dtype):
    rows = seq_len // GRID_W
    t = jnp.arange(rows * GRID_W)
    r = (t // GRID_W).astype(jnp.float32)
    col = (t % GRID_W).astype(jnp.float32)
    q = d // 4
    omega = 1.0 / (POS_BASE ** (jnp.arange(q, dtype=jnp.float32) / q))
    def emb(p):
        ang = p[:, None] * omega[None, :]
        return jnp.concatenate([jnp.sin(ang), jnp.cos(ang)], axis=-1)
    return jnp.concatenate([emb(r), emb(col)], axis=-1).astype(dtype)


def _lin_comb(e1, e2):
    a1, b1 = e1
    a2, b2 = e2
    return a1 * a2, a2 * b1 + b2


def _rglru(xc, h0, rg_w, rg_b, rg_lambda):
    B, S, _ = xc.shape
    xh = xc.reshape(B, S, H_RNN, BS_RNN)
    g = jnp.einsum('bshi,dkhij->dkbshj', xh, rg_w.astype(xc.dtype)).reshape(2, 2, B, S, D_RNN)
    g = jax.nn.sigmoid(g.astype(jnp.float32) + rg_b.astype(jnp.float32)[:, :, None, None, :])
    r, i = g[:, 0], g[:, 1]
    log_a = -C_RG * r * jax.nn.softplus(-rg_lambda.astype(jnp.float32))[:, None, None, :]
    a = jnp.exp(log_a)
    u = jnp.sqrt(-jnp.expm1(2.0 * log_a)) * i * xc.astype(jnp.float32)[None]
    A_f, B_f = lax.associative_scan(_lin_comb, (a[0], u[0]), axis=1)
    A_b, B_b = lax.associative_scan(_lin_comb, (a[1], u[1]), axis=1, reverse=True)
    h_f = A_f * h0[:, 0, None, :] + B_f
    h_b = A_b * h0[:, 1, None, :] + B_b
    return h_f, h_b


def _layer(x, cond, h0, p):
    dt = x.dtype
    B, S, _ = x.shape
    mod = jax.nn.silu(cond) @ p['w_mod'] + p['b_mod']
    sh1, sc1, gt1, sh2, sc2, gt2 = [m[:, None, :] for m in jnp.split(mod, 6, axis=-1)]
    gn = p['g_norm']

    h = _rms(x, gn[0]) * (1 + sc1) + sh1
    z = h @ p['w_in'] + p['b_in']
    o = np.cumsum([D_RNN, D_RNN, 2 * D_SG, 2 * D_CF])
    z_rx, z_rg, z_sg, z_cf, z_gate = jnp.split(z, [int(v) for v in o], axis=-1)

    xc = _dwconv(z_rx, p['rnn_conv_w'], p['rnn_conv_b'], (2, 1))
    h_f, h_b = _rglru(xc, h0, p['rg_w'], p['rg_b'], p['rg_lambda'])
    y_a = (h_f + h_b).astype(dt) * jax.nn.gelu(z_rg)

    uv = jax.nn.gelu(z_sg)
    su, sv = jnp.split(uv, 2, axis=-1)
    sv = _ln(sv, p['sg_norm_g']).reshape(B, S // CHUNK, CHUNK, SG_GROUPS, SG_GD)
    mixed = jnp.einsum('gpq,bnqgc->bnpgc', p['sg_w'].astype(dt), sv) + p['sg_b'].T[None, None, :, :, None]
    y_b = su * mixed.reshape(B, S, D_SG)

    ca, cg = jnp.split(z_cf, 2, axis=-1)
    yc = ca * jax.nn.sigmoid(cg)
    yc = _dwconv(yc, p['cf_conv_w'], p['cf_conv_b'], (CF_CONV // 2, CF_CONV // 2))
    y_c = jax.nn.silu(_ln(yc, p['cf_ln_g'], p['cf_ln_b']))

    g_a, g_b, g_c = jnp.split(jax.nn.sigmoid(z_gate), 3, axis=-1)
    wb = p['w_branch']
    merged = g_a * (y_a @ wb[0]) + g_b * (y_b @ wb[1]) + g_c * (y_c @ wb[2])
    out = merged @ p['w_out']
    x = x + gt1 * _rms(out, gn[1])

    h = _rms(x, gn[2]) * (1 + sc2) + sh2
    up = _dwconv(h @ p['ffn_up'], p['ffn_conv_w'], p['ffn_conv_b'], (FFN_CONV // 2, FFN_CONV // 2))
    fg, fv = jnp.split(up, 2, axis=-1)
    f = (jax.nn.gelu(fg) * fv) @ p['ffn_down']
    x = x + gt2 * _rms(f, gn[3])
    return x, h_f, h_b


def setup_inputs(seed: int = 0) -> dict:
    key = jax.random.key(seed)
    ks = jax.random.split(key, 32)
    nrm = lambda k, shape, s: jax.random.normal(k, shape, jnp.float32) * s
    L, D = DEPTH, D_MODEL
    a8 = jax.random.uniform(ks[10], (L, 2, D_RNN), jnp.float32, 0.9, 0.999)
    s = a8 ** (1.0 / C_RG)
    rg_lambda = jnp.log(s) - jnp.log1p(-s)
    return {
        'x_prompt': nrm(ks[0], (BATCH, SEQ, D), 1.0),
        'x_sample': nrm(ks[1], (DEC_BATCH, DEC_SEQ, D), 1.0),
        'state_rglru': nrm(ks[2], (DEC_BATCH, DEPTH, 2, D_RNN), 0.5),
        'c': nrm(ks[3], (DEC_BATCH, D), 1.0),
        'c_ctx': nrm(ks[4], (D,), 1.0),
        'w_mod': nrm(ks[5], (L, D, 6 * D), D ** -0.5),
        'b_mod': nrm(ks[6], (L, 6 * D), 0.01),
        'g_norm': 1.0 + nrm(ks[7], (L, 4, D), 0.05),
        'w_in': nrm(ks[8], (L, D, N_IN), D ** -0.5),
        'b_in': nrm(ks[9], (L, N_IN), 0.01),
        'rnn_conv_w': nrm(ks[11], (L, RNN_CONV, D_RNN), RNN_CONV ** -0.5),
        'rnn_conv_b': nrm(ks[12], (L, D_RNN), 0.01),
        'rg_w': nrm(ks[13], (L, 2, 2, H_RNN, BS_RNN, BS_RNN), BS_RNN ** -0.5),
        'rg_b': nrm(ks[14], (L, 2, 2, D_RNN), 0.01),
        'rg_lambda': rg_lambda,
        'sg_norm_g': 1.0 + nrm(ks[15], (L, D_SG), 0.05),
        'sg_w': nrm(ks[16], (L, SG_GROUPS, CHUNK, CHUNK), 0.5 * CHUNK ** -0.5),
        'sg_b': 1.0 + nrm(ks[17], (L, SG_GROUPS, CHUNK), 0.1),
        'cf_conv_w': nrm(ks[18], (L, CF_CONV, D_CF), CF_CONV ** -0.5),
        'cf_conv_b': nrm(ks[19], (L, D_CF), 0.01),
        'cf_ln_g': 1.0 + nrm(ks[20], (L, D_CF), 0.05),
        'cf_ln_b': nrm(ks[21], (L, D_CF), 0.01),
        'w_branch': nrm(ks[22], (L, N_BRANCH, D_RNN, D), D_RNN ** -0.5),
        'w_out': nrm(ks[23], (L, D, D), D ** -0.5),
        'ffn_up': nrm(ks[24], (L, D, 2 * D_FF), D ** -0.5),
        'ffn_conv_w': nrm(ks[25], (L, FFN_CONV, 2 * D_FF), FFN_CONV ** -0.5),
        'ffn_conv_b': nrm(ks[26], (L, 2 * D_FF), 0.01),
        'ffn_down': nrm(ks[27], (L, D_FF, D), D_FF ** -0.5),
    }


def reference(x_prompt, x_sample, state_rglru, c, c_ctx, w_mod, b_mod, g_norm, w_in, b_in,
              rnn_conv_w, rnn_conv_b, rg_w, rg_b, rg_lambda, sg_norm_g, sg_w, sg_b,
              cf_conv_w, cf_conv_b, cf_ln_g, cf_ln_b, w_branch, w_out, ffn_up, ffn_conv_w,
              ffn_conv_b, ffn_down):
    def layer_params(l):
        return {'w_mod': w_mod[l], 'b_mod': b_mod[l], 'g_norm': g_norm[l], 'w_in': w_in[l],
                'b_in': b_in[l], 'rnn_conv_w': rnn_conv_w[l], 'rnn_conv_b': rnn_conv_b[l],
                'rg_w': rg_w[l], 'rg_b': rg_b[l], 'rg_lambda': rg_lambda[l],
                'sg_norm_g': sg_norm_g[l], 'sg_w': sg_w[l], 'sg_b': sg_b[l],
                'cf_conv_w': cf_conv_w[l], 'cf_conv_b': cf_conv_b[l], 'cf_ln_g': cf_ln_g[l],
                'cf_ln_b': cf_ln_b[l], 'w_branch': w_branch[l], 'w_out': w_out[l],
                'ffn_up': ffn_up[l], 'ffn_conv_w': ffn_conv_w[l], 'ffn_conv_b': ffn_conv_b[l],
                'ffn_down': ffn_down[l]}

    y_prompt = x_prompt
    h0_ctx = jnp.zeros((x_prompt.shape[0], 2, D_RNN), jnp.float32)
    cond_ctx = c_ctx[None, :]
    states = []
    for l in range(DEPTH):
        y_prompt, h_f, h_b = _layer(y_prompt, cond_ctx, h0_ctx, layer_params(l))
        states.append(jnp.stack([h_f[:, -1], h_b[:, 0]], axis=1))
    new_state_rglru = jnp.stack(states, axis=1)

    y_sample = x_sample + _pos_2d(x_sample.shape[1], x_sample.shape[2], x_sample.dtype)[None]
    for l in range(DEPTH):
        y_sample, _, _ = _layer(y_sample, c, state_rglru[:, l].astype(jnp.float32), layer_params(l))

    return (y_prompt, y_sample, new_state_rglru)
```

```python
import functools

import jax
import jax.numpy as jnp
import numpy as np
from jax import lax
from jax.experimental import pallas as pl
from jax.experimental.pallas import tpu as pltpu

F32 = jnp.float32
BF16 = jnp.bfloat16

D_MODEL = 1024
BATCH = 16
SEQ = 256
DEPTH = 4
DEC_BATCH = 4
DEC_SEQ = 1024
GRID_W = 64
D_RNN = 1024
H_RNN = 16
BS_RNN = D_RNN // H_RNN
RNN_CONV = 4
C_RG = 8.0
D_SG = 1024
SG_GROUPS = 8
SG_GD = D_SG // SG_GROUPS
CHUNK = 128
D_CF = 1024
CF_CONV = 31
D_FF = 4096
FFN_CONV = 3
N_IN = 2 * D_RNN + 2 * D_SG + 2 * D_CF + 3 * D_MODEL
EPS = 1e-6
POS_BASE = 10000.0

LANES = 128
SUBLANES = 8
MXU_DIM = 256

N_TOK = BATCH * SEQ + DEC_BATCH * DEC_SEQ
TM = 1024
N_TILES = N_TOK // TM
CTX_TILES = BATCH * SEQ // TM
REGION = SEQ
N_REGIONS = TM // REGION
MOD_ROWS = 8
CTX_MOD_ROW = DEC_BATCH
CONV_RB = 64
FF_CW = 512
TM_MERGE = 512
VMEM_LIMIT = 56 * 1024 * 1024


def _sigmoid(x):
    return 0.5 * (jnp.tanh(0.5 * x) + 1.0)


def _rms_rows(x, g):
    return x * lax.rsqrt(jnp.mean(x * x, axis=-1, keepdims=True) + EPS) * g


def _mod_row_of_tile(i, tile_rows):
    tiles_per_latent = DEC_SEQ // tile_rows
    ctx_tiles = BATCH * SEQ // tile_rows
    lat = jnp.maximum(i - ctx_tiles, 0) // tiles_per_latent
    return jnp.where(i < ctx_tiles, CTX_MOD_ROW, lat)


def _params(n_grid):
    return pltpu.CompilerParams(dimension_semantics=("arbitrary",) * n_grid,
                                vmem_limit_bytes=VMEM_LIMIT)


def _layer_call(body, grid, in_specs, out_specs, out_shape, scratch_shapes, name):
    return pl.pallas_call(
        body,
        grid_spec=pltpu.PrefetchScalarGridSpec(
            num_scalar_prefetch=1, grid=grid, in_specs=in_specs, out_specs=out_specs,
            scratch_shapes=scratch_shapes),
        out_shape=out_shape,
        compiler_params=_params(len(grid)),
        name=name)


def _region_stride(halo):
    return REGION + 2 * halo


def _fill_halos(pad_ref, halo, is_ctx):
    rs = _region_stride(halo)
    width = pad_ref.shape[1]
    zeros = jnp.zeros((halo, width), F32)
    for s in range(N_REGIONS):
        lo = s * rs
        if s == 0:
            before = zeros
        else:
            prev_end = (s - 1) * rs + halo + REGION
            before = jnp.where(is_ctx, 0.0, pad_ref[prev_end - halo:prev_end, :])
        pad_ref[lo:lo + halo, :] = before
        if s == N_REGIONS - 1:
            after = zeros
        else:
            next_start = (s + 1) * rs + halo
            after = jnp.where(is_ctx, 0.0, pad_ref[next_start:next_start + halo, :])
        pad_ref[lo + halo + REGION:lo + 2 * halo + REGION, :] = after


def _store_regions(pad_ref, halo, value):
    rs = _region_stride(halo)
    for s in range(N_REGIONS):
        pad_ref[s * rs + halo:s * rs + halo + REGION, :] = value[s * REGION:(s + 1) * REGION, :]


def _dwconv(pads, halo, offsets, emit, post=None):
    rs = _region_stride(halo)
    n_rb = REGION // CONV_RB
    wn = CONV_RB + 2 * halo
    width = pads[0][0].shape[1]
    by_residue = {}
    for k, d in enumerate(offsets):
        o = halo + d
        by_residue.setdefault(o % SUBLANES, []).append((k, o))

    def body(idx, carry):
        s = idx // n_rb
        rb = idx - s * n_rb
        wbase = pl.multiple_of(s * rs + rb * CONV_RB, SUBLANES)
        obase = pl.multiple_of(idx * CONV_RB, CONV_RB)
        for lb in range(width // LANES):
            lanes = slice(lb * LANES, (lb + 1) * LANES)
            accs = []
            for pad_ref, tap_fn in pads:
                window = pad_ref[pl.ds(wbase, wn), lanes]
                acc = None
                for r in sorted(by_residue):
                    shifted = window if r == 0 else pltpu.roll(window, wn - r, 0)
                    for k, o in by_residue[r]:
                        q = o - r
                        term = shifted[q:q + CONV_RB, :] * tap_fn(k, lanes)
                        acc = term if acc is None else acc + term
                accs.append(acc)
            emit(obase, lb, accs)
        if post is not None:
            post(obase)
        return carry

    lax.fori_loop(0, N_REGIONS * n_rb, body, 0)


def _prep_body(xp_ref, xs_ref, pos_ref, o_ref):
    i = pl.program_id(0)

    @pl.when(i < CTX_TILES)
    def _():
        o_ref[...] = xp_ref[...]

    @pl.when(i >= CTX_TILES)
    def _():
        o_ref[...] = xs_ref[...] + pos_ref[...]


def _prep(xp, xs, pos):
    return pl.pallas_call(
        _prep_body,
        grid=(N_TILES,),
        in_specs=[pl.BlockSpec((TM, D_MODEL), lambda i: (jnp.minimum(i, CTX_TILES - 1), 0)),
                  pl.BlockSpec((TM, D_MODEL), lambda i: (jnp.maximum(i - CTX_TILES, 0), 0)),
                  pl.BlockSpec((DEC_SEQ, D_MODEL), lambda i: (0, 0))],
        out_specs=pl.BlockSpec((TM, D_MODEL), lambda i: (i, 0)),
        out_shape=jax.ShapeDtypeStruct((N_TOK, D_MODEL), F32),
        compiler_params=_params(1),
        name="prep")(xp, xs, pos)


MOD_TN = 1536


def _mod_body(cond_ref, w_ref, b_ref, o_ref):
    c = cond_ref[...]
    s = (c * _sigmoid(c)).astype(BF16)
    o_ref[0] = jnp.dot(s, w_ref[0].astype(BF16), preferred_element_type=F32) + b_ref[0]


def _mod(cond, w_mod, b_mod):
    n_mod = 6 * D_MODEL
    return pl.pallas_call(
        _mod_body,
        grid=(DEPTH, n_mod // MOD_TN),
        in_specs=[pl.BlockSpec((MOD_ROWS, D_MODEL), lambda l, j: (0, 0)),
                  pl.BlockSpec((1, D_MODEL, MOD_TN), lambda l, j: (l, 0, j)),
                  pl.BlockSpec((1, 1, MOD_TN), lambda l, j: (l, 0, j))],
        out_specs=pl.BlockSpec((1, MOD_ROWS, MOD_TN), lambda l, j: (l, 0, j)),
        out_shape=jax.ShapeDtypeStruct((DEPTH, MOD_ROWS, n_mod), F32),
        compiler_params=_params(2),
        name="mod")(cond, w_mod, b_mod.reshape(DEPTH, 1, n_mod))


IN_TN = 1024
NORM_RB = 64


def _modulated_norm_to(h_ref, x_ref, g, scale, shift):
    def body(rb, carry):
        rows = pl.ds(pl.multiple_of(rb * NORM_RB, NORM_RB), NORM_RB)
        h = _rms_rows(x_ref[rows, :], g) * (1.0 + scale) + shift
        h_ref[rows, :] = h.astype(h_ref.dtype)
        return carry
    lax.fori_loop(0, x_ref.shape[0] // NORM_RB, body, 0)


def _inproj_body(l_ref, x_ref, mod_ref, gn_ref, w_ref, b_ref, z_ref, h_ref):
    i = pl.program_id(0)
    j = pl.program_id(1)

    @pl.when(j == 0)
    def _():
        m = mod_ref[0, _mod_row_of_tile(i, TM)]
        _modulated_norm_to(h_ref, x_ref, gn_ref[0, 0:1, :],
                           m[:, D_MODEL:2 * D_MODEL], m[:, 0:D_MODEL])

    acc = jnp.dot(h_ref[...], w_ref[0].astype(BF16), preferred_element_type=F32)
    z_ref[...] = (acc + b_ref[0]).astype(z_ref.dtype)


def _inproj(l, x, mod4, g_norm, w_in, b_in3):
    return _layer_call(
        _inproj_body,
        grid=(N_TILES, N_IN // IN_TN),
        in_specs=[pl.BlockSpec((TM, D_MODEL), lambda i, j, l: (i, 0)),
                  pl.BlockSpec((1, MOD_ROWS, 1, 6 * D_MODEL), lambda i, j, l: (l[0], 0, 0, 0)),
                  pl.BlockSpec((1, 4, D_MODEL), lambda i, j, l: (l[0], 0, 0)),
                  pl.BlockSpec((1, D_MODEL, IN_TN), lambda i, j, l: (l[0], 0, j)),
                  pl.BlockSpec((1, 1, IN_TN), lambda i, j, l: (l[0], 0, j))],
        out_specs=pl.BlockSpec((TM, IN_TN), lambda i, j, l: (i, j)),
        out_shape=jax.ShapeDtypeStruct((N_TOK, N_IN), BF16),
        scratch_shapes=[pltpu.VMEM((TM, D_MODEL), BF16)],
        name="inproj")(l, x, mod4, g_norm, w_in, b_in3)


RNN_HALO = 8
GATE_RB = 256
OUT_RB = 64


def _scan_block(a, u, h_prev, rows, reverse):
    for s in (1, 2, 4):
        if reverse:
            keep = rows < SUBLANES - s
            shift = SUBLANES - s
        else:
            keep = rows >= s
            shift = s
        a_sh = jnp.where(keep, pltpu.roll(a, shift, 0), 1.0)
        u_sh = jnp.where(keep, pltpu.roll(u, shift, 0), 0.0)
        u = a * u_sh + u
        a = a * a_sh
    h = a * h_prev + u
    carry = h[0:1, :] if reverse else h[SUBLANES - 1:SUBLANES, :]
    return h, carry


def _rglru_scan(af_ref, uf_ref, ab_ref, ub_ref, h0_ref, st_ref, n_seq, seq_len):
    n_blocks = seq_len // SUBLANES
    rows = lax.broadcasted_iota(jnp.int32, (SUBLANES, D_RNN), 0)
    init = tuple(h0_ref[0, 0, r:r + 1, :] for r in range(2 * n_seq))

    def body(jb, carries):
        out = []
        for s in range(n_seq):
            f0 = pl.multiple_of(s * seq_len + jb * SUBLANES, SUBLANES)
            b0 = pl.multiple_of(s * seq_len + seq_len - SUBLANES - jb * SUBLANES, SUBLANES)
            hf, cf = _scan_block(af_ref[pl.ds(f0, SUBLANES), :], uf_ref[pl.ds(f0, SUBLANES), :],
                                 carries[2 * s], rows, False)
            uf_ref[pl.ds(f0, SUBLANES), :] = hf
            hb, cb = _scan_block(ab_ref[pl.ds(b0, SUBLANES), :], ub_ref[pl.ds(b0, SUBLANES), :],
                                 carries[2 * s + 1], rows, True)
            ub_ref[pl.ds(b0, SUBLANES), :] = hb
            out += [cf, cb]
        return tuple(out)

    final = lax.fori_loop(0, n_blocks, body, init)
    for r in range(SUBLANES):
        st_ref[0, 0, r:r + 1, :] = final[r] if r < 2 * n_seq else jnp.zeros((1, D_RNN), F32)


def _rnn_body(l_ref, zx_ref, zg_ref, cw_ref, cb_ref, rgw_ref, rgb_ref, lam_ref, h0_ref,
              ya_ref, st_ref, pad_ref, xc_ref, af_ref, uf_ref, ab_ref, ub_ref):
    i = pl.program_id(0)
    is_ctx = i < CTX_TILES

    _store_regions(pad_ref, RNN_HALO, zx_ref[...].astype(F32))
    _fill_halos(pad_ref, RNN_HALO, is_ctx)

    def emit_xc(row0, lb, accs):
        lanes = slice(lb * LANES, (lb + 1) * LANES)
        xc_ref[pl.ds(row0, CONV_RB), lanes] = accs[0] + cb_ref[0, :, lanes]

    _dwconv([(pad_ref, lambda k, lanes: cw_ref[0, k:k + 1, lanes])], RNN_HALO,
            [k - 2 for k in range(RNN_CONV)], emit_xc)

    neg_lam = -lam_ref[0]
    softplus = jnp.maximum(neg_lam, 0.0) + jnp.log1p(jnp.exp(-jnp.abs(neg_lam)))
    a_refs = (af_ref, ab_ref)
    u_refs = (uf_ref, ub_ref)
    n_col = D_RNN // MXU_DIM

    def gate_body(rb, carry):
        rows = pl.ds(pl.multiple_of(rb * GATE_RB, GATE_RB), GATE_RB)
        for j in range(n_col):
            lanes = slice(j * MXU_DIM, (j + 1) * MXU_DIM)
            xc = xc_ref[rows, lanes]
            xcb = xc.astype(BF16)
            for d in range(2):
                base = d * 2 * n_col
                g_r = jnp.dot(xcb, rgw_ref[0, base + j], preferred_element_type=F32)
                g_i = jnp.dot(xcb, rgw_ref[0, base + n_col + j], preferred_element_type=F32)
                r = _sigmoid(g_r + rgb_ref[0, 2 * d:2 * d + 1, lanes])
                gi = _sigmoid(g_i + rgb_ref[0, 2 * d + 1:2 * d + 2, lanes])
                a = jnp.exp((-C_RG) * r * softplus[d:d + 1, lanes])
                a_refs[d][rows, lanes] = a
                u_refs[d][rows, lanes] = jnp.sqrt(1.0 - a * a) * gi * xc
        return carry

    lax.fori_loop(0, TM // GATE_RB, gate_body, 0)

    @pl.when(is_ctx)
    def _():
        _rglru_scan(af_ref, uf_ref, ab_ref, ub_ref, h0_ref, st_ref, TM // SEQ, SEQ)

    @pl.when(jnp.logical_not(is_ctx))
    def _():
        _rglru_scan(af_ref, uf_ref, ab_ref, ub_ref, h0_ref, st_ref, TM // DEC_SEQ, DEC_SEQ)

    def out_body(rb, carry):
        rows = pl.ds(pl.multiple_of(rb * OUT_RB, OUT_RB), OUT_RB)
        h = uf_ref[rows, :] + ub_ref[rows, :]
        ya_ref[rows, :] = (h * jax.nn.gelu(zg_ref[rows, :].astype(F32))).astype(ya_ref.dtype)
        return carry

    lax.fori_loop(0, TM // OUT_RB, out_body, 0)


def _rnn(l, z, rnn_conv_w, rnn_conv_b3, rgw, rg_b4, rg_lambda, h0_all):
    n_rgw = 2 * 2 * (D_RNN // MXU_DIM)
    return _layer_call(
        _rnn_body,
        grid=(N_TILES,),
        in_specs=[pl.BlockSpec((TM, D_RNN), lambda i, l: (i, 0)),
                  pl.BlockSpec((TM, D_RNN), lambda i, l: (i, 1)),
                  pl.BlockSpec((1, RNN_CONV, D_RNN), lambda i, l: (l[0], 0, 0)),
                  pl.BlockSpec((1, 1, D_RNN), lambda i, l: (l[0], 0, 0)),
                  pl.BlockSpec((1, n_rgw, MXU_DIM, MXU_DIM), lambda i, l: (l[0], 0, 0, 0)),
                  pl.BlockSpec((1, 4, D_RNN), lambda i, l: (l[0], 0, 0)),
                  pl.BlockSpec((1, 2, D_RNN), lambda i, l: (l[0], 0, 0)),
                  pl.BlockSpec((1, 1, SUBLANES, D_RNN), lambda i, l: (l[0], i, 0, 0))],
        out_specs=[pl.BlockSpec((TM, D_MODEL), lambda i, l: (i, 0)),
                   pl.BlockSpec((1, 1, SUBLANES, D_RNN), lambda i, l: (0, i, 0, 0))],
        out_shape=[jax.ShapeDtypeStruct((N_TOK, D_RNN), BF16),
                   jax.ShapeDtypeStruct((1, N_TILES, SUBLANES, D_RNN), F32)],
        scratch_shapes=[pltpu.VMEM((N_REGIONS * _region_stride(RNN_HALO), D_RNN), F32),
                        pltpu.VMEM((TM, D_RNN), F32),
                        pltpu.VMEM((TM, D_RNN), F32),
                        pltpu.VMEM((TM, D_RNN), F32),
                        pltpu.VMEM((TM, D_RNN), F32),
                        pltpu.VMEM((TM, D_RNN), F32)],
        name="rnn")(l, z, z, rnn_conv_w, rnn_conv_b3, rgw, rg_b4, rg_lambda, h0_all)


N_CHUNKS = TM // CHUNK


def _sgu_body(l_ref, z_ref, g_ref, w_ref, bt_ref, yb_ref, su_ref, sv_ref):
    gain = g_ref[0]

    def act_body(rb, carry):
        rows = pl.ds(pl.multiple_of(rb * OUT_RB, OUT_RB), OUT_RB)
        uv = jax.nn.gelu(z_ref[rows, :].astype(F32))
        su_ref[rows, :] = uv[:, :D_SG]
        sv = uv[:, D_SG:]
        mu = jnp.mean(sv, axis=-1, keepdims=True)
        svc = sv - mu
        y = svc * lax.rsqrt(jnp.mean(svc * svc, axis=-1, keepdims=True) + EPS) * gain
        sv_ref[rows, :] = y.astype(sv_ref.dtype)
        return carry

    lax.fori_loop(0, TM // OUT_RB, act_body, 0)

    for g in range(SG_GROUPS):
        lanes = slice(g * SG_GD, (g + 1) * SG_GD)
        rhs = jnp.concatenate([sv_ref[n * CHUNK:(n + 1) * CHUNK, lanes] for n in range(N_CHUNKS)],
                              axis=1)
        mixed = jnp.dot(w_ref[0, g].astype(BF16), rhs, preferred_element_type=F32)
        bias = bt_ref[0, :, g:g + 1]
        for n in range(N_CHUNKS):
            rows = slice(n * CHUNK, (n + 1) * CHUNK)
            m = mixed[:, n * SG_GD:(n + 1) * SG_GD] + bias
            yb_ref[rows, lanes] = (su_ref[rows, lanes] * m).astype(yb_ref.dtype)


def _sgu(l, z, sg_norm_g3, sg_w, sg_bt):
    return _layer_call(
        _sgu_body,
        grid=(N_TILES,),
        in_specs=[pl.BlockSpec((TM, 2 * D_SG), lambda i, l: (i, 1)),
                  pl.BlockSpec((1, 1, D_SG), lambda i, l: (l[0], 0, 0)),
                  pl.BlockSpec((1, SG_GROUPS, CHUNK, CHUNK), lambda i, l: (l[0], 0, 0, 0)),
                  pl.BlockSpec((1, CHUNK, SG_GROUPS), lambda i, l: (l[0], 0, 0))],
        out_specs=pl.BlockSpec((TM, D_SG), lambda i, l: (i, 0)),
        out_shape=jax.ShapeDtypeStruct((N_TOK, D_SG), BF16),
        scratch_shapes=[pltpu.VMEM((TM, D_SG), F32), pltpu.VMEM((TM, D_SG), BF16)],
        name="sgu")(l, z, sg_norm_g3, sg_w, sg_bt)


CF_HALO = 16


def _conformer_body(l_ref, z_ref, cw_ref, cb_ref, g_ref, b_ref, yc_ref, pad_ref, blk_ref):
    i = pl.program_id(0)
    rs = _region_stride(CF_HALO)

    def glu_body(rb, carry):
        s = rb // (REGION // OUT_RB)
        src = pl.ds(pl.multiple_of(rb * OUT_RB, OUT_RB), OUT_RB)
        dst = pl.ds(pl.multiple_of(rb * OUT_RB + (2 * s + 1) * CF_HALO, SUBLANES), OUT_RB)
        zz = z_ref[src, :].astype(F32)
        pad_ref[dst, :] = zz[:, :D_CF] * _sigmoid(zz[:, D_CF:])
        return carry

    lax.fori_loop(0, TM // OUT_RB, glu_body, 0)
    _fill_halos(pad_ref, CF_HALO, i < CTX_TILES)

    def emit(row0, lb, accs):
        lanes = slice(lb * LANES, (lb + 1) * LANES)
        blk_ref[:, lanes] = accs[0] + cb_ref[0, :, lanes]

    def post(row0):
        y = blk_ref[...]
        mu = jnp.mean(y, axis=-1, keepdims=True)
        yc = y - mu
        ln = yc * lax.rsqrt(jnp.mean(yc * yc, axis=-1, keepdims=True) + EPS) * g_ref[0] + b_ref[0]
        yc_ref[pl.ds(row0, CONV_RB), :] = (ln * _sigmoid(ln)).astype(yc_ref.dtype)

    _dwconv([(pad_ref, lambda k, lanes: cw_ref[0, k:k + 1, lanes])], CF_HALO,
            [k - CF_CONV // 2 for k in range(CF_CONV)], emit, post)


def _conformer(l, z, cf_conv_w, cf_conv_b3, cf_ln_g3, cf_ln_b3):
    return _layer_call(
        _conformer_body,
        grid=(N_TILES,),
        in_specs=[pl.BlockSpec((TM, 2 * D_CF), lambda i, l: (i, 2)),
                  pl.BlockSpec((1, CF_CONV, D_CF), lambda i, l: (l[0], 0, 0)),
                  pl.BlockSpec((1, 1, D_CF), lambda i, l: (l[0], 0, 0)),
                  pl.BlockSpec((1, 1, D_CF), lambda i, l: (l[0], 0, 0)),
                  pl.BlockSpec((1, 1, D_CF), lambda i, l: (l[0], 0, 0))],
        out_specs=pl.BlockSpec((TM, D_CF), lambda i, l: (i, 0)),
        out_shape=jax.ShapeDtypeStruct((N_TOK, D_CF), BF16),
        scratch_shapes=[pltpu.VMEM((N_REGIONS * _region_stride(CF_HALO), D_CF), F32),
                        pltpu.VMEM((CONV_RB, D_CF), F32)],
        name="conformer")(l, z, cf_conv_w, cf_conv_b3, cf_ln_g3, cf_ln_b3)


def _merge_body(l_ref, x_ref, ya_ref, yb_ref, yc_ref, zg_ref, mod_ref, gn_ref, wb_ref, wo_ref,
                o_ref):
    i = pl.program_id(0)
    merged = None
    for p, y_ref in enumerate((ya_ref, yb_ref, yc_ref)):
        gate = _sigmoid(zg_ref[:, p * D_MODEL:(p + 1) * D_MODEL].astype(F32))
        term = gate * jnp.dot(y_ref[...], wb_ref[0, p], preferred_element_type=F32)
        merged = term if merged is None else merged + term
    out = jnp.dot(merged.astype(BF16), wo_ref[0], preferred_element_type=F32)
    m = mod_ref[0, _mod_row_of_tile(i, TM_MERGE)]
    gate1 = m[:, 2 * D_MODEL:3 * D_MODEL]
    o_ref[...] = x_ref[...] + gate1 * _rms_rows(out, gn_ref[0, 1:2, :])


def _merge(l, x, ya, yb, yc, z, mod4, g_norm, wb_bf16, wo_bf16):
    tile = lambda i, l: (i, 0)
    return _layer_call(
        _merge_body,
        grid=(N_TOK // TM_MERGE,),
        in_specs=[pl.BlockSpec((TM_MERGE, D_MODEL), tile),
                  pl.BlockSpec((TM_MERGE, D_MODEL), tile),
                  pl.BlockSpec((TM_MERGE, D_MODEL), tile),
                  pl.BlockSpec((TM_MERGE, D_MODEL), tile),
                  pl.BlockSpec((TM_MERGE, 3 * D_MODEL), lambda i, l: (i, 2)),
                  pl.BlockSpec((1, MOD_ROWS, 1, 6 * D_MODEL), lambda i, l: (l[0], 0, 0, 0)),
                  pl.BlockSpec((1, 4, D_MODEL), lambda i, l: (l[0], 0, 0)),
                  pl.BlockSpec((1, 3, D_MODEL, D_MODEL), lambda i, l: (l[0], 0, 0, 0)),
                  pl.BlockSpec((1, D_MODEL, D_MODEL), lambda i, l: (l[0], 0, 0))],
        out_specs=pl.BlockSpec((TM_MERGE, D_MODEL), tile),
        out_shape=jax.ShapeDtypeStruct((N_TOK, D_MODEL), F32),
        scratch_shapes=[],
        name="merge")(l, x, ya, yb, yc, z, mod4, g_norm, wb_bf16, wo_bf16)


FFN_HALO = 8
N_FF_STEPS = D_FF // FF_CW


def _ffn_body(l_ref, x_ref, mod_ref, gn_ref, wg_ref, wv_ref, cwg_ref, cwv_ref, cbg_ref, cbv_ref,
              wd_ref, o_ref, h_ref, acc_ref, padg_ref, padv_ref, act_ref):
    i = pl.program_id(0)
    c = pl.program_id(1)
    m = mod_ref[0, _mod_row_of_tile(i, TM)]

    @pl.when(c == 0)
    def _():
        _modulated_norm_to(h_ref, x_ref, gn_ref[0, 2:3, :],
                           m[:, 4 * D_MODEL:5 * D_MODEL], m[:, 3 * D_MODEL:4 * D_MODEL])
        acc_ref[...] = jnp.zeros_like(acc_ref)

    h = h_ref[...]
    _store_regions(padg_ref, FFN_HALO,
                   jnp.dot(h, wg_ref[0].astype(BF16), preferred_element_type=F32))
    _store_regions(padv_ref, FFN_HALO,
                   jnp.dot(h, wv_ref[0].astype(BF16), preferred_element_type=F32))
    is_ctx = i < CTX_TILES
    _fill_halos(padg_ref, FFN_HALO, is_ctx)
    _fill_halos(padv_ref, FFN_HALO, is_ctx)

    def emit(row0, lb, accs):
        lanes = slice(lb * LANES, (lb + 1) * LANES)
        fg = accs[0] + cbg_ref[0, :, lanes]
        fv = accs[1] + cbv_ref[0, :, lanes]
        act_ref[pl.ds(row0, CONV_RB), lanes] = (jax.nn.gelu(fg) * fv).astype(act_ref.dtype)

    _dwconv([(padg_ref, lambda k, lanes: cwg_ref[0, k:k + 1, lanes]),
             (padv_ref, lambda k, lanes: cwv_ref[0, k:k + 1, lanes])],
            FFN_HALO, [k - FFN_CONV // 2 for k in range(FFN_CONV)], emit)

    acc_ref[...] += jnp.dot(act_ref[...], wd_ref[0].astype(BF16), preferred_element_type=F32)

    @pl.when(c == N_FF_STEPS - 1)
    def _():
        gate2 = m[:, 5 * D_MODEL:6 * D_MODEL]
        g3 = gn_ref[0, 3:4, :]

        def body(rb, carry):
            rows = pl.ds(pl.multiple_of(rb * NORM_RB, NORM_RB), NORM_RB)
            o_ref[rows, :] = x_ref[rows, :] + gate2 * _rms_rows(acc_ref[rows, :], g3)
            return carry
        lax.fori_loop(0, TM // NORM_RB, body, 0)


def _ffn(l, x, mod4, g_norm, ffn_up, ffn_conv_w, ffn_conv_b3, ffn_down):
    return _layer_call(
        _ffn_body,
        grid=(N_TILES, N_FF_STEPS),
        in_specs=[pl.BlockSpec((TM, D_MODEL), lambda i, c, l: (i, 0)),
                  pl.BlockSpec((1, MOD_ROWS, 1, 6 * D_MODEL), lambda i, c, l: (l[0], 0, 0, 0)),
                  pl.BlockSpec((1, 4, D_MODEL), lambda i, c, l: (l[0], 0, 0)),
                  pl.BlockSpec((1, D_MODEL, FF_CW), lambda i, c, l: (l[0], 0, c)),
                  pl.BlockSpec((1, D_MODEL, FF_CW), lambda i, c, l: (l[0], 0, N_FF_STEPS + c)),
                  pl.BlockSpec((1, FFN_CONV, FF_CW), lambda i, c, l: (l[0], 0, c)),
                  pl.BlockSpec((1, FFN_CONV, FF_CW), lambda i, c, l: (l[0], 0, N_FF_STEPS + c)),
                  pl.BlockSpec((1, 1, FF_CW), lambda i, c, l: (l[0], 0, c)),
                  pl.BlockSpec((1, 1, FF_CW), lambda i, c, l: (l[0], 0, N_FF_STEPS + c)),
                  pl.BlockSpec((1, FF_CW, D_MODEL), lambda i, c, l: (l[0], c, 0))],
        out_specs=pl.BlockSpec((TM, D_MODEL), lambda i, c, l: (i, 0)),
        out_shape=jax.ShapeDtypeStruct((N_TOK, D_MODEL), F32),
        scratch_shapes=[pltpu.VMEM((TM, D_MODEL), BF16),
                        pltpu.VMEM((TM, D_MODEL), F32),
                        pltpu.VMEM((N_REGIONS * _region_stride(FFN_HALO), FF_CW), F32),
                        pltpu.VMEM((N_REGIONS * _region_stride(FFN_HALO), FF_CW), F32),
                        pltpu.VMEM((TM, FF_CW), BF16)],
        name="ffn")(l, x, mod4, g_norm, ffn_up, ffn_up, ffn_conv_w, ffn_conv_w,
                    ffn_conv_b3, ffn_conv_b3, ffn_down)


def _pos_table():
    t = jnp.arange(DEC_SEQ)
    r = (t // GRID_W).astype(F32)
    col = (t % GRID_W).astype(F32)
    q = D_MODEL // 4
    omega = 1.0 / (POS_BASE ** (jnp.arange(q, dtype=F32) / q))

    def emb(p):
        ang = p[:, None] * omega[None, :]
        return jnp.concatenate([jnp.sin(ang), jnp.cos(ang)], axis=-1)
    return jnp.concatenate([emb(r), emb(col)], axis=-1).astype(F32)


def _block_diag_gates(rg_w):
    per_tile = MXU_DIM // BS_RNN
    n_col = H_RNN // per_tile
    w = rg_w.reshape(DEPTH, 2, 2, n_col, per_tile, BS_RNN, BS_RNN)
    eye = jnp.eye(per_tile, dtype=rg_w.dtype)
    tiles = jnp.einsum('ldkcaij,ab->ldkcaibj', w, eye)
    return tiles.reshape(DEPTH, 2 * 2 * n_col, MXU_DIM, MXU_DIM).astype(BF16)


def kernel(x_prompt, x_sample, state_rglru, c, c_ctx, w_mod, b_mod, g_norm, w_in, b_in,
           rnn_conv_w, rnn_conv_b, rg_w, rg_b, rg_lambda, sg_norm_g, sg_w, sg_b,
           cf_conv_w, cf_conv_b, cf_ln_g, cf_ln_b, w_branch, w_out, ffn_up, ffn_conv_w,
           ffn_conv_b, ffn_down):
    x = _prep(x_prompt.reshape(BATCH * SEQ, D_MODEL), x_sample.reshape(DEC_BATCH * DEC_SEQ, D_MODEL),
              _pos_table())

    cond = jnp.zeros((MOD_ROWS, D_MODEL), F32).at[:DEC_BATCH].set(c).at[CTX_MOD_ROW].set(c_ctx)
    mod4 = _mod(cond, w_mod, b_mod).reshape(DEPTH, MOD_ROWS, 1, 6 * D_MODEL)

    h0_all = jnp.zeros((DEPTH, N_TILES, SUBLANES, D_RNN), F32)
    h0_all = h0_all.at[:, CTX_TILES:, 0:2, :].set(jnp.transpose(state_rglru.astype(F32), (1, 0, 2, 3)))

    rgw = _block_diag_gates(rg_w)
    wb_bf16 = w_branch.astype(BF16)
    wo_bf16 = w_out.astype(BF16)
    b_in3 = b_in.reshape(DEPTH, 1, N_IN)
    rnn_conv_b3 = rnn_conv_b.reshape(DEPTH, 1, D_RNN)
    rg_b4 = rg_b.reshape(DEPTH, 4, D_RNN)
    sg_norm_g3 = sg_norm_g.reshape(DEPTH, 1, D_SG)
    sg_bt = jnp.transpose(sg_b, (0, 2, 1))
    cf_conv_b3 = cf_conv_b.reshape(DEPTH, 1, D_CF)
    cf_ln_g3 = cf_ln_g.reshape(DEPTH, 1, D_CF)
    cf_ln_b3 = cf_ln_b.reshape(DEPTH, 1, D_CF)
    ffn_conv_b3 = ffn_conv_b.reshape(DEPTH, 1, 2 * D_FF)

    def layer(x, l):
        z = _inproj(l, x, mod4, g_norm, w_in, b_in3)
        ya, st = _rnn(l, z, rnn_conv_w, rnn_conv_b3, rgw, rg_b4, rg_lambda, h0_all)
        yb = _sgu(l, z, sg_norm_g3, sg_w, sg_bt)
        yc = _conformer(l, z, cf_conv_w, cf_conv_b3, cf_ln_g3, cf_ln_b3)
        x = _merge(l, x, ya, yb, yc, z, mod4, g_norm, wb_bf16, wo_bf16)
        x = _ffn(l, x, mod4, g_norm, ffn_up, ffn_conv_w, ffn_conv_b3, ffn_down)
        return x, st[0, :CTX_TILES]

    x, states = lax.scan(layer, x, jnp.arange(DEPTH, dtype=jnp.int32).reshape(DEPTH, 1))

    y_prompt = x[:BATCH * SEQ].reshape(BATCH, SEQ, D_MODEL)
    y_sample = x[BATCH * SEQ:].reshape(DEC_BATCH, DEC_SEQ, D_MODEL)
    new_state = states.reshape(DEPTH, BATCH, 2, D_RNN).transpose(1, 0, 2, 3)
    return (y_prompt, y_sample, new_state)
```

```python
import functools

import jax
import jax.numpy as jnp
import numpy as np
from jax import lax
from jax.experimental import pallas as pl
from jax.experimental.pallas import tpu as pltpu

F32 = jnp.float32
BF16 = jnp.bfloat16

D_MODEL = 1024
BATCH = 16
SEQ = 256
DEPTH = 4
DEC_BATCH = 4
DEC_SEQ = 1024
GRID_W = 64
D_RNN = 1024
H_RNN = 16
BS_RNN = D_RNN // H_RNN
RNN_CONV = 4
C_RG = 8.0
D_SG = 1024
SG_GROUPS = 8
SG_GD = D_SG // SG_GROUPS
CHUNK = 128
D_CF = 1024
CF_CONV = 31
D_FF = 4096
FFN_CONV = 3
N_IN = 2 * D_RNN + 2 * D_SG + 2 * D_CF + 3 * D_MODEL
EPS = 1e-6
POS_BASE = 10000.0

LANES = 128
SUBLANES = 8
MXU_DIM = 256

N_TOK = BATCH * SEQ + DEC_BATCH * DEC_SEQ
TM = 1024
N_TILES = N_TOK // TM
CTX_TILES = BATCH * SEQ // TM
REGION = SEQ
N_REGIONS = TM // REGION
MOD_ROWS = 8
CTX_MOD_ROW = DEC_BATCH
CONV_RB = 64
FF_CW = 512
TM_MERGE = 512
VMEM_LIMIT = 56 * 1024 * 1024


def _sigmoid(x):
    return 0.5 * (jnp.tanh(0.5 * x) + 1.0)


def _rms_rows(x, g):
    return x * lax.rsqrt(jnp.mean(x * x, axis=-1, keepdims=True) + EPS) * g


def _mod_row_of_tile(i, tile_rows):
    tiles_per_latent = DEC_SEQ // tile_rows
    ctx_tiles = BATCH * SEQ // tile_rows
    lat = jnp.maximum(i - ctx_tiles, 0) // tiles_per_latent
    return jnp.where(i < ctx_tiles, CTX_MOD_ROW, lat)


def _params(n_grid):
    return pltpu.CompilerParams(dimension_semantics=("arbitrary",) * n_grid,
                                vmem_limit_bytes=VMEM_LIMIT)


def _layer_call(body, grid, in_specs, out_specs, out_shape, scratch_shapes, name):
    return pl.pallas_call(
        body,
        grid_spec=pltpu.PrefetchScalarGridSpec(
            num_scalar_prefetch=1, grid=grid, in_specs=in_specs, out_specs=out_specs,
            scratch_shapes=scratch_shapes),
        out_shape=out_shape,
        compiler_params=_params(len(grid)),
        name=name)


def _region_stride(halo):
    return REGION + 2 * halo


def _pad_shape(halo, width):
    return (width // LANES, N_REGIONS * _region_stride(halo), LANES)


def _fill_halos(pad_ref, halo, is_ctx):
    rs = _region_stride(halo)
    zeros = jnp.zeros((halo, LANES), F32)
    for lb in range(pad_ref.shape[0]):
        for s in range(N_REGIONS):
            lo = s * rs
            if s == 0:
                before = zeros
            else:
                prev_end = (s - 1) * rs + halo + REGION
                before = jnp.where(is_ctx, 0.0, pad_ref[lb, prev_end - halo:prev_end, :])
            pad_ref[lb, lo:lo + halo, :] = before
            if s == N_REGIONS - 1:
                after = zeros
            else:
                next_start = (s + 1) * rs + halo
                after = jnp.where(is_ctx, 0.0, pad_ref[lb, next_start:next_start + halo, :])
            pad_ref[lb, lo + halo + REGION:lo + 2 * halo + REGION, :] = after


FILL_RB = 64


def _fill_regions(pad_ref, halo, produce):
    per_region = REGION // FILL_RB

    def body(rb, carry):
        s = rb // per_region
        src = pl.ds(pl.multiple_of(rb * FILL_RB, FILL_RB), FILL_RB)
        dst = pl.ds(pl.multiple_of(rb * FILL_RB + (2 * s + 1) * halo, SUBLANES), FILL_RB)
        value = produce(src)
        for lb in range(pad_ref.shape[0]):
            pad_ref[lb, dst, :] = value[:, lb * LANES:(lb + 1) * LANES]
        return carry

    lax.fori_loop(0, TM // FILL_RB, body, 0)


def _dwconv(pads, halo, offsets, emit, post=None):
    rs = _region_stride(halo)
    n_rb = REGION // CONV_RB

    def body(idx, carry):
        s = idx // n_rb
        rb = idx - s * n_rb
        wbase = pl.multiple_of(s * rs + rb * CONV_RB, SUBLANES)
        obase = pl.multiple_of(idx * CONV_RB, CONV_RB)
        for lb in range(pads[0][0].shape[0]):
            lanes = slice(lb * LANES, (lb + 1) * LANES)
            accs = []
            for pad_ref, tap_fn in pads:
                acc = None
                for k, d in enumerate(offsets):
                    term = pad_ref[lb, pl.ds(wbase + (halo + d), CONV_RB), :] * tap_fn(k, lanes)
                    acc = term if acc is None else acc + term
                accs.append(acc)
            emit(obase, lb, accs)
        if post is not None:
            post(obase)
        return carry

    lax.fori_loop(0, N_REGIONS * n_rb, body, 0)


def _prep_body(xp_ref, xs_ref, pos_ref, o_ref):
    i = pl.program_id(0)

    @pl.when(i < CTX_TILES)
    def _():
        o_ref[...] = xp_ref[...]

    @pl.when(i >= CTX_TILES)
    def _():
        o_ref[...] = xs_ref[...] + pos_ref[...]


def _prep(xp, xs, pos):
    return pl.pallas_call(
        _prep_body,
        grid=(N_TILES,),
        in_specs=[pl.BlockSpec((TM, D_MODEL), lambda i: (jnp.minimum(i, CTX_TILES - 1), 0)),
                  pl.BlockSpec((TM, D_MODEL), lambda i: (jnp.maximum(i - CTX_TILES, 0), 0)),
                  pl.BlockSpec((DEC_SEQ, D_MODEL), lambda i: (0, 0))],
        out_specs=pl.BlockSpec((TM, D_MODEL), lambda i: (i, 0)),
        out_shape=jax.ShapeDtypeStruct((N_TOK, D_MODEL), F32),
        compiler_params=_params(1),
        name="prep")(xp, xs, pos)


MOD_TN = 1536


def _mod_body(cond_ref, w_ref, b_ref, o_ref):
    c = cond_ref[...]
    s = (c * _sigmoid(c)).astype(BF16)
    o_ref[0] = jnp.dot(s, w_ref[0].astype(BF16), preferred_element_type=F32) + b_ref[0]


def _mod(cond, w_mod, b_mod):
    n_mod = 6 * D_MODEL
    return pl.pallas_call(
        _mod_body,
        grid=(DEPTH, n_mod // MOD_TN),
        in_specs=[pl.BlockSpec((MOD_ROWS, D_MODEL), lambda l, j: (0, 0)),
                  pl.BlockSpec((1, D_MODEL, MOD_TN), lambda l, j: (l, 0, j)),
                  pl.BlockSpec((1, 1, MOD_TN), lambda l, j: (l, 0, j))],
        out_specs=pl.BlockSpec((1, MOD_ROWS, MOD_TN), lambda l, j: (l, 0, j)),
        out_shape=jax.ShapeDtypeStruct((DEPTH, MOD_ROWS, n_mod), F32),
        compiler_params=_params(2),
        name="mod")(cond, w_mod, b_mod.reshape(DEPTH, 1, n_mod))


IN_TN = 1024
NORM_RB = 64


def _modulated_norm_to(h_ref, x_ref, g, scale, shift):
    def body(rb, carry):
        rows = pl.ds(pl.multiple_of(rb * NORM_RB, NORM_RB), NORM_RB)
        h = _rms_rows(x_ref[rows, :], g) * (1.0 + scale) + shift
        h_ref[rows, :] = h.astype(h_ref.dtype)
        return carry
    lax.fori_loop(0, x_ref.shape[0] // NORM_RB, body, 0)


def _inproj_body(l_ref, x_ref, mod_ref, gn_ref, w_ref, b_ref, z_ref, h_ref):
    i = pl.program_id(0)
    j = pl.program_id(1)

    @pl.when(j == 0)
    def _():
        m = mod_ref[0, _mod_row_of_tile(i, TM)]
        _modulated_norm_to(h_ref, x_ref, gn_ref[0, 0:1, :],
                           m[:, D_MODEL:2 * D_MODEL], m[:, 0:D_MODEL])

    acc = jnp.dot(h_ref[...], w_ref[0].astype(BF16), preferred_element_type=F32)
    z_ref[...] = (acc + b_ref[0]).astype(z_ref.dtype)


def _inproj(l, x, mod4, g_norm, w_in, b_in3):
    return _layer_call(
        _inproj_body,
        grid=(N_TILES, N_IN // IN_TN),
        in_specs=[pl.BlockSpec((TM, D_MODEL), lambda i, j, l: (i, 0)),
                  pl.BlockSpec((1, MOD_ROWS, 1, 6 * D_MODEL), lambda i, j, l: (l[0], 0, 0, 0)),
                  pl.BlockSpec((1, 4, D_MODEL), lambda i, j, l: (l[0], 0, 0)),
                  pl.BlockSpec((1, D_MODEL, IN_TN), lambda i, j, l: (l[0], 0, j)),
                  pl.BlockSpec((1, 1, IN_TN), lambda i, j, l: (l[0], 0, j))],
        out_specs=pl.BlockSpec((TM, IN_TN), lambda i, j, l: (i, j)),
        out_shape=jax.ShapeDtypeStruct((N_TOK, N_IN), BF16),
        scratch_shapes=[pltpu.VMEM((TM, D_MODEL), BF16)],
        name="inproj")(l, x, mod4, g_norm, w_in, b_in3)


RNN_HALO = 8
GATE_RB = 256
OUT_RB = 64


def _scan_block(a, u, h_prev, rows, reverse):
    for s in (1, 2, 4):
        if reverse:
            keep = rows < SUBLANES - s
            shift = SUBLANES - s
        else:
            keep = rows >= s
            shift = s
        a_sh = jnp.where(keep, pltpu.roll(a, shift, 0), 1.0)
        u_sh = jnp.where(keep, pltpu.roll(u, shift, 0), 0.0)
        u = a * u_sh + u
        a = a * a_sh
    h = a * h_prev + u
    carry = h[0:1, :] if reverse else h[SUBLANES - 1:SUBLANES, :]
    return h, carry


def _rglru_scan(af_ref, uf_ref, ab_ref, ub_ref, h0_ref, st_ref, n_seq, seq_len):
    n_blocks = seq_len // SUBLANES
    rows = lax.broadcasted_iota(jnp.int32, (SUBLANES, D_RNN), 0)
    init = tuple(h0_ref[0, 0, r:r + 1, :] for r in range(2 * n_seq))

    def body(jb, carries):
        out = []
        for s in range(n_seq):
            f0 = pl.multiple_of(s * seq_len + jb * SUBLANES, SUBLANES)
            b0 = pl.multiple_of(s * seq_len + seq_len - SUBLANES - jb * SUBLANES, SUBLANES)
            hf, cf = _scan_block(af_ref[pl.ds(f0, SUBLANES), :], uf_ref[pl.ds(f0, SUBLANES), :],
                                 carries[2 * s], rows, False)
            uf_ref[pl.ds(f0, SUBLANES), :] = hf
            hb, cb = _scan_block(ab_ref[pl.ds(b0, SUBLANES), :], ub_ref[pl.ds(b0, SUBLANES), :],
                                 carries[2 * s + 1], rows, True)
            ub_ref[pl.ds(b0, SUBLANES), :] = hb
            out += [cf, cb]
        return tuple(out)

    final = lax.fori_loop(0, n_blocks, body, init)
    for r in range(SUBLANES):
        st_ref[0, 0, r:r + 1, :] = final[r] if r < 2 * n_seq else jnp.zeros((1, D_RNN), F32)


def _rnn_body(l_ref, zx_ref, zg_ref, cw_ref, cb_ref, rgw_ref, rgb_ref, lam_ref, h0_ref,
              ya_ref, st_ref, pad_ref, xc_ref, af_ref, uf_ref, ab_ref, ub_ref):
    i = pl.program_id(0)
    is_ctx = i < CTX_TILES

    _fill_regions(pad_ref, RNN_HALO, lambda rows: zx_ref[rows, :].astype(F32))
    _fill_halos(pad_ref, RNN_HALO, is_ctx)

    def emit_xc(row0, lb, accs):
        lanes = slice(lb * LANES, (lb + 1) * LANES)
        xc_ref[pl.ds(row0, CONV_RB), lanes] = accs[0] + cb_ref[0, :, lanes]

    _dwconv([(pad_ref, lambda k, lanes: cw_ref[0, k:k + 1, lanes])], RNN_HALO,
            [k - 2 for k in range(RNN_CONV)], emit_xc)

    neg_lam = -lam_ref[0]
    softplus = jnp.maximum(neg_lam, 0.0) + jnp.log1p(jnp.exp(-jnp.abs(neg_lam)))
    a_refs = (af_ref, ab_ref)
    u_refs = (uf_ref, ub_ref)
    n_col = D_RNN // MXU_DIM

    def gate_body(rb, carry):
        rows = pl.ds(pl.multiple_of(rb * GATE_RB, GATE_RB), GATE_RB)
        for j in range(n_col):
            lanes = slice(j * MXU_DIM, (j + 1) * MXU_DIM)
            xc = xc_ref[rows, lanes]
            xcb = xc.astype(BF16)
            for d in range(2):
                base = d * 2 * n_col
                g_r = jnp.dot(xcb, rgw_ref[0, base + j], preferred_element_type=F32)
                g_i = jnp.dot(xcb, rgw_ref[0, base + n_col + j], preferred_element_type=F32)
                r = _sigmoid(g_r + rgb_ref[0, 2 * d:2 * d + 1, lanes])
                gi = _sigmoid(g_i + rgb_ref[0, 2 * d + 1:2 * d + 2, lanes])
                a = jnp.exp((-C_RG) * r * softplus[d:d + 1, lanes])
                a_refs[d][rows, lanes] = a
                u_refs[d][rows, lanes] = jnp.sqrt(1.0 - a * a) * gi * xc
        return carry

    lax.fori_loop(0, TM // GATE_RB, gate_body, 0)

    @pl.when(is_ctx)
    def _():
        _rglru_scan(af_ref, uf_ref, ab_ref, ub_ref, h0_ref, st_ref, TM // SEQ, SEQ)

    @pl.when(jnp.logical_not(is_ctx))
    def _():
        _rglru_scan(af_ref, uf_ref, ab_ref, ub_ref, h0_ref, st_ref, TM // DEC_SEQ, DEC_SEQ)

    def out_body(rb, carry):
        rows = pl.ds(pl.multiple_of(rb * OUT_RB, OUT_RB), OUT_RB)
        h = uf_ref[rows, :] + ub_ref[rows, :]
        ya_ref[rows, :] = (h * jax.nn.gelu(zg_ref[rows, :].astype(F32))).astype(ya_ref.dtype)
        return carry

    lax.fori_loop(0, TM // OUT_RB, out_body, 0)


def _rnn(l, z, rnn_conv_w, rnn_conv_b3, rgw, rg_b4, rg_lambda, h0_all):
    n_rgw = 2 * 2 * (D_RNN // MXU_DIM)
    return _layer_call(
        _rnn_body,
        grid=(N_TILES,),
        in_specs=[pl.BlockSpec((TM, D_RNN), lambda i, l: (i, 0)),
                  pl.BlockSpec((TM, D_RNN), lambda i, l: (i, 1)),
                  pl.BlockSpec((1, RNN_CONV, D_RNN), lambda i, l: (l[0], 0, 0)),
                  pl.BlockSpec((1, 1, D_RNN), lambda i, l: (l[0], 0, 0)),
                  pl.BlockSpec((1, n_rgw, MXU_DIM, MXU_DIM), lambda i, l: (l[0], 0, 0, 0)),
                  pl.BlockSpec((1, 4, D_RNN), lambda i, l: (l[0], 0, 0)),
                  pl.BlockSpec((1, 2, D_RNN), lambda i, l: (l[0], 0, 0)),
                  pl.BlockSpec((1, 1, SUBLANES, D_RNN), lambda i, l: (l[0], i, 0, 0))],
        out_specs=[pl.BlockSpec((TM, D_MODEL), lambda i, l: (i, 0)),
                   pl.BlockSpec((1, 1, SUBLANES, D_RNN), lambda i, l: (0, i, 0, 0))],
        out_shape=[jax.ShapeDtypeStruct((N_TOK, D_RNN), BF16),
                   jax.ShapeDtypeStruct((1, N_TILES, SUBLANES, D_RNN), F32)],
        scratch_shapes=[pltpu.VMEM(_pad_shape(RNN_HALO, D_RNN), F32),
                        pltpu.VMEM((TM, D_RNN), F32),
                        pltpu.VMEM((TM, D_RNN), F32),
                        pltpu.VMEM((TM, D_RNN), F32),
                        pltpu.VMEM((TM, D_RNN), F32),
                        pltpu.VMEM((TM, D_RNN), F32)],
        name="rnn")(l, z, z, rnn_conv_w, rnn_conv_b3, rgw, rg_b4, rg_lambda, h0_all)


N_CHUNKS = TM // CHUNK


def _sgu_body(l_ref, z_ref, g_ref, w_ref, bt_ref, yb_ref, su_ref, sv_ref):
    gain = g_ref[0]

    def act_body(rb, carry):
        rows = pl.ds(pl.multiple_of(rb * OUT_RB, OUT_RB), OUT_RB)
        uv = jax.nn.gelu(z_ref[rows, :].astype(F32))
        su_ref[rows, :] = uv[:, :D_SG]
        sv = uv[:, D_SG:]
        mu = jnp.mean(sv, axis=-1, keepdims=True)
        svc = sv - mu
        y = svc * lax.rsqrt(jnp.mean(svc * svc, axis=-1, keepdims=True) + EPS) * gain
        sv_ref[rows, :] = y.astype(sv_ref.dtype)
        return carry

    lax.fori_loop(0, TM // OUT_RB, act_body, 0)

    for g in range(SG_GROUPS):
        lanes = slice(g * SG_GD, (g + 1) * SG_GD)
        rhs = jnp.concatenate([sv_ref[n * CHUNK:(n + 1) * CHUNK, lanes] for n in range(N_CHUNKS)],
                              axis=1)
        mixed = jnp.dot(w_ref[0, g].astype(BF16), rhs, preferred_element_type=F32)
        bias = bt_ref[0, :, g:g + 1]
        for n in range(N_CHUNKS):
            rows = slice(n * CHUNK, (n + 1) * CHUNK)
            m = mixed[:, n * SG_GD:(n + 1) * SG_GD] + bias
            yb_ref[rows, lanes] = (su_ref[rows, lanes] * m).astype(yb_ref.dtype)


def _sgu(l, z, sg_norm_g3, sg_w, sg_bt):
    return _layer_call(
        _sgu_body,
        grid=(N_TILES,),
        in_specs=[pl.BlockSpec((TM, 2 * D_SG), lambda i, l: (i, 1)),
                  pl.BlockSpec((1, 1, D_SG), lambda i, l: (l[0], 0, 0)),
                  pl.BlockSpec((1, SG_GROUPS, CHUNK, CHUNK), lambda i, l: (l[0], 0, 0, 0)),
                  pl.BlockSpec((1, CHUNK, SG_GROUPS), lambda i, l: (l[0], 0, 0))],
        out_specs=pl.BlockSpec((TM, D_SG), lambda i, l: (i, 0)),
        out_shape=jax.ShapeDtypeStruct((N_TOK, D_SG), BF16),
        scratch_shapes=[pltpu.VMEM((TM, D_SG), F32), pltpu.VMEM((TM, D_SG), BF16)],
        name="sgu")(l, z, sg_norm_g3, sg_w, sg_bt)


CF_HALO = 16


def _conformer_body(l_ref, z_ref, cw_ref, cb_ref, g_ref, b_ref, yc_ref, pad_ref, blk_ref):
    i = pl.program_id(0)

    def glu(rows):
        zz = z_ref[rows, :].astype(F32)
        return zz[:, :D_CF] * _sigmoid(zz[:, D_CF:])

    _fill_regions(pad_ref, CF_HALO, glu)
    _fill_halos(pad_ref, CF_HALO, i < CTX_TILES)

    def emit(row0, lb, accs):
        lanes = slice(lb * LANES, (lb + 1) * LANES)
        blk_ref[:, lanes] = accs[0] + cb_ref[0, :, lanes]

    def post(row0):
        y = blk_ref[...]
        mu = jnp.mean(y, axis=-1, keepdims=True)
        yc = y - mu
        ln = yc * lax.rsqrt(jnp.mean(yc * yc, axis=-1, keepdims=True) + EPS) * g_ref[0] + b_ref[0]
        yc_ref[pl.ds(row0, CONV_RB), :] = (ln * _sigmoid(ln)).astype(yc_ref.dtype)

    _dwconv([(pad_ref, lambda k, lanes: cw_ref[0, k:k + 1, lanes])], CF_HALO,
            [k - CF_CONV // 2 for k in range(CF_CONV)], emit, post)


def _conformer(l, z, cf_conv_w, cf_conv_b3, cf_ln_g3, cf_ln_b3):
    return _layer_call(
        _conformer_body,
        grid=(N_TILES,),
        in_specs=[pl.BlockSpec((TM, 2 * D_CF), lambda i, l: (i, 2)),
                  pl.BlockSpec((1, CF_CONV, D_CF), lambda i, l: (l[0], 0, 0)),
                  pl.BlockSpec((1, 1, D_CF), lambda i, l: (l[0], 0, 0)),
                  pl.BlockSpec((1, 1, D_CF), lambda i, l: (l[0], 0, 0)),
                  pl.BlockSpec((1, 1, D_CF), lambda i, l: (l[0], 0, 0))],
        out_specs=pl.BlockSpec((TM, D_CF), lambda i, l: (i, 0)),
        out_shape=jax.ShapeDtypeStruct((N_TOK, D_CF), BF16),
        scratch_shapes=[pltpu.VMEM(_pad_shape(CF_HALO, D_CF), F32),
                        pltpu.VMEM((CONV_RB, D_CF), F32)],
        name="conformer")(l, z, cf_conv_w, cf_conv_b3, cf_ln_g3, cf_ln_b3)


def _merge_body(l_ref, x_ref, ya_ref, yb_ref, yc_ref, zg_ref, mod_ref, gn_ref, wb_ref, wo_ref,
                o_ref):
    i = pl.program_id(0)
    merged = None
    for p, y_ref in enumerate((ya_ref, yb_ref, yc_ref)):
        gate = _sigmoid(zg_ref[:, p * D_MODEL:(p + 1) * D_MODEL].astype(F32))
        term = gate * jnp.dot(y_ref[...], wb_ref[0, p], preferred_element_type=F32)
        merged = term if merged is None else merged + term
    out = jnp.dot(merged.astype(BF16), wo_ref[0], preferred_element_type=F32)
    m = mod_ref[0, _mod_row_of_tile(i, TM_MERGE)]
    gate1 = m[:, 2 * D_MODEL:3 * D_MODEL]
    o_ref[...] = x_ref[...] + gate1 * _rms_rows(out, gn_ref[0, 1:2, :])


def _merge(l, x, ya, yb, yc, z, mod4, g_norm, wb_bf16, wo_bf16):
    tile = lambda i, l: (i, 0)
    return _layer_call(
        _merge_body,
        grid=(N_TOK // TM_MERGE,),
        in_specs=[pl.BlockSpec((TM_MERGE, D_MODEL), tile),
                  pl.BlockSpec((TM_MERGE, D_MODEL), tile),
                  pl.BlockSpec((TM_MERGE, D_MODEL), tile),
                  pl.BlockSpec((TM_MERGE, D_MODEL), tile),
                  pl.BlockSpec((TM_MERGE, 3 * D_MODEL), lambda i, l: (i, 2)),
                  pl.BlockSpec((1, MOD_ROWS, 1, 6 * D_MODEL), lambda i, l: (l[0], 0, 0, 0)),
                  pl.BlockSpec((1, 4, D_MODEL), lambda i, l: (l[0], 0, 0)),
                  pl.BlockSpec((1, 3, D_MODEL, D_MODEL), lambda i, l: (l[0], 0, 0, 0)),
                  pl.BlockSpec((1, D_MODEL, D_MODEL), lambda i, l: (l[0], 0, 0))],
        out_specs=pl.BlockSpec((TM_MERGE, D_MODEL), tile),
        out_shape=jax.ShapeDtypeStruct((N_TOK, D_MODEL), F32),
        scratch_shapes=[],
        name="merge")(l, x, ya, yb, yc, z, mod4, g_norm, wb_bf16, wo_bf16)


FFN_HALO = 8
N_FF_STEPS = D_FF // FF_CW
FFN_BLOCKS = TM // CONV_RB
FFN_SPLIT = (TM // 2 - FFN_HALO) // CONV_RB
BLOCKS_PER_REGION = REGION // CONV_RB


def _ffn_body(l_ref, x_ref, mod_ref, gn_ref, wg_ref, wv_ref, cwg_ref, cwv_ref, cbg_ref, cbv_ref,
              wd_ref, o_ref, h_ref, acc_ref, padg_ref, padv_ref, act_ref, wgb_ref, wvb_ref, wdb_ref):
    i = pl.program_id(0)
    c = pl.program_id(1)
    m = mod_ref[0, _mod_row_of_tile(i, TM)]

    @pl.when(c == 0)
    def _():
        _modulated_norm_to(h_ref, x_ref, gn_ref[0, 2:3, :],
                           m[:, 4 * D_MODEL:5 * D_MODEL], m[:, 3 * D_MODEL:4 * D_MODEL])
        acc_ref[...] = jnp.zeros_like(acc_ref)

    wgb_ref[...] = wg_ref[0].astype(BF16)
    wvb_ref[...] = wv_ref[0].astype(BF16)
    wdb_ref[...] = wd_ref[0].astype(BF16)
    zeros = jnp.zeros((FFN_HALO, FF_CW), F32)
    for pad_ref in (padg_ref, padv_ref):
        pad_ref[0:FFN_HALO, :] = zeros
        pad_ref[FFN_HALO + TM:2 * FFN_HALO + TM, :] = zeros

    def up(r0, r1):
        h = h_ref[r0:r1, :]
        padg_ref[FFN_HALO + r0:FFN_HALO + r1, :] = jnp.dot(h, wgb_ref[...], preferred_element_type=F32)
        padv_ref[FFN_HALO + r0:FFN_HALO + r1, :] = jnp.dot(h, wvb_ref[...], preferred_element_type=F32)

    is_ctx = i < CTX_TILES
    row = lax.broadcasted_iota(jnp.int32, (SUBLANES, LANES), 0)
    seq_first = row == jnp.where(is_ctx, 0, -1)
    seq_last = row == jnp.where(is_ctx, SUBLANES - 1, -1)

    def conv_act(rb):
        base = rb * CONV_RB
        for lb in range(FF_CW // LANES):
            lanes = slice(lb * LANES, (lb + 1) * LANES)
            outs = []
            for pad_ref, cw_ref, cb_ref in ((padg_ref, cwg_ref, cbg_ref), (padv_ref, cwv_ref, cbv_ref)):
                prev = pad_ref[base + FFN_HALO - 1:base + FFN_HALO - 1 + CONV_RB, lanes]
                cur = pad_ref[base + FFN_HALO:base + FFN_HALO + CONV_RB, lanes]
                nxt = pad_ref[base + FFN_HALO + 1:base + FFN_HALO + 1 + CONV_RB, lanes]
                if rb % BLOCKS_PER_REGION == 0 and rb > 0:
                    prev = jnp.concatenate(
                        [jnp.where(seq_first, 0.0, prev[:SUBLANES]), prev[SUBLANES:]], axis=0)
                if rb % BLOCKS_PER_REGION == BLOCKS_PER_REGION - 1 and rb < FFN_BLOCKS - 1:
                    nxt = jnp.concatenate(
                        [nxt[:-SUBLANES], jnp.where(seq_last, 0.0, nxt[-SUBLANES:])], axis=0)
                outs.append(prev * cw_ref[0, 0:1, lanes] + cur * cw_ref[0, 1:2, lanes]
                            + nxt * cw_ref[0, 2:3, lanes] + cb_ref[0, :, lanes])
            act_ref[base:base + CONV_RB, lanes] = (jax.nn.gelu(outs[0]) * outs[1]).astype(act_ref.dtype)

    def down(r0, r1):
        acc_ref[r0:r1, :] += jnp.dot(act_ref[r0:r1, :], wdb_ref[...], preferred_element_type=F32)

    up(0, TM // 2)
    up(TM // 2, TM)
    for rb in range(FFN_SPLIT):
        conv_act(rb)
    down(0, FFN_SPLIT * CONV_RB)
    for rb in range(FFN_SPLIT, FFN_BLOCKS):
        conv_act(rb)
    down(FFN_SPLIT * CONV_RB, TM)

    @pl.when(c == N_FF_STEPS - 1)
    def _():
        gate2 = m[:, 5 * D_MODEL:6 * D_MODEL]
        g3 = gn_ref[0, 3:4, :]

        def body(rb, carry):
            rows = pl.ds(pl.multiple_of(rb * NORM_RB, NORM_RB), NORM_RB)
            o_ref[rows, :] = x_ref[rows, :] + gate2 * _rms_rows(acc_ref[rows, :], g3)
            return carry
        lax.fori_loop(0, TM // NORM_RB, body, 0)


def _ffn(l, x, mod4, g_norm, ffn_up, ffn_conv_w, ffn_conv_b3, ffn_down):
    return _layer_call(
        _ffn_body,
        grid=(N_TILES, N_FF_STEPS),
        in_specs=[pl.BlockSpec((TM, D_MODEL), lambda i, c, l: (i, 0)),
                  pl.BlockSpec((1, MOD_ROWS, 1, 6 * D_MODEL), lambda i, c, l: (l[0], 0, 0, 0)),
                  pl.BlockSpec((1, 4, D_MODEL), lambda i, c, l: (l[0], 0, 0)),
                  pl.BlockSpec((1, D_MODEL, FF_CW), lambda i, c, l: (l[0], 0, c)),
                  pl.BlockSpec((1, D_MODEL, FF_CW), lambda i, c, l: (l[0], 0, N_FF_STEPS + c)),
                  pl.BlockSpec((1, FFN_CONV, FF_CW), lambda i, c, l: (l[0], 0, c)),
                  pl.BlockSpec((1, FFN_CONV, FF_CW), lambda i, c, l: (l[0], 0, N_FF_STEPS + c)),
                  pl.BlockSpec((1, 1, FF_CW), lambda i, c, l: (l[0], 0, c)),
                  pl.BlockSpec((1, 1, FF_CW), lambda i, c, l: (l[0], 0, N_FF_STEPS + c)),
                  pl.BlockSpec((1, FF_CW, D_MODEL), lambda i, c, l: (l[0], c, 0))],
        out_specs=pl.BlockSpec((TM, D_MODEL), lambda i, c, l: (i, 0)),
        out_shape=jax.ShapeDtypeStruct((N_TOK, D_MODEL), F32),
        scratch_shapes=[pltpu.VMEM((TM, D_MODEL), BF16),
                        pltpu.VMEM((TM, D_MODEL), F32),
                        pltpu.VMEM((TM + 2 * FFN_HALO, FF_CW), F32),
                        pltpu.VMEM((TM + 2 * FFN_HALO, FF_CW), F32),
                        pltpu.VMEM((TM, FF_CW), BF16),
                        pltpu.VMEM((D_MODEL, FF_CW), BF16),
                        pltpu.VMEM((D_MODEL, FF_CW), BF16),
                        pltpu.VMEM((FF_CW, D_MODEL), BF16)],
        name="ffn")(l, x, mod4, g_norm, ffn_up, ffn_up, ffn_conv_w, ffn_conv_w,
                    ffn_conv_b3, ffn_conv_b3, ffn_down)


def _pos_table():
    t = jnp.arange(DEC_SEQ)
    r = (t // GRID_W).astype(F32)
    col = (t % GRID_W).astype(F32)
    q = D_MODEL // 4
    omega = 1.0 / (POS_BASE ** (jnp.arange(q, dtype=F32) / q))

    def emb(p):
        ang = p[:, None] * omega[None, :]
        return jnp.concatenate([jnp.sin(ang), jnp.cos(ang)], axis=-1)
    return jnp.concatenate([emb(r), emb(col)], axis=-1).astype(F32)


def _block_diag_gates(rg_w):
    per_tile = MXU_DIM // BS_RNN
    n_col = H_RNN // per_tile
    w = rg_w.reshape(DEPTH, 2, 2, n_col, per_tile, BS_RNN, BS_RNN)
    eye = jnp.eye(per_tile, dtype=rg_w.dtype)
    tiles = jnp.einsum('ldkcaij,ab->ldkcaibj', w, eye)
    return tiles.reshape(DEPTH, 2 * 2 * n_col, MXU_DIM, MXU_DIM).astype(BF16)


def kernel(x_prompt, x_sample, state_rglru, c, c_ctx, w_mod, b_mod, g_norm, w_in, b_in,
           rnn_conv_w, rnn_conv_b, rg_w, rg_b, rg_lambda, sg_norm_g, sg_w, sg_b,
           cf_conv_w, cf_conv_b, cf_ln_g, cf_ln_b, w_branch, w_out, ffn_up, ffn_conv_w,
           ffn_conv_b, ffn_down):
    x = _prep(x_prompt.reshape(BATCH * SEQ, D_MODEL), x_sample.reshape(DEC_BATCH * DEC_SEQ, D_MODEL),
              _pos_table())

    cond = jnp.zeros((MOD_ROWS, D_MODEL), F32).at[:DEC_BATCH].set(c).at[CTX_MOD_ROW].set(c_ctx)
    mod4 = _mod(cond, w_mod, b_mod).reshape(DEPTH, MOD_ROWS, 1, 6 * D_MODEL)

    h0_all = jnp.zeros((DEPTH, N_TILES, SUBLANES, D_RNN), F32)
    h0_all = h0_all.at[:, CTX_TILES:, 0:2, :].set(jnp.transpose(state_rglru.astype(F32), (1, 0, 2, 3)))

    rgw = _block_diag_gates(rg_w)
    wb_bf16 = w_branch.astype(BF16)
    wo_bf16 = w_out.astype(BF16)
    b_in3 = b_in.reshape(DEPTH, 1, N_IN)
    rnn_conv_b3 = rnn_conv_b.reshape(DEPTH, 1, D_RNN)
    rg_b4 = rg_b.reshape(DEPTH, 4, D_RNN)
    sg_norm_g3 = sg_norm_g.reshape(DEPTH, 1, D_SG)
    sg_bt = jnp.transpose(sg_b, (0, 2, 1))
    cf_conv_b3 = cf_conv_b.reshape(DEPTH, 1, D_CF)
    cf_ln_g3 = cf_ln_g.reshape(DEPTH, 1, D_CF)
    cf_ln_b3 = cf_ln_b.reshape(DEPTH, 1, D_CF)
    ffn_conv_b3 = ffn_conv_b.reshape(DEPTH, 1, 2 * D_FF)

    def layer(x, l):
        z = _inproj(l, x, mod4, g_norm, w_in, b_in3)
        ya, st = _rnn(l, z, rnn_conv_w, rnn_conv_b3, rgw, rg_b4, rg_lambda, h0_all)
        yb = _sgu(l, z, sg_norm_g3, sg_w, sg_bt)
        yc = _conformer(l, z, cf_conv_w, cf_conv_b3, cf_ln_g3, cf_ln_b3)
        x = _merge(l, x, ya, yb, yc, z, mod4, g_norm, wb_bf16, wo_bf16)
        x = _ffn(l, x, mod4, g_norm, ffn_up, ffn_conv_w, ffn_conv_b3, ffn_down)
        return x, st[0, :CTX_TILES]

    x, states = lax.scan(layer, x, jnp.arange(DEPTH, dtype=jnp.int32).reshape(DEPTH, 1))

    y_prompt = x[:BATCH * SEQ].reshape(BATCH, SEQ, D_MODEL)
    y_sample = x[BATCH * SEQ:].reshape(DEC_BATCH, DEC_SEQ, D_MODEL)
    new_state = states.reshape(DEPTH, BATCH, 2, D_RNN).transpose(1, 0, 2, 3)
    return (y_prompt, y_sample, new_state)
```

```python
import functools

import jax
import jax.numpy as jnp
import numpy as np
from jax import lax
from jax.experimental import pallas as pl
from jax.experimental.pallas import tpu as pltpu

F32 = jnp.float32
BF16 = jnp.bfloat16

D_MODEL = 1024
BATCH = 16
SEQ = 256
DEPTH = 4
DEC_BATCH = 4
DEC_SEQ = 1024
GRID_W = 64
D_RNN = 1024
H_RNN = 16
BS_RNN = D_RNN // H_RNN
RNN_CONV = 4
C_RG = 8.0
D_SG = 1024
SG_GROUPS = 8
SG_GD = D_SG // SG_GROUPS
CHUNK = 128
D_CF = 1024
CF_CONV = 31
D_FF = 4096
FFN_CONV = 3
N_IN = 2 * D_RNN + 2 * D_SG + 2 * D_CF + 3 * D_MODEL
EPS = 1e-6
POS_BASE = 10000.0

LANES = 128
SUBLANES = 8
MXU_DIM = 256

N_TOK = BATCH * SEQ + DEC_BATCH * DEC_SEQ
TM = 1024
N_TILES = N_TOK // TM
CTX_TILES = BATCH * SEQ // TM
MOD_ROWS = 8
CTX_MOD_ROW = DEC_BATCH
CONV_RB = 64
FF_CW = 512
TM_MERGE = 512
VMEM_LIMIT = 56 * 1024 * 1024


def _sigmoid(x):
    return 0.5 * (jnp.tanh(0.5 * x) + 1.0)


def _rms_rows(x, g):
    return x * lax.rsqrt(jnp.mean(x * x, axis=-1, keepdims=True) + EPS) * g


def _mod_row_of_tile(i, tile_rows):
    tiles_per_latent = DEC_SEQ // tile_rows
    ctx_tiles = BATCH * SEQ // tile_rows
    lat = jnp.maximum(i - ctx_tiles, 0) // tiles_per_latent
    return jnp.where(i < ctx_tiles, CTX_MOD_ROW, lat)


def _params(n_grid):
    return pltpu.CompilerParams(dimension_semantics=("arbitrary",) * n_grid,
                                vmem_limit_bytes=VMEM_LIMIT)


def _layer_call(body, grid, in_specs, out_specs, out_shape, scratch_shapes, name):
    return pl.pallas_call(
        body,
        grid_spec=pltpu.PrefetchScalarGridSpec(
            num_scalar_prefetch=1, grid=grid, in_specs=in_specs, out_specs=out_specs,
            scratch_shapes=scratch_shapes),
        out_shape=out_shape,
        compiler_params=_params(len(grid)),
        name=name)


STEPS = CHUNK
HALO_LW = 256


def _interleave(x):
    return x.reshape(N_TILES, SUBLANES, STEPS, -1).transpose(0, 2, 1, 3).reshape(x.shape)


def _deinterleave(x):
    return x.reshape(N_TILES, STEPS, SUBLANES, -1).transpose(0, 2, 1, 3).reshape(x.shape)


def _chunk_edge_masks(is_ctx, shape):
    c = lax.broadcasted_iota(jnp.int32, shape, 0) & (SUBLANES - 1)
    m = jnp.where(is_ctx, SEQ // CHUNK - 1, DEC_SEQ // CHUNK - 1)
    cm = c & m
    return cm == 0, cm == m


def _fill_time_halo(pad_ref, halo, is_ctx):
    n = halo * SUBLANES
    for lo in range(0, pad_ref.shape[1], HALO_LW):
        lanes = slice(lo, lo + HALO_LW)
        first, last = _chunk_edge_masks(is_ctx, (n, HALO_LW))
        tail = pad_ref[TM:TM + n, lanes]
        head = pad_ref[n:2 * n, lanes]
        pad_ref[0:n, lanes] = jnp.where(first, 0.0, pltpu.roll(tail, 1, 0))
        pad_ref[n + TM:2 * n + TM, lanes] = jnp.where(last, 0.0, pltpu.roll(head, n - 1, 0))


def _expand_rows(dst_ref, row0, rows):
    for k in range(rows.shape[0]):
        lo = row0 + k * SUBLANES
        dst_ref[lo:lo + SUBLANES, :] = jnp.broadcast_to(rows[k:k + 1, :], (SUBLANES, rows.shape[1]))


def _conv_block(pad_ref, row0, lanes, wexp_ref, w0, n_taps):
    def wrow(k):
        return wexp_ref[w0 + k * SUBLANES:w0 + (k + 1) * SUBLANES, lanes]

    groups = CONV_RB // SUBLANES
    acc = jnp.broadcast_to(wrow(n_taps), (groups, SUBLANES, LANES))
    for k in range(n_taps):
        x = pad_ref[pl.ds(row0 + k * SUBLANES, CONV_RB), lanes]
        acc = acc + x.reshape(groups, SUBLANES, LANES) * wrow(k)
    return acc.reshape(CONV_RB, LANES)


def _dwconv(pad_ref, halo, first_offset, n_taps, wexp_ref, emit, post=None):
    def body(b, carry):
        base = pl.multiple_of(b * CONV_RB, CONV_RB)
        for lb in range(pad_ref.shape[1] // LANES):
            lanes = slice(lb * LANES, (lb + 1) * LANES)
            emit(base, lb, _conv_block(pad_ref, base + (halo + first_offset) * SUBLANES, lanes,
                                       wexp_ref, 0, n_taps))
        if post is not None:
            post(base)
        return carry

    lax.fori_loop(0, TM // CONV_RB, body, 0)


def _prep_body(xp_ref, xs_ref, pos_ref, o_ref):
    i = pl.program_id(0)

    @pl.when(i < CTX_TILES)
    def _():
        o_ref[...] = xp_ref[...]

    @pl.when(i >= CTX_TILES)
    def _():
        o_ref[...] = xs_ref[...] + pos_ref[...]


def _prep(xp, xs, pos):
    return pl.pallas_call(
        _prep_body,
        grid=(N_TILES,),
        in_specs=[pl.BlockSpec((TM, D_MODEL), lambda i: (jnp.minimum(i, CTX_TILES - 1), 0)),
                  pl.BlockSpec((TM, D_MODEL), lambda i: (jnp.maximum(i - CTX_TILES, 0), 0)),
                  pl.BlockSpec((DEC_SEQ, D_MODEL), lambda i: (0, 0))],
        out_specs=pl.BlockSpec((TM, D_MODEL), lambda i: (i, 0)),
        out_shape=jax.ShapeDtypeStruct((N_TOK, D_MODEL), F32),
        compiler_params=_params(1),
        name="prep")(xp, xs, pos)


MOD_TN = 1536


def _mod_body(cond_ref, w_ref, b_ref, o_ref):
    c = cond_ref[...]
    s = (c * _sigmoid(c)).astype(BF16)
    o_ref[0] = jnp.dot(s, w_ref[0].astype(BF16), preferred_element_type=F32) + b_ref[0]


def _mod(cond, w_mod, b_mod):
    n_mod = 6 * D_MODEL
    return pl.pallas_call(
        _mod_body,
        grid=(DEPTH, n_mod // MOD_TN),
        in_specs=[pl.BlockSpec((MOD_ROWS, D_MODEL), lambda l, j: (0, 0)),
                  pl.BlockSpec((1, D_MODEL, MOD_TN), lambda l, j: (l, 0, j)),
                  pl.BlockSpec((1, 1, MOD_TN), lambda l, j: (l, 0, j))],
        out_specs=pl.BlockSpec((1, MOD_ROWS, MOD_TN), lambda l, j: (l, 0, j)),
        out_shape=jax.ShapeDtypeStruct((DEPTH, MOD_ROWS, n_mod), F32),
        compiler_params=_params(2),
        name="mod")(cond, w_mod, b_mod.reshape(DEPTH, 1, n_mod))


IN_TN = 1024
NORM_RB = 64


def _modulated_norm_to(h_ref, x_ref, g, scale, shift):
    def body(rb, carry):
        rows = pl.ds(pl.multiple_of(rb * NORM_RB, NORM_RB), NORM_RB)
        h = _rms_rows(x_ref[rows, :], g) * (1.0 + scale) + shift
        h_ref[rows, :] = h.astype(h_ref.dtype)
        return carry
    lax.fori_loop(0, x_ref.shape[0] // NORM_RB, body, 0)


def _inproj_body(l_ref, x_ref, mod_ref, gn_ref, w_ref, b_ref, z_ref, h_ref):
    i = pl.program_id(0)
    j = pl.program_id(1)

    @pl.when(j == 0)
    def _():
        m = mod_ref[0, _mod_row_of_tile(i, TM)]
        _modulated_norm_to(h_ref, x_ref, gn_ref[0, 0:1, :],
                           m[:, D_MODEL:2 * D_MODEL], m[:, 0:D_MODEL])

    acc = jnp.dot(h_ref[...], w_ref[0].astype(BF16), preferred_element_type=F32)
    z_ref[...] = (acc + b_ref[0]).astype(z_ref.dtype)


def _inproj(l, x, mod4, g_norm, w_in, b_in3):
    return _layer_call(
        _inproj_body,
        grid=(N_TILES, N_IN // IN_TN),
        in_specs=[pl.BlockSpec((TM, D_MODEL), lambda i, j, l: (i, 0)),
                  pl.BlockSpec((1, MOD_ROWS, 1, 6 * D_MODEL), lambda i, j, l: (l[0], 0, 0, 0)),
                  pl.BlockSpec((1, 4, D_MODEL), lambda i, j, l: (l[0], 0, 0)),
                  pl.BlockSpec((1, D_MODEL, IN_TN), lambda i, j, l: (l[0], 0, j)),
                  pl.BlockSpec((1, 1, IN_TN), lambda i, j, l: (l[0], 0, j))],
        out_specs=pl.BlockSpec((TM, IN_TN), lambda i, j, l: (i, j)),
        out_shape=jax.ShapeDtypeStruct((N_TOK, N_IN), BF16),
        scratch_shapes=[pltpu.VMEM((TM, D_MODEL), BF16)],
        name="inproj")(l, x, mod4, g_norm, w_in, b_in3)


RNN_HALO = 2
GATE_RB = 256
FILL_RB = 64


def _rows_of(value):
    return [value[r:r + 1, :] for r in range(SUBLANES)]


def _chain_chunks(a_end, b_end, h0_rows, is_ctx, reverse):
    per_ctx_seq = SEQ // CHUNK
    a_rows, b_rows = _rows_of(a_end), _rows_of(b_end)
    h_in, h_out = [None] * SUBLANES, [None] * SUBLANES
    order = range(SUBLANES - 1, -1, -1) if reverse else range(SUBLANES)
    prev = None
    for c in order:
        starts_ctx_seq = (c % per_ctx_seq == per_ctx_seq - 1) if reverse else (c % per_ctx_seq == 0)
        if prev is None:
            h = jnp.where(is_ctx, h0_rows[c // per_ctx_seq], h0_rows[0])
        elif starts_ctx_seq:
            h = jnp.where(is_ctx, h0_rows[c // per_ctx_seq], prev)
        else:
            h = prev
        h_in[c] = h
        prev = a_rows[c] * h + b_rows[c]
        h_out[c] = prev
    return jnp.concatenate(h_in, axis=0), h_out


def _rnn_body(l_ref, zx_ref, zg_ref, cw_ref, cb_ref, rgw_ref, rgb_ref, lam_ref, h0_ref,
              ya_ref, st_ref, pad_ref, wexp_ref, xc_ref, af_ref, uf_ref, ab_ref, ub_ref):
    i = pl.program_id(0)
    is_ctx = i < CTX_TILES
    halo_rows = RNN_HALO * SUBLANES

    def fill(rb, carry):
        r0 = pl.multiple_of(rb * FILL_RB, FILL_RB)
        pad_ref[pl.ds(r0 + halo_rows, FILL_RB), :] = zx_ref[pl.ds(r0, FILL_RB), :].astype(F32)
        return carry

    lax.fori_loop(0, TM // FILL_RB, fill, 0)
    _fill_time_halo(pad_ref, RNN_HALO, is_ctx)

    _expand_rows(wexp_ref, 0, cw_ref[0])
    _expand_rows(wexp_ref, RNN_CONV * SUBLANES, cb_ref[0])

    def emit_xc(row0, lb, block):
        xc_ref[pl.ds(row0, CONV_RB), lb * LANES:(lb + 1) * LANES] = block

    _dwconv(pad_ref, RNN_HALO, -2, RNN_CONV, wexp_ref, emit_xc)

    neg_lam = -lam_ref[0]
    softplus = jnp.maximum(neg_lam, 0.0) + jnp.log1p(jnp.exp(-jnp.abs(neg_lam)))
    coef = (-0.5 * C_RG * np.log2(np.e)) * softplus
    a_refs = (af_ref, ab_ref)
    u_refs = (uf_ref, ub_ref)
    n_col = D_RNN // MXU_DIM

    def gate_body(rb, carry):
        rows = pl.ds(pl.multiple_of(rb * GATE_RB, GATE_RB), GATE_RB)
        for j in range(n_col):
            lanes = slice(j * MXU_DIM, (j + 1) * MXU_DIM)
            xc = xc_ref[rows, lanes]
            xcb = xc.astype(BF16)
            half_xc = 0.5 * xc
            for d in range(2):
                base = d * 2 * n_col
                g_r = jnp.dot(xcb, rgw_ref[0, base + j], preferred_element_type=F32)
                g_i = jnp.dot(xcb, rgw_ref[0, base + n_col + j], preferred_element_type=F32)
                t_r = jnp.tanh(g_r + rgb_ref[0, 2 * d:2 * d + 1, lanes])
                t_i = jnp.tanh(g_i + rgb_ref[0, 2 * d + 1:2 * d + 2, lanes])
                cf = coef[d:d + 1, lanes]
                a = jnp.exp2(cf * t_r + cf)
                y = 1.0 - a * a
                root = jnp.where(y == 0.0, 0.0, y * lax.rsqrt(y))
                a_refs[d][rows, lanes] = a
                u_refs[d][rows, lanes] = root * ((t_i + 1.0) * half_xc)
        return carry

    lax.fori_loop(0, TM // GATE_RB, gate_body, 0)

    def group(t):
        return pl.ds(pl.multiple_of(t * SUBLANES, SUBLANES), SUBLANES)

    def totals(t, carry):
        a_f, b_f, a_b, b_b = carry
        rf, rb = group(t), group(STEPS - 1 - t)
        a = af_ref[rf, :]
        b_f = a * b_f + uf_ref[rf, :]
        a_f = a * a_f
        a = ab_ref[rb, :]
        b_b = a * b_b + ub_ref[rb, :]
        a_b = a * a_b
        return a_f, b_f, a_b, b_b

    ones = jnp.ones((SUBLANES, D_RNN), F32)
    zeros = jnp.zeros((SUBLANES, D_RNN), F32)
    a_f, b_f, a_b, b_b = lax.fori_loop(0, STEPS, totals, (ones, zeros, ones, zeros))

    n_seq = TM // SEQ
    h0_f = [h0_ref[0, 0, 2 * s:2 * s + 1, :] for s in range(n_seq)]
    h0_b = [h0_ref[0, 0, 2 * s + 1:2 * s + 2, :] for s in range(n_seq)]
    hin_f, hout_f = _chain_chunks(a_f, b_f, h0_f, is_ctx, False)
    hin_b, hout_b = _chain_chunks(a_b, b_b, h0_b, is_ctx, True)

    per_ctx_seq = SEQ // CHUNK
    none = jnp.zeros((1, D_RNN), F32)
    for s in range(n_seq):
        last_f = hout_f[s * per_ctx_seq + per_ctx_seq - 1]
        first_b = hout_b[s * per_ctx_seq]
        st_ref[0, 0, 2 * s:2 * s + 1, :] = jnp.where(is_ctx, last_f, hout_f[SUBLANES - 1] if s == 0 else none)
        st_ref[0, 0, 2 * s + 1:2 * s + 2, :] = jnp.where(is_ctx, first_b, hout_b[0] if s == 0 else none)

    def backward(t, h):
        rb = group(STEPS - 1 - t)
        h = ab_ref[rb, :] * h + ub_ref[rb, :]
        ub_ref[rb, :] = h
        return h

    lax.fori_loop(0, STEPS, backward, hin_b)

    def forward(tt, h):
        r0, r1 = group(2 * tt), group(2 * tt + 1)
        h0 = af_ref[r0, :] * h + uf_ref[r0, :]
        h1 = af_ref[r1, :] * h0 + uf_ref[r1, :]
        rows = pl.ds(pl.multiple_of(tt * 2 * SUBLANES, 2 * SUBLANES), 2 * SUBLANES)
        both = jnp.concatenate([h0, h1], axis=0) + ub_ref[rows, :]
        ya_ref[rows, :] = (both * jax.nn.gelu(zg_ref[rows, :].astype(F32))).astype(ya_ref.dtype)
        return h1

    lax.fori_loop(0, STEPS // 2, forward, hin_f)


def _rnn(l, z, rnn_conv_w, rnn_conv_b3, rgw, rg_b4, rg_lambda, h0_all):
    n_rgw = 2 * 2 * (D_RNN // MXU_DIM)
    return _layer_call(
        _rnn_body,
        grid=(N_TILES,),
        in_specs=[pl.BlockSpec((TM, D_RNN), lambda i, l: (i, 0)),
                  pl.BlockSpec((TM, D_RNN), lambda i, l: (i, 1)),
                  pl.BlockSpec((1, RNN_CONV, D_RNN), lambda i, l: (l[0], 0, 0)),
                  pl.BlockSpec((1, 1, D_RNN), lambda i, l: (l[0], 0, 0)),
                  pl.BlockSpec((1, n_rgw, MXU_DIM, MXU_DIM), lambda i, l: (l[0], 0, 0, 0)),
                  pl.BlockSpec((1, 4, D_RNN), lambda i, l: (l[0], 0, 0)),
                  pl.BlockSpec((1, 2, D_RNN), lambda i, l: (l[0], 0, 0)),
                  pl.BlockSpec((1, 1, SUBLANES, D_RNN), lambda i, l: (l[0], i, 0, 0))],
        out_specs=[pl.BlockSpec((TM, D_MODEL), lambda i, l: (i, 0)),
                   pl.BlockSpec((1, 1, SUBLANES, D_RNN), lambda i, l: (0, i, 0, 0))],
        out_shape=[jax.ShapeDtypeStruct((N_TOK, D_RNN), BF16),
                   jax.ShapeDtypeStruct((1, N_TILES, SUBLANES, D_RNN), F32)],
        scratch_shapes=[pltpu.VMEM((TM + 2 * RNN_HALO * SUBLANES, D_RNN), F32),
                        pltpu.VMEM(((RNN_CONV + 1) * SUBLANES, D_RNN), F32),
                        pltpu.VMEM((TM, D_RNN), F32),
                        pltpu.VMEM((TM, D_RNN), F32),
                        pltpu.VMEM((TM, D_RNN), F32),
                        pltpu.VMEM((TM, D_RNN), F32),
                        pltpu.VMEM((TM, D_RNN), F32)],
        name="rnn")(l, z, z, rnn_conv_w, rnn_conv_b3, rgw, rg_b4, rg_lambda, h0_all)


ACT_RB = 64


def _split3(x):
    hi = x.astype(BF16)
    r1 = x - hi.astype(F32)
    mid = r1.astype(BF16)
    lo = (r1 - mid.astype(F32)).astype(BF16)
    return hi, mid, lo


def _sgu_body(l_ref, z_ref, g_ref, w_ref, b_ref, yb_ref, su_ref, sv_ref, kron_ref, bias_ref):
    i = pl.program_id(0)
    g = pl.program_id(1)

    @pl.when(i == 0)
    def _():
        shift = SUBLANES.bit_length() - 1
        rep = jnp.where(jnp.right_shift(lax.broadcasted_iota(jnp.int32, (TM, CHUNK), 0), shift)
                        == lax.broadcasted_iota(jnp.int32, (TM, CHUNK), 1), 1.0, 0.0).astype(BF16)
        rep_t = jnp.where(jnp.right_shift(lax.broadcasted_iota(jnp.int32, (CHUNK, TM), 1), shift)
                          == lax.broadcasted_iota(jnp.int32, (CHUNK, TM), 0), 1.0, 0.0).astype(BF16)
        rows = jnp.dot(rep, w_ref[0, 0].astype(BF16), preferred_element_type=F32).astype(BF16)
        full = jnp.dot(rows, rep_t, preferred_element_type=F32)
        same_chunk = ((lax.broadcasted_iota(jnp.int32, (TM, TM), 0) & (SUBLANES - 1))
                      == (lax.broadcasted_iota(jnp.int32, (TM, TM), 1) & (SUBLANES - 1)))
        kron_ref[g] = jnp.where(same_chunk, full, 0.0).astype(BF16)
        bias = jnp.broadcast_to(b_ref[0, 0], (CHUNK, SG_GD))
        bias_ref[g] = sum(jnp.dot(rep, piece, preferred_element_type=F32) for piece in _split3(bias))

    @pl.when(g == 0)
    def _():
        gain = g_ref[0]

        def act_body(rb, carry):
            rows = pl.ds(pl.multiple_of(rb * ACT_RB, ACT_RB), ACT_RB)
            uv = jax.nn.gelu(z_ref[rows, :].astype(F32))
            sv = uv[:, D_SG:]
            mu = jnp.mean(sv, axis=-1, keepdims=True)
            svc = sv - mu
            y = svc * lax.rsqrt(jnp.mean(svc * svc, axis=-1, keepdims=True) + EPS) * gain
            for j in range(SG_GROUPS):
                su_ref[j, rows, :] = uv[:, j * SG_GD:(j + 1) * SG_GD]
                sv_ref[j, rows, :] = y[:, j * SG_GD:(j + 1) * SG_GD].astype(sv_ref.dtype)
            return carry

        lax.fori_loop(0, TM // ACT_RB, act_body, 0)

    mixed = jnp.dot(kron_ref[g], sv_ref[g], preferred_element_type=F32) + bias_ref[g]
    yb_ref[...] = (su_ref[g] * mixed).astype(yb_ref.dtype)


def _sgu(l, z, sg_norm_g3, sg_w, sg_b4):
    return _layer_call(
        _sgu_body,
        grid=(N_TILES, SG_GROUPS),
        in_specs=[pl.BlockSpec((TM, 2 * D_SG), lambda i, g, l: (i, 1)),
                  pl.BlockSpec((1, 1, D_SG), lambda i, g, l: (l[0], 0, 0)),
                  pl.BlockSpec((1, 1, CHUNK, CHUNK), lambda i, g, l: (l[0], g, 0, 0)),
                  pl.BlockSpec((1, 1, CHUNK, 1), lambda i, g, l: (l[0], g, 0, 0))],
        out_specs=pl.BlockSpec((TM, SG_GD), lambda i, g, l: (i, g)),
        out_shape=jax.ShapeDtypeStruct((N_TOK, D_SG), BF16),
        scratch_shapes=[pltpu.VMEM((SG_GROUPS, TM, SG_GD), F32),
                        pltpu.VMEM((SG_GROUPS, TM, SG_GD), BF16),
                        pltpu.VMEM((SG_GROUPS, TM, TM), BF16),
                        pltpu.VMEM((SG_GROUPS, TM, SG_GD), F32)],
        name="sgu")(l, z, sg_norm_g3, sg_w, sg_b4)


CF_HALO = CF_CONV // 2


def _conformer_body(l_ref, z_ref, cw_ref, cb_ref, g_ref, b_ref, yc_ref, pad_ref, wexp_ref, blk_ref):
    i = pl.program_id(0)
    halo_rows = CF_HALO * SUBLANES

    def glu(rb, carry):
        r0 = pl.multiple_of(rb * FILL_RB, FILL_RB)
        zz = z_ref[pl.ds(r0, FILL_RB), :].astype(F32)
        pad_ref[pl.ds(r0 + halo_rows, FILL_RB), :] = zz[:, :D_CF] * _sigmoid(zz[:, D_CF:])
        return carry

    lax.fori_loop(0, TM // FILL_RB, glu, 0)
    _fill_time_halo(pad_ref, CF_HALO, i < CTX_TILES)

    _expand_rows(wexp_ref, 0, cw_ref[0])
    _expand_rows(wexp_ref, CF_CONV * SUBLANES, cb_ref[0])

    def emit(row0, lb, block):
        blk_ref[:, lb * LANES:(lb + 1) * LANES] = block

    def post(row0):
        y = blk_ref[...]
        mu = jnp.mean(y, axis=-1, keepdims=True)
        yc = y - mu
        ln = yc * lax.rsqrt(jnp.mean(yc * yc, axis=-1, keepdims=True) + EPS) * g_ref[0] + b_ref[0]
        yc_ref[pl.ds(row0, CONV_RB), :] = (ln * _sigmoid(ln)).astype(yc_ref.dtype)

    _dwconv(pad_ref, CF_HALO, -CF_HALO, CF_CONV, wexp_ref, emit, post)


def _conformer(l, z, cf_conv_w, cf_conv_b3, cf_ln_g3, cf_ln_b3):
    return _layer_call(
        _conformer_body,
        grid=(N_TILES,),
        in_specs=[pl.BlockSpec((TM, 2 * D_CF), lambda i, l: (i, 2)),
                  pl.BlockSpec((1, CF_CONV, D_CF), lambda i, l: (l[0], 0, 0)),
                  pl.BlockSpec((1, 1, D_CF), lambda i, l: (l[0], 0, 0)),
                  pl.BlockSpec((1, 1, D_CF), lambda i, l: (l[0], 0, 0)),
                  pl.BlockSpec((1, 1, D_CF), lambda i, l: (l[0], 0, 0))],
        out_specs=pl.BlockSpec((TM, D_CF), lambda i, l: (i, 0)),
        out_shape=jax.ShapeDtypeStruct((N_TOK, D_CF), BF16),
        scratch_shapes=[pltpu.VMEM((TM + 2 * CF_HALO * SUBLANES, D_CF), F32),
                        pltpu.VMEM(((CF_CONV + 1) * SUBLANES, D_CF), F32),
                        pltpu.VMEM((CONV_RB, D_CF), F32)],
        name="conformer")(l, z, cf_conv_w, cf_conv_b3, cf_ln_g3, cf_ln_b3)


def _merge_body(l_ref, x_ref, ya_ref, yb_ref, yc_ref, zg_ref, mod_ref, gn_ref, wb_ref, wo_ref,
                o_ref):
    i = pl.program_id(0)
    merged = None
    for p, y_ref in enumerate((ya_ref, yb_ref, yc_ref)):
        gate = _sigmoid(zg_ref[:, p * D_MODEL:(p + 1) * D_MODEL].astype(F32))
        term = gate * jnp.dot(y_ref[...], wb_ref[0, p], preferred_element_type=F32)
        merged = term if merged is None else merged + term
    out = jnp.dot(merged.astype(BF16), wo_ref[0], preferred_element_type=F32)
    m = mod_ref[0, _mod_row_of_tile(i, TM_MERGE)]
    gate1 = m[:, 2 * D_MODEL:3 * D_MODEL]
    o_ref[...] = x_ref[...] + gate1 * _rms_rows(out, gn_ref[0, 1:2, :])


def _merge(l, x, ya, yb, yc, z, mod4, g_norm, wb_bf16, wo_bf16):
    tile = lambda i, l: (i, 0)
    return _layer_call(
        _merge_body,
        grid=(N_TOK // TM_MERGE,),
        in_specs=[pl.BlockSpec((TM_MERGE, D_MODEL), tile),
                  pl.BlockSpec((TM_MERGE, D_MODEL), tile),
                  pl.BlockSpec((TM_MERGE, D_MODEL), tile),
                  pl.BlockSpec((TM_MERGE, D_MODEL), tile),
                  pl.BlockSpec((TM_MERGE, 3 * D_MODEL), lambda i, l: (i, 2)),
                  pl.BlockSpec((1, MOD_ROWS, 1, 6 * D_MODEL), lambda i, l: (l[0], 0, 0, 0)),
                  pl.BlockSpec((1, 4, D_MODEL), lambda i, l: (l[0], 0, 0)),
                  pl.BlockSpec((1, 3, D_MODEL, D_MODEL), lambda i, l: (l[0], 0, 0, 0)),
                  pl.BlockSpec((1, D_MODEL, D_MODEL), lambda i, l: (l[0], 0, 0))],
        out_specs=pl.BlockSpec((TM_MERGE, D_MODEL), tile),
        out_shape=jax.ShapeDtypeStruct((N_TOK, D_MODEL), F32),
        scratch_shapes=[],
        name="merge")(l, x, ya, yb, yc, z, mod4, g_norm, wb_bf16, wo_bf16)


FFN_HALO = 1
N_FF_STEPS = D_FF // FF_CW
FFN_BLOCKS = TM // CONV_RB
FFN_SPLIT = (TM // 2 - FFN_HALO * SUBLANES) // CONV_RB


def _ffn_body(l_ref, x_ref, mod_ref, gn_ref, wg_ref, wv_ref, cwg_ref, cwv_ref, cbg_ref, cbv_ref,
              wd_ref, o_ref, h_ref, acc_ref, padg_ref, padv_ref, act_ref, wgb_ref, wvb_ref, wdb_ref,
              wexp_ref):
    i = pl.program_id(0)
    c = pl.program_id(1)
    m = mod_ref[0, _mod_row_of_tile(i, TM)]
    halo_rows = FFN_HALO * SUBLANES

    @pl.when(c == 0)
    def _():
        _modulated_norm_to(h_ref, x_ref, gn_ref[0, 2:3, :],
                           m[:, 4 * D_MODEL:5 * D_MODEL], m[:, 3 * D_MODEL:4 * D_MODEL])
        acc_ref[...] = jnp.zeros_like(acc_ref)

    wgb_ref[...] = wg_ref[0].astype(BF16)
    wvb_ref[...] = wv_ref[0].astype(BF16)
    wdb_ref[...] = wd_ref[0].astype(BF16)

    def up(r0, r1):
        h = h_ref[r0:r1, :]
        padg_ref[halo_rows + r0:halo_rows + r1, :] = jnp.dot(h, wgb_ref[...], preferred_element_type=F32)
        padv_ref[halo_rows + r0:halo_rows + r1, :] = jnp.dot(h, wvb_ref[...], preferred_element_type=F32)

    n_wrows = (FFN_CONV + 1) * SUBLANES
    for s, (cw_ref, cb_ref) in enumerate(((cwg_ref, cbg_ref), (cwv_ref, cbv_ref))):
        _expand_rows(wexp_ref, s * n_wrows, cw_ref[0])
        _expand_rows(wexp_ref, s * n_wrows + FFN_CONV * SUBLANES, cb_ref[0])

    def conv_act(rb):
        base = rb * CONV_RB
        for lb in range(FF_CW // LANES):
            lanes = slice(lb * LANES, (lb + 1) * LANES)
            outs = []
            for s, pad_ref in enumerate((padg_ref, padv_ref)):
                outs.append(_conv_block(pad_ref, base, lanes, wexp_ref, s * n_wrows, FFN_CONV))
            act_ref[base:base + CONV_RB, lanes] = (jax.nn.gelu(outs[0]) * outs[1]).astype(act_ref.dtype)

    def down(r0, r1):
        acc_ref[r0:r1, :] += jnp.dot(act_ref[r0:r1, :], wdb_ref[...], preferred_element_type=F32)

    up(0, TM // 2)
    up(TM // 2, TM)
    is_ctx = i < CTX_TILES
    _fill_time_halo(padg_ref, FFN_HALO, is_ctx)
    _fill_time_halo(padv_ref, FFN_HALO, is_ctx)
    for rb in range(1, FFN_SPLIT):
        conv_act(rb)
    conv_act(0)
    down(0, FFN_SPLIT * CONV_RB)
    for rb in range(FFN_SPLIT, FFN_BLOCKS):
        conv_act(rb)
    down(FFN_SPLIT * CONV_RB, TM)

    @pl.when(c == N_FF_STEPS - 1)
    def _():
        gate2 = m[:, 5 * D_MODEL:6 * D_MODEL]
        g3 = gn_ref[0, 3:4, :]

        def body(rb, carry):
            rows = pl.ds(pl.multiple_of(rb * NORM_RB, NORM_RB), NORM_RB)
            o_ref[rows, :] = x_ref[rows, :] + gate2 * _rms_rows(acc_ref[rows, :], g3)
            return carry
        lax.fori_loop(0, TM // NORM_RB, body, 0)


def _ffn(l, x, mod4, g_norm, ffn_up, ffn_conv_w, ffn_conv_b3, ffn_down):
    return _layer_call(
        _ffn_body,
        grid=(N_TILES, N_FF_STEPS),
        in_specs=[pl.BlockSpec((TM, D_MODEL), lambda i, c, l: (i, 0)),
                  pl.BlockSpec((1, MOD_ROWS, 1, 6 * D_MODEL), lambda i, c, l: (l[0], 0, 0, 0)),
                  pl.BlockSpec((1, 4, D_MODEL), lambda i, c, l: (l[0], 0, 0)),
                  pl.BlockSpec((1, D_MODEL, FF_CW), lambda i, c, l: (l[0], 0, c)),
                  pl.BlockSpec((1, D_MODEL, FF_CW), lambda i, c, l: (l[0], 0, N_FF_STEPS + c)),
                  pl.BlockSpec((1, FFN_CONV, FF_CW), lambda i, c, l: (l[0], 0, c)),
                  pl.BlockSpec((1, FFN_CONV, FF_CW), lambda i, c, l: (l[0], 0, N_FF_STEPS + c)),
                  pl.BlockSpec((1, 1, FF_CW), lambda i, c, l: (l[0], 0, c)),
                  pl.BlockSpec((1, 1, FF_CW), lambda i, c, l: (l[0], 0, N_FF_STEPS + c)),
                  pl.BlockSpec((1, FF_CW, D_MODEL), lambda i, c, l: (l[0], c, 0))],
        out_specs=pl.BlockSpec((TM, D_MODEL), lambda i, c, l: (i, 0)),
        out_shape=jax.ShapeDtypeStruct((N_TOK, D_MODEL), F32),
        scratch_shapes=[pltpu.VMEM((TM, D_MODEL), BF16),
                        pltpu.VMEM((TM, D_MODEL), F32),
                        pltpu.VMEM((TM + 2 * FFN_HALO * SUBLANES, FF_CW), F32),
                        pltpu.VMEM((TM + 2 * FFN_HALO * SUBLANES, FF_CW), F32),
                        pltpu.VMEM((TM, FF_CW), BF16),
                        pltpu.VMEM((D_MODEL, FF_CW), BF16),
                        pltpu.VMEM((D_MODEL, FF_CW), BF16),
                        pltpu.VMEM((FF_CW, D_MODEL), BF16),
                        pltpu.VMEM((2 * (FFN_CONV + 1) * SUBLANES, FF_CW), F32)],
        name="ffn")(l, x, mod4, g_norm, ffn_up, ffn_up, ffn_conv_w, ffn_conv_w,
                    ffn_conv_b3, ffn_conv_b3, ffn_down)


def _pos_table():
    t = jnp.arange(DEC_SEQ)
    r = (t // GRID_W).astype(F32)
    col = (t % GRID_W).astype(F32)
    q = D_MODEL // 4
    omega = 1.0 / (POS_BASE ** (jnp.arange(q, dtype=F32) / q))

    def emb(p):
        ang = p[:, None] * omega[None, :]
        return jnp.concatenate([jnp.sin(ang), jnp.cos(ang)], axis=-1)
    return jnp.concatenate([emb(r), emb(col)], axis=-1).astype(F32)


def _block_diag_gates(rg_w):
    per_tile = MXU_DIM // BS_RNN
    n_col = H_RNN // per_tile
    w = rg_w.reshape(DEPTH, 2, 2, n_col, per_tile, BS_RNN, BS_RNN)
    eye = jnp.eye(per_tile, dtype=rg_w.dtype)
    tiles = jnp.einsum('ldkcaij,ab->ldkcaibj', w, eye)
    return tiles.reshape(DEPTH, 2 * 2 * n_col, MXU_DIM, MXU_DIM).astype(BF16)


def kernel(x_prompt, x_sample, state_rglru, c, c_ctx, w_mod, b_mod, g_norm, w_in, b_in,
           rnn_conv_w, rnn_conv_b, rg_w, rg_b, rg_lambda, sg_norm_g, sg_w, sg_b,
           cf_conv_w, cf_conv_b, cf_ln_g, cf_ln_b, w_branch, w_out, ffn_up, ffn_conv_w,
           ffn_conv_b, ffn_down):
    x = _interleave(_prep(x_prompt.reshape(BATCH * SEQ, D_MODEL),
                          x_sample.reshape(DEC_BATCH * DEC_SEQ, D_MODEL), _pos_table()))

    cond = jnp.zeros((MOD_ROWS, D_MODEL), F32).at[:DEC_BATCH].set(c).at[CTX_MOD_ROW].set(c_ctx)
    mod4 = _mod(cond, w_mod, b_mod).reshape(DEPTH, MOD_ROWS, 1, 6 * D_MODEL)

    h0_all = jnp.zeros((DEPTH, N_TILES, SUBLANES, D_RNN), F32)
    h0_all = h0_all.at[:, CTX_TILES:, 0:2, :].set(jnp.transpose(state_rglru.astype(F32), (1, 0, 2, 3)))

    rgw = _block_diag_gates(0.5 * rg_w)
    wb_bf16 = w_branch.astype(BF16)
    wo_bf16 = w_out.astype(BF16)
    b_in3 = b_in.reshape(DEPTH, 1, N_IN)
    rnn_conv_b3 = rnn_conv_b.reshape(DEPTH, 1, D_RNN)
    rg_b4 = 0.5 * rg_b.reshape(DEPTH, 4, D_RNN)
    sg_norm_g3 = sg_norm_g.reshape(DEPTH, 1, D_SG)
    sg_b4 = sg_b[..., None]
    cf_conv_b3 = cf_conv_b.reshape(DEPTH, 1, D_CF)
    cf_ln_g3 = cf_ln_g.reshape(DEPTH, 1, D_CF)
    cf_ln_b3 = cf_ln_b.reshape(DEPTH, 1, D_CF)
    ffn_conv_b3 = ffn_conv_b.reshape(DEPTH, 1, 2 * D_FF)

    states = []
    for layer in range(DEPTH):
        l = jnp.full((1,), layer, jnp.int32)
        z = _inproj(l, x, mod4, g_norm, w_in, b_in3)
        ya, st = _rnn(l, z, rnn_conv_w, rnn_conv_b3, rgw, rg_b4, rg_lambda, h0_all)
        yb = _sgu(l, z, sg_norm_g3, sg_w, sg_b4)
        yc = _conformer(l, z, cf_conv_w, cf_conv_b3, cf_ln_g3, cf_ln_b3)
        x = _merge(l, x, ya, yb, yc, z, mod4, g_norm, wb_bf16, wo_bf16)
        x = _ffn(l, x, mod4, g_norm, ffn_up, ffn_conv_w, ffn_conv_b3, ffn_down)
        states.append(st[0, :CTX_TILES])
    states = jnp.stack(states)

    x = _deinterleave(x)
    y_prompt = x[:BATCH * SEQ].reshape(BATCH, SEQ, D_MODEL)
    y_sample = x[BATCH * SEQ:].reshape(DEC_BATCH, DEC_SEQ, D_MODEL)
    new_state = states.reshape(DEPTH, BATCH, 2, D_RNN).transpose(1, 0, 2, 3)
    return (y_prompt, y_sample, new_state)
```

```python
import functools

import jax
import jax.numpy as jnp
import numpy as np
from jax import lax
from jax.experimental import pallas as pl
from jax.experimental.pallas import tpu as pltpu

F32 = jnp.float32
BF16 = jnp.bfloat16

D_MODEL = 1024
BATCH = 16
SEQ = 256
DEPTH = 4
DEC_BATCH = 4
DEC_SEQ = 1024
GRID_W = 64
D_RNN = 1024
H_RNN = 16
BS_RNN = D_RNN // H_RNN
RNN_CONV = 4
C_RG = 8.0
D_SG = 1024
SG_GROUPS = 8
SG_GD = D_SG // SG_GROUPS
CHUNK = 128
D_CF = 1024
CF_CONV = 31
D_FF = 4096
FFN_CONV = 3
N_IN = 2 * D_RNN + 2 * D_SG + 2 * D_CF + 3 * D_MODEL
EPS = 1e-6
POS_BASE = 10000.0

LANES = 128
SUBLANES = 8
MXU_DIM = 256

N_TOK = BATCH * SEQ + DEC_BATCH * DEC_SEQ
TM = 1024
N_TILES = N_TOK // TM
CTX_TILES = BATCH * SEQ // TM
MOD_ROWS = 8
CTX_MOD_ROW = DEC_BATCH
CONV_RB = 64
FF_CW = 512
TM_MERGE = 512
VMEM_LIMIT = 56 * 1024 * 1024


def _sigmoid(x):
    return 0.5 * (jnp.tanh(0.5 * x) + 1.0)


def _rms_rows(x, g):
    return x * lax.rsqrt(jnp.mean(x * x, axis=-1, keepdims=True) + EPS) * g


def _mod_row_of_tile(i, tile_rows):
    tiles_per_latent = DEC_SEQ // tile_rows
    ctx_tiles = BATCH * SEQ // tile_rows
    lat = jnp.maximum(i - ctx_tiles, 0) // tiles_per_latent
    return jnp.where(i < ctx_tiles, CTX_MOD_ROW, lat)


def _params(n_grid):
    return pltpu.CompilerParams(dimension_semantics=("arbitrary",) * n_grid,
                                vmem_limit_bytes=VMEM_LIMIT)


def _layer_call(body, grid, in_specs, out_specs, out_shape, scratch_shapes, name):
    return pl.pallas_call(
        body,
        grid_spec=pltpu.PrefetchScalarGridSpec(
            num_scalar_prefetch=1, grid=grid, in_specs=in_specs, out_specs=out_specs,
            scratch_shapes=scratch_shapes),
        out_shape=out_shape,
        compiler_params=_params(len(grid)),
        name=name)


STEPS = CHUNK
HALO_LW = 256


def _interleave(x):
    return x.reshape(N_TILES, SUBLANES, STEPS, -1).transpose(0, 2, 1, 3).reshape(x.shape)


def _deinterleave(x):
    return x.reshape(N_TILES, STEPS, SUBLANES, -1).transpose(0, 2, 1, 3).reshape(x.shape)


def _chunk_edge_masks(is_ctx, shape):
    c = lax.broadcasted_iota(jnp.int32, shape, 0) & (SUBLANES - 1)
    m = jnp.where(is_ctx, SEQ // CHUNK - 1, DEC_SEQ // CHUNK - 1)
    cm = c & m
    return cm == 0, cm == m


def _fill_time_halo(pad_ref, halo, is_ctx):
    n = halo * SUBLANES
    for lo in range(0, pad_ref.shape[1], HALO_LW):
        lanes = slice(lo, lo + HALO_LW)
        first, last = _chunk_edge_masks(is_ctx, (n, HALO_LW))
        tail = pad_ref[TM:TM + n, lanes]
        head = pad_ref[n:2 * n, lanes]
        pad_ref[0:n, lanes] = jnp.where(first, 0.0, pltpu.roll(tail, 1, 0))
        pad_ref[n + TM:2 * n + TM, lanes] = jnp.where(last, 0.0, pltpu.roll(head, n - 1, 0))


def _expand_rows(dst_ref, row0, rows):
    for k in range(rows.shape[0]):
        lo = row0 + k * SUBLANES
        dst_ref[lo:lo + SUBLANES, :] = jnp.broadcast_to(rows[k:k + 1, :], (SUBLANES, rows.shape[1]))


def _conv_block(pad_ref, row0, lanes, wexp_ref, w0, n_taps):
    def wrow(k):
        return wexp_ref[w0 + k * SUBLANES:w0 + (k + 1) * SUBLANES, lanes]

    groups = CONV_RB // SUBLANES
    acc = jnp.broadcast_to(wrow(n_taps), (groups, SUBLANES, LANES))
    for k in range(n_taps):
        x = pad_ref[pl.ds(row0 + k * SUBLANES, CONV_RB), lanes]
        acc = acc + x.reshape(groups, SUBLANES, LANES) * wrow(k)
    return acc.reshape(CONV_RB, LANES)


def _dwconv(pad_ref, halo, first_offset, n_taps, wexp_ref, emit, post=None):
    def body(b, carry):
        base = pl.multiple_of(b * CONV_RB, CONV_RB)
        for lb in range(pad_ref.shape[1] // LANES):
            lanes = slice(lb * LANES, (lb + 1) * LANES)
            emit(base, lb, _conv_block(pad_ref, base + (halo + first_offset) * SUBLANES, lanes,
                                       wexp_ref, 0, n_taps))
        if post is not None:
            post(base)
        return carry

    lax.fori_loop(0, TM // CONV_RB, body, 0)


def _prep_body(xp_ref, xs_ref, pos_ref, o_ref):
    i = pl.program_id(0)

    @pl.when(i < CTX_TILES)
    def _():
        o_ref[...] = xp_ref[...]

    @pl.when(i >= CTX_TILES)
    def _():
        o_ref[...] = xs_ref[...] + pos_ref[...]


def _prep(xp, xs, pos):
    return pl.pallas_call(
        _prep_body,
        grid=(N_TILES,),
        in_specs=[pl.BlockSpec((TM, D_MODEL), lambda i: (jnp.minimum(i, CTX_TILES - 1), 0)),
                  pl.BlockSpec((TM, D_MODEL), lambda i: (jnp.maximum(i - CTX_TILES, 0), 0)),
                  pl.BlockSpec((DEC_SEQ, D_MODEL), lambda i: (0, 0))],
        out_specs=pl.BlockSpec((TM, D_MODEL), lambda i: (i, 0)),
        out_shape=jax.ShapeDtypeStruct((N_TOK, D_MODEL), F32),
        compiler_params=_params(1),
        name="prep")(xp, xs, pos)


MOD_TN = 1536


def _mod_body(cond_ref, w_ref, b_ref, o_ref):
    c = cond_ref[...]
    s = (c * _sigmoid(c)).astype(BF16)
    o_ref[0] = jnp.dot(s, w_ref[0].astype(BF16), preferred_element_type=F32) + b_ref[0]


def _mod(cond, w_mod, b_mod):
    n_mod = 6 * D_MODEL
    return pl.pallas_call(
        _mod_body,
        grid=(DEPTH, n_mod // MOD_TN),
        in_specs=[pl.BlockSpec((MOD_ROWS, D_MODEL), lambda l, j: (0, 0)),
                  pl.BlockSpec((1, D_MODEL, MOD_TN), lambda l, j: (l, 0, j)),
                  pl.BlockSpec((1, 1, MOD_TN), lambda l, j: (l, 0, j))],
        out_specs=pl.BlockSpec((1, MOD_ROWS, MOD_TN), lambda l, j: (l, 0, j)),
        out_shape=jax.ShapeDtypeStruct((DEPTH, MOD_ROWS, n_mod), F32),
        compiler_params=_params(2),
        name="mod")(cond, w_mod, b_mod.reshape(DEPTH, 1, n_mod))


IN_TN = 1024
NORM_RB = 64


def _modulated_norm_to(h_ref, x_ref, g, scale, shift):
    def body(rb, carry):
        rows = pl.ds(pl.multiple_of(rb * NORM_RB, NORM_RB), NORM_RB)
        h = _rms_rows(x_ref[rows, :], g) * (1.0 + scale) + shift
        h_ref[rows, :] = h.astype(h_ref.dtype)
        return carry
    lax.fori_loop(0, x_ref.shape[0] // NORM_RB, body, 0, unroll=2)


def _inproj_body(l_ref, x_ref, mod_ref, gn_ref, w_ref, b_ref, z_ref, h_ref):
    i = pl.program_id(0)
    j = pl.program_id(1)

    @pl.when(j == 0)
    def _():
        m = mod_ref[0, _mod_row_of_tile(i, TM)]
        _modulated_norm_to(h_ref, x_ref, gn_ref[0, 0:1, :],
                           m[:, D_MODEL:2 * D_MODEL], m[:, 0:D_MODEL])

    acc = jnp.dot(h_ref[...], w_ref[0].astype(BF16), preferred_element_type=F32)
    z_ref[...] = (acc + b_ref[0]).astype(z_ref.dtype)


def _inproj(l, x, mod4, g_norm, w_in, b_in3):
    return _layer_call(
        _inproj_body,
        grid=(N_TILES, N_IN // IN_TN),
        in_specs=[pl.BlockSpec((TM, D_MODEL), lambda i, j, l: (i, 0)),
                  pl.BlockSpec((1, MOD_ROWS, 1, 6 * D_MODEL), lambda i, j, l: (l[0], 0, 0, 0)),
                  pl.BlockSpec((1, 4, D_MODEL), lambda i, j, l: (l[0], 0, 0)),
                  pl.BlockSpec((1, D_MODEL, IN_TN), lambda i, j, l: (l[0], 0, j)),
                  pl.BlockSpec((1, 1, IN_TN), lambda i, j, l: (l[0], 0, j))],
        out_specs=pl.BlockSpec((TM, IN_TN), lambda i, j, l: (i, j)),
        out_shape=jax.ShapeDtypeStruct((N_TOK, N_IN), BF16),
        scratch_shapes=[pltpu.VMEM((TM, D_MODEL), BF16)],
        name="inproj")(l, x, mod4, g_norm, w_in, b_in3)


RNN_HALO = 2
GATE_RB = 256
FILL_RB = 64


def _rows_of(value):
    return [value[r:r + 1, :] for r in range(SUBLANES)]


def _chain_chunks(a_end, b_end, h0_rows, is_ctx, reverse):
    per_ctx_seq = SEQ // CHUNK
    a_rows, b_rows = _rows_of(a_end), _rows_of(b_end)
    h_in, h_out = [None] * SUBLANES, [None] * SUBLANES
    order = range(SUBLANES - 1, -1, -1) if reverse else range(SUBLANES)
    prev = None
    for c in order:
        starts_ctx_seq = (c % per_ctx_seq == per_ctx_seq - 1) if reverse else (c % per_ctx_seq == 0)
        if prev is None:
            h = jnp.where(is_ctx, h0_rows[c // per_ctx_seq], h0_rows[0])
        elif starts_ctx_seq:
            h = jnp.where(is_ctx, h0_rows[c // per_ctx_seq], prev)
        else:
            h = prev
        h_in[c] = h
        prev = a_rows[c] * h + b_rows[c]
        h_out[c] = prev
    return jnp.concatenate(h_in, axis=0), h_out


def _rnn_body(l_ref, zx_ref, zg_ref, cw_ref, cb_ref, rgw_ref, rgb_ref, lam_ref, h0_ref,
              ya_ref, st_ref, pad_ref, wexp_ref, xc_ref, af_ref, uf_ref, ab_ref, ub_ref):
    i = pl.program_id(0)
    is_ctx = i < CTX_TILES
    halo_rows = RNN_HALO * SUBLANES

    def fill(rb, carry):
        r0 = pl.multiple_of(rb * FILL_RB, FILL_RB)
        pad_ref[pl.ds(r0 + halo_rows, FILL_RB), :] = zx_ref[pl.ds(r0, FILL_RB), :].astype(F32)
        return carry

    lax.fori_loop(0, TM // FILL_RB, fill, 0)
    _fill_time_halo(pad_ref, RNN_HALO, is_ctx)

    _expand_rows(wexp_ref, 0, cw_ref[0])
    _expand_rows(wexp_ref, RNN_CONV * SUBLANES, cb_ref[0])

    def emit_xc(row0, lb, block):
        xc_ref[pl.ds(row0, CONV_RB), lb * LANES:(lb + 1) * LANES] = block

    _dwconv(pad_ref, RNN_HALO, -2, RNN_CONV, wexp_ref, emit_xc)

    neg_lam = -lam_ref[0]
    softplus = jnp.maximum(neg_lam, 0.0) + jnp.log1p(jnp.exp(-jnp.abs(neg_lam)))
    coef = (-0.5 * C_RG * np.log2(np.e)) * softplus
    a_refs = (af_ref, ab_ref)
    u_refs = (uf_ref, ub_ref)
    n_col = D_RNN // MXU_DIM

    def gate_body(rb, carry):
        rows = pl.ds(pl.multiple_of(rb * GATE_RB, GATE_RB), GATE_RB)
        for j in range(n_col):
            lanes = slice(j * MXU_DIM, (j + 1) * MXU_DIM)
            xc = xc_ref[rows, lanes]
            xcb = xc.astype(BF16)
            half_xc = 0.5 * xc
            for d in range(2):
                base = d * 2 * n_col
                g_r = jnp.dot(xcb, rgw_ref[0, base + j], preferred_element_type=F32)
                g_i = jnp.dot(xcb, rgw_ref[0, base + n_col + j], preferred_element_type=F32)
                t_r = jnp.tanh(g_r + rgb_ref[0, 2 * d:2 * d + 1, lanes])
                t_i = jnp.tanh(g_i + rgb_ref[0, 2 * d + 1:2 * d + 2, lanes])
                cf = coef[d:d + 1, lanes]
                a = jnp.exp2(cf * t_r + cf)
                y = 1.0 - a * a
                root = jnp.where(y == 0.0, 0.0, y * lax.rsqrt(y))
                a_refs[d][rows, lanes] = a
                u_refs[d][rows, lanes] = root * ((t_i + 1.0) * half_xc)
        return carry

    lax.fori_loop(0, TM // GATE_RB, gate_body, 0)

    def group(t):
        return pl.ds(pl.multiple_of(t * SUBLANES, SUBLANES), SUBLANES)

    def totals(t, carry):
        a_f, b_f, a_b, b_b = carry
        rf, rb = group(t), group(STEPS - 1 - t)
        a = af_ref[rf, :]
        b_f = a * b_f + uf_ref[rf, :]
        a_f = a * a_f
        a = ab_ref[rb, :]
        b_b = a * b_b + ub_ref[rb, :]
        a_b = a * a_b
        return a_f, b_f, a_b, b_b

    ones = jnp.ones((SUBLANES, D_RNN), F32)
    zeros = jnp.zeros((SUBLANES, D_RNN), F32)
    a_f, b_f, a_b, b_b = lax.fori_loop(0, STEPS, totals, (ones, zeros, ones, zeros))

    n_seq = TM // SEQ
    h0_f = [h0_ref[0, 0, 2 * s:2 * s + 1, :] for s in range(n_seq)]
    h0_b = [h0_ref[0, 0, 2 * s + 1:2 * s + 2, :] for s in range(n_seq)]
    hin_f, hout_f = _chain_chunks(a_f, b_f, h0_f, is_ctx, False)
    hin_b, hout_b = _chain_chunks(a_b, b_b, h0_b, is_ctx, True)

    per_ctx_seq = SEQ // CHUNK
    none = jnp.zeros((1, D_RNN), F32)
    for s in range(n_seq):
        last_f = hout_f[s * per_ctx_seq + per_ctx_seq - 1]
        first_b = hout_b[s * per_ctx_seq]
        st_ref[0, 0, 2 * s:2 * s + 1, :] = jnp.where(is_ctx, last_f, hout_f[SUBLANES - 1] if s == 0 else none)
        st_ref[0, 0, 2 * s + 1:2 * s + 2, :] = jnp.where(is_ctx, first_b, hout_b[0] if s == 0 else none)

    def backward(t, h):
        rb = group(STEPS - 1 - t)
        h = ab_ref[rb, :] * h + ub_ref[rb, :]
        ub_ref[rb, :] = h
        return h

    lax.fori_loop(0, STEPS, backward, hin_b)

    def forward(tt, h):
        r0, r1 = group(2 * tt), group(2 * tt + 1)
        h0 = af_ref[r0, :] * h + uf_ref[r0, :]
        h1 = af_ref[r1, :] * h0 + uf_ref[r1, :]
        rows = pl.ds(pl.multiple_of(tt * 2 * SUBLANES, 2 * SUBLANES), 2 * SUBLANES)
        both = jnp.concatenate([h0, h1], axis=0) + ub_ref[rows, :]
        ya_ref[rows, :] = (both * jax.nn.gelu(zg_ref[rows, :].astype(F32))).astype(ya_ref.dtype)
        return h1

    lax.fori_loop(0, STEPS // 2, forward, hin_f)


def _rnn(l, z, rnn_conv_w, rnn_conv_b3, rgw, rg_b4, rg_lambda, h0_all):
    n_rgw = 2 * 2 * (D_RNN // MXU_DIM)
    return _layer_call(
        _rnn_body,
        grid=(N_TILES,),
        in_specs=[pl.BlockSpec((TM, D_RNN), lambda i, l: (i, 0)),
                  pl.BlockSpec((TM, D_RNN), lambda i, l: (i, 1)),
                  pl.BlockSpec((1, RNN_CONV, D_RNN), lambda i, l: (l[0], 0, 0)),
                  pl.BlockSpec((1, 1, D_RNN), lambda i, l: (l[0], 0, 0)),
                  pl.BlockSpec((1, n_rgw, MXU_DIM, MXU_DIM), lambda i, l: (l[0], 0, 0, 0)),
                  pl.BlockSpec((1, 4, D_RNN), lambda i, l: (l[0], 0, 0)),
                  pl.BlockSpec((1, 2, D_RNN), lambda i, l: (l[0], 0, 0)),
                  pl.BlockSpec((1, 1, SUBLANES, D_RNN), lambda i, l: (l[0], i, 0, 0))],
        out_specs=[pl.BlockSpec((TM, D_MODEL), lambda i, l: (i, 0)),
                   pl.BlockSpec((1, 1, SUBLANES, D_RNN), lambda i, l: (0, i, 0, 0))],
        out_shape=[jax.ShapeDtypeStruct((N_TOK, D_RNN), BF16),
                   jax.ShapeDtypeStruct((1, N_TILES, SUBLANES, D_RNN), F32)],
        scratch_shapes=[pltpu.VMEM((TM + 2 * RNN_HALO * SUBLANES, D_RNN), F32),
                        pltpu.VMEM(((RNN_CONV + 1) * SUBLANES, D_RNN), F32),
                        pltpu.VMEM((TM, D_RNN), F32),
                        pltpu.VMEM((TM, D_RNN), F32),
                        pltpu.VMEM((TM, D_RNN), F32),
                        pltpu.VMEM((TM, D_RNN), F32),
                        pltpu.VMEM((TM, D_RNN), F32)],
        name="rnn")(l, z, z, rnn_conv_w, rnn_conv_b3, rgw, rg_b4, rg_lambda, h0_all)


ACT_RB = 64


def _split3(x):
    hi = x.astype(BF16)
    r1 = x - hi.astype(F32)
    mid = r1.astype(BF16)
    lo = (r1 - mid.astype(F32)).astype(BF16)
    return hi, mid, lo


def _sgu_body(l_ref, z_ref, g_ref, w_ref, b_ref, yb_ref, su_ref, sv_ref, kron_ref, bias_ref):
    i = pl.program_id(0)
    g = pl.program_id(1)

    @pl.when(i == 0)
    def _():
        shift = SUBLANES.bit_length() - 1
        rep = jnp.where(jnp.right_shift(lax.broadcasted_iota(jnp.int32, (TM, CHUNK), 0), shift)
                        == lax.broadcasted_iota(jnp.int32, (TM, CHUNK), 1), 1.0, 0.0).astype(BF16)
        rep_t = jnp.where(jnp.right_shift(lax.broadcasted_iota(jnp.int32, (CHUNK, TM), 1), shift)
                          == lax.broadcasted_iota(jnp.int32, (CHUNK, TM), 0), 1.0, 0.0).astype(BF16)
        rows = jnp.dot(rep, w_ref[0, 0].astype(BF16), preferred_element_type=F32).astype(BF16)
        full = jnp.dot(rows, rep_t, preferred_element_type=F32)
        same_chunk = ((lax.broadcasted_iota(jnp.int32, (TM, TM), 0) & (SUBLANES - 1))
                      == (lax.broadcasted_iota(jnp.int32, (TM, TM), 1) & (SUBLANES - 1)))
        kron_ref[g] = jnp.where(same_chunk, full, 0.0).astype(BF16)
        bias = jnp.broadcast_to(b_ref[0, 0], (CHUNK, SG_GD))
        bias_ref[g] = sum(jnp.dot(rep, piece, preferred_element_type=F32) for piece in _split3(bias))

    @pl.when(g == 0)
    def _():
        gain = g_ref[0]

        def act_body(rb, carry):
            rows = pl.ds(pl.multiple_of(rb * ACT_RB, ACT_RB), ACT_RB)
            uv = jax.nn.gelu(z_ref[rows, :].astype(F32))
            sv = uv[:, D_SG:]
            mu = jnp.mean(sv, axis=-1, keepdims=True)
            svc = sv - mu
            y = svc * lax.rsqrt(jnp.mean(svc * svc, axis=-1, keepdims=True) + EPS) * gain
            for j in range(SG_GROUPS):
                su_ref[j, rows, :] = uv[:, j * SG_GD:(j + 1) * SG_GD]
                sv_ref[j, rows, :] = y[:, j * SG_GD:(j + 1) * SG_GD].astype(sv_ref.dtype)
            return carry

        lax.fori_loop(0, TM // ACT_RB, act_body, 0)

    mixed = jnp.dot(kron_ref[g], sv_ref[g], preferred_element_type=F32) + bias_ref[g]
    yb_ref[...] = (su_ref[g] * mixed).astype(yb_ref.dtype)


def _sgu(l, z, sg_norm_g3, sg_w, sg_b4):
    return _layer_call(
        _sgu_body,
        grid=(N_TILES, SG_GROUPS),
        in_specs=[pl.BlockSpec((TM, 2 * D_SG), lambda i, g, l: (i, 1)),
                  pl.BlockSpec((1, 1, D_SG), lambda i, g, l: (l[0], 0, 0)),
                  pl.BlockSpec((1, 1, CHUNK, CHUNK), lambda i, g, l: (l[0], g, 0, 0)),
                  pl.BlockSpec((1, 1, CHUNK, 1), lambda i, g, l: (l[0], g, 0, 0))],
        out_specs=pl.BlockSpec((TM, SG_GD), lambda i, g, l: (i, g)),
        out_shape=jax.ShapeDtypeStruct((N_TOK, D_SG), BF16),
        scratch_shapes=[pltpu.VMEM((SG_GROUPS, TM, SG_GD), F32),
                        pltpu.VMEM((SG_GROUPS, TM, SG_GD), BF16),
                        pltpu.VMEM((SG_GROUPS, TM, TM), BF16),
                        pltpu.VMEM((SG_GROUPS, TM, SG_GD), F32)],
        name="sgu")(l, z, sg_norm_g3, sg_w, sg_b4)


CF_HALO = CF_CONV // 2


def _conformer_body(l_ref, z_ref, cw_ref, cb_ref, g_ref, b_ref, yc_ref, pad_ref, wexp_ref, blk_ref):
    i = pl.program_id(0)
    halo_rows = CF_HALO * SUBLANES

    def glu(rb, carry):
        r0 = pl.multiple_of(rb * FILL_RB, FILL_RB)
        zz = z_ref[pl.ds(r0, FILL_RB), :].astype(F32)
        pad_ref[pl.ds(r0 + halo_rows, FILL_RB), :] = zz[:, :D_CF] * _sigmoid(zz[:, D_CF:])
        return carry

    lax.fori_loop(0, TM // FILL_RB, glu, 0)
    _fill_time_halo(pad_ref, CF_HALO, i < CTX_TILES)

    _expand_rows(wexp_ref, 0, cw_ref[0])
    _expand_rows(wexp_ref, CF_CONV * SUBLANES, cb_ref[0])

    def emit(row0, lb, block):
        blk_ref[:, lb * LANES:(lb + 1) * LANES] = block

    def post(row0):
        y = blk_ref[...]
        mu = jnp.mean(y, axis=-1, keepdims=True)
        yc = y - mu
        ln = yc * lax.rsqrt(jnp.mean(yc * yc, axis=-1, keepdims=True) + EPS) * g_ref[0] + b_ref[0]
        yc_ref[pl.ds(row0, CONV_RB), :] = (ln * _sigmoid(ln)).astype(yc_ref.dtype)

    _dwconv(pad_ref, CF_HALO, -CF_HALO, CF_CONV, wexp_ref, emit, post)


def _conformer(l, z, cf_conv_w, cf_conv_b3, cf_ln_g3, cf_ln_b3):
    return _layer_call(
        _conformer_body,
        grid=(N_TILES,),
        in_specs=[pl.BlockSpec((TM, 2 * D_CF), lambda i, l: (i, 2)),
                  pl.BlockSpec((1, CF_CONV, D_CF), lambda i, l: (l[0], 0, 0)),
                  pl.BlockSpec((1, 1, D_CF), lambda i, l: (l[0], 0, 0)),
                  pl.BlockSpec((1, 1, D_CF), lambda i, l: (l[0], 0, 0)),
                  pl.BlockSpec((1, 1, D_CF), lambda i, l: (l[0], 0, 0))],
        out_specs=pl.BlockSpec((TM, D_CF), lambda i, l: (i, 0)),
        out_shape=jax.ShapeDtypeStruct((N_TOK, D_CF), BF16),
        scratch_shapes=[pltpu.VMEM((TM + 2 * CF_HALO * SUBLANES, D_CF), F32),
                        pltpu.VMEM(((CF_CONV + 1) * SUBLANES, D_CF), F32),
                        pltpu.VMEM((CONV_RB, D_CF), F32)],
        name="conformer")(l, z, cf_conv_w, cf_conv_b3, cf_ln_g3, cf_ln_b3)


def _merge_body(l_ref, x_ref, ya_ref, yb_ref, yc_ref, zg_ref, mod_ref, gn_ref, wb_ref, wo_ref,
                o_ref):
    i = pl.program_id(0)
    merged = None
    for p, y_ref in enumerate((ya_ref, yb_ref, yc_ref)):
        gate = _sigmoid(zg_ref[:, p * D_MODEL:(p + 1) * D_MODEL].astype(F32))
        term = gate * jnp.dot(y_ref[...], wb_ref[0, p], preferred_element_type=F32)
        merged = term if merged is None else merged + term
    out = jnp.dot(merged.astype(BF16), wo_ref[0], preferred_element_type=F32)
    m = mod_ref[0, _mod_row_of_tile(i, TM_MERGE)]
    gate1 = m[:, 2 * D_MODEL:3 * D_MODEL]
    o_ref[...] = x_ref[...] + gate1 * _rms_rows(out, gn_ref[0, 1:2, :])


def _merge(l, x, ya, yb, yc, z, mod4, g_norm, wb_bf16, wo_bf16):
    tile = lambda i, l: (i, 0)
    return _layer_call(
        _merge_body,
        grid=(N_TOK // TM_MERGE,),
        in_specs=[pl.BlockSpec((TM_MERGE, D_MODEL), tile),
                  pl.BlockSpec((TM_MERGE, D_MODEL), tile),
                  pl.BlockSpec((TM_MERGE, D_MODEL), tile),
                  pl.BlockSpec((TM_MERGE, D_MODEL), tile),
                  pl.BlockSpec((TM_MERGE, 3 * D_MODEL), lambda i, l: (i, 2)),
                  pl.BlockSpec((1, MOD_ROWS, 1, 6 * D_MODEL), lambda i, l: (l[0], 0, 0, 0)),
                  pl.BlockSpec((1, 4, D_MODEL), lambda i, l: (l[0], 0, 0)),
                  pl.BlockSpec((1, 3, D_MODEL, D_MODEL), lambda i, l: (l[0], 0, 0, 0)),
                  pl.BlockSpec((1, D_MODEL, D_MODEL), lambda i, l: (l[0], 0, 0))],
        out_specs=pl.BlockSpec((TM_MERGE, D_MODEL), tile),
        out_shape=jax.ShapeDtypeStruct((N_TOK, D_MODEL), F32),
        scratch_shapes=[],
        name="merge")(l, x, ya, yb, yc, z, mod4, g_norm, wb_bf16, wo_bf16)


FFN_HALO = 1
N_FF_STEPS = D_FF // FF_CW
FFN_BLOCKS = TM // CONV_RB
FFN_SPLIT = (TM // 2 - FFN_HALO * SUBLANES) // CONV_RB


def _ffn_body(l_ref, x_ref, mod_ref, gn_ref, wg_ref, wv_ref, cwg_ref, cwv_ref, cbg_ref, cbv_ref,
              wd_ref, o_ref, h_ref, acc_ref, padg_ref, padv_ref, act_ref, wexp_ref):
    i = pl.program_id(0)
    c = pl.program_id(1)
    m = mod_ref[0, _mod_row_of_tile(i, TM)]
    halo_rows = FFN_HALO * SUBLANES

    @pl.when(c == 0)
    def _():
        _modulated_norm_to(h_ref, x_ref, gn_ref[0, 2:3, :],
                           m[:, 4 * D_MODEL:5 * D_MODEL], m[:, 3 * D_MODEL:4 * D_MODEL])
        acc_ref[...] = jnp.zeros_like(acc_ref)

    def up(r0, r1):
        h = h_ref[r0:r1, :]
        padg_ref[halo_rows + r0:halo_rows + r1, :] = jnp.dot(h, wg_ref[0, 0], preferred_element_type=F32)
        padv_ref[halo_rows + r0:halo_rows + r1, :] = jnp.dot(h, wv_ref[0, 0], preferred_element_type=F32)

    n_wrows = (FFN_CONV + 1) * SUBLANES
    for s, (cw_ref, cb_ref) in enumerate(((cwg_ref, cbg_ref), (cwv_ref, cbv_ref))):
        _expand_rows(wexp_ref, s * n_wrows, cw_ref[0])
        _expand_rows(wexp_ref, s * n_wrows + FFN_CONV * SUBLANES, cb_ref[0])

    def conv_act(rb):
        base = rb * CONV_RB
        for lb in range(FF_CW // LANES):
            lanes = slice(lb * LANES, (lb + 1) * LANES)
            outs = []
            for s, pad_ref in enumerate((padg_ref, padv_ref)):
                outs.append(_conv_block(pad_ref, base, lanes, wexp_ref, s * n_wrows, FFN_CONV))
            act_ref[base:base + CONV_RB, lanes] = (jax.nn.gelu(outs[0]) * outs[1]).astype(act_ref.dtype)

    def down(r0, r1):
        acc_ref[r0:r1, :] += jnp.dot(act_ref[r0:r1, :], wd_ref[0], preferred_element_type=F32)

    up(0, TM // 2)
    up(TM // 2, TM)
    is_ctx = i < CTX_TILES
    _fill_time_halo(padg_ref, FFN_HALO, is_ctx)
    _fill_time_halo(padv_ref, FFN_HALO, is_ctx)
    for rb in range(1, FFN_SPLIT):
        conv_act(rb)
    conv_act(0)
    down(0, FFN_SPLIT * CONV_RB)
    for rb in range(FFN_SPLIT, FFN_BLOCKS):
        conv_act(rb)
    down(FFN_SPLIT * CONV_RB, TM)

    @pl.when(c == N_FF_STEPS - 1)
    def _():
        gate2 = m[:, 5 * D_MODEL:6 * D_MODEL]
        g3 = gn_ref[0, 3:4, :]

        def body(rb, carry):
            rows = pl.ds(pl.multiple_of(rb * NORM_RB, NORM_RB), NORM_RB)
            o_ref[rows, :] = x_ref[rows, :] + gate2 * _rms_rows(acc_ref[rows, :], g3)
            return carry
        lax.fori_loop(0, TM // NORM_RB, body, 0, unroll=2)


def _ffn(l, x, mod4, g_norm, ffn_up_chunks, ffn_conv_w, ffn_conv_b3, ffn_down_bf16):
    return _layer_call(
        _ffn_body,
        grid=(N_TILES, N_FF_STEPS),
        in_specs=[pl.BlockSpec((TM, D_MODEL), lambda i, c, l: (i, 0)),
                  pl.BlockSpec((1, MOD_ROWS, 1, 6 * D_MODEL), lambda i, c, l: (l[0], 0, 0, 0)),
                  pl.BlockSpec((1, 4, D_MODEL), lambda i, c, l: (l[0], 0, 0)),
                  pl.BlockSpec((1, 1, D_MODEL, FF_CW), lambda i, c, l: (l[0], c, 0, 0)),
                  pl.BlockSpec((1, 1, D_MODEL, FF_CW), lambda i, c, l: (l[0], N_FF_STEPS + c, 0, 0)),
                  pl.BlockSpec((1, FFN_CONV, FF_CW), lambda i, c, l: (l[0], 0, c)),
                  pl.BlockSpec((1, FFN_CONV, FF_CW), lambda i, c, l: (l[0], 0, N_FF_STEPS + c)),
                  pl.BlockSpec((1, 1, FF_CW), lambda i, c, l: (l[0], 0, c)),
                  pl.BlockSpec((1, 1, FF_CW), lambda i, c, l: (l[0], 0, N_FF_STEPS + c)),
                  pl.BlockSpec((1, FF_CW, D_MODEL), lambda i, c, l: (l[0], c, 0))],
        out_specs=pl.BlockSpec((TM, D_MODEL), lambda i, c, l: (i, 0)),
        out_shape=jax.ShapeDtypeStruct((N_TOK, D_MODEL), F32),
        scratch_shapes=[pltpu.VMEM((TM, D_MODEL), BF16),
                        pltpu.VMEM((TM, D_MODEL), F32),
                        pltpu.VMEM((TM + 2 * FFN_HALO * SUBLANES, FF_CW), F32),
                        pltpu.VMEM((TM + 2 * FFN_HALO * SUBLANES, FF_CW), F32),
                        pltpu.VMEM((TM, FF_CW), BF16),
                        pltpu.VMEM((2 * (FFN_CONV + 1) * SUBLANES, FF_CW), F32)],
        name="ffn")(l, x, mod4, g_norm, ffn_up_chunks, ffn_up_chunks, ffn_conv_w, ffn_conv_w,
                    ffn_conv_b3, ffn_conv_b3, ffn_down_bf16)


def _pos_table():
    t = jnp.arange(DEC_SEQ)
    r = (t // GRID_W).astype(F32)
    col = (t % GRID_W).astype(F32)
    q = D_MODEL // 4
    omega = 1.0 / (POS_BASE ** (jnp.arange(q, dtype=F32) / q))

    def emb(p):
        ang = p[:, None] * omega[None, :]
        return jnp.concatenate([jnp.sin(ang), jnp.cos(ang)], axis=-1)
    return jnp.concatenate([emb(r), emb(col)], axis=-1).astype(F32)


def _block_diag_gates(rg_w):
    per_tile = MXU_DIM // BS_RNN
    n_col = H_RNN // per_tile
    w = rg_w.reshape(DEPTH, 2, 2, n_col, per_tile, BS_RNN, BS_RNN)
    eye = jnp.eye(per_tile, dtype=rg_w.dtype)
    tiles = jnp.einsum('ldkcaij,ab->ldkcaibj', w, eye)
    return tiles.reshape(DEPTH, 2 * 2 * n_col, MXU_DIM, MXU_DIM).astype(BF16)


def kernel(x_prompt, x_sample, state_rglru, c, c_ctx, w_mod, b_mod, g_norm, w_in, b_in,
           rnn_conv_w, rnn_conv_b, rg_w, rg_b, rg_lambda, sg_norm_g, sg_w, sg_b,
           cf_conv_w, cf_conv_b, cf_ln_g, cf_ln_b, w_branch, w_out, ffn_up, ffn_conv_w,
           ffn_conv_b, ffn_down):
    x = _interleave(_prep(x_prompt.reshape(BATCH * SEQ, D_MODEL),
                          x_sample.reshape(DEC_BATCH * DEC_SEQ, D_MODEL), _pos_table()))

    cond = jnp.zeros((MOD_ROWS, D_MODEL), F32).at[:DEC_BATCH].set(c).at[CTX_MOD_ROW].set(c_ctx)
    mod4 = _mod(cond, w_mod, b_mod).reshape(DEPTH, MOD_ROWS, 1, 6 * D_MODEL)

    h0_all = jnp.zeros((DEPTH, N_TILES, SUBLANES, D_RNN), F32)
    h0_all = h0_all.at[:, CTX_TILES:, 0:2, :].set(jnp.transpose(state_rglru.astype(F32), (1, 0, 2, 3)))

    rgw = _block_diag_gates(0.5 * rg_w)
    wb_bf16 = w_branch.astype(BF16)
    wo_bf16 = w_out.astype(BF16)
    b_in3 = b_in.reshape(DEPTH, 1, N_IN)
    rnn_conv_b3 = rnn_conv_b.reshape(DEPTH, 1, D_RNN)
    rg_b4 = 0.5 * rg_b.reshape(DEPTH, 4, D_RNN)
    sg_norm_g3 = sg_norm_g.reshape(DEPTH, 1, D_SG)
    sg_b4 = sg_b[..., None]
    cf_conv_b3 = cf_conv_b.reshape(DEPTH, 1, D_CF)
    cf_ln_g3 = cf_ln_g.reshape(DEPTH, 1, D_CF)
    cf_ln_b3 = cf_ln_b.reshape(DEPTH, 1, D_CF)
    ffn_conv_b3 = ffn_conv_b.reshape(DEPTH, 1, 2 * D_FF)
    ffn_up_chunks = ffn_up.reshape(DEPTH, D_MODEL, 2 * D_FF // FF_CW, FF_CW).transpose(0, 2, 1, 3).astype(BF16)
    ffn_down_bf16 = ffn_down.astype(BF16)

    states = []
    for layer in range(DEPTH):
        l = jnp.full((1,), layer, jnp.int32)
        z = _inproj(l, x, mod4, g_norm, w_in, b_in3)
        ya, st = _rnn(l, z, rnn_conv_w, rnn_conv_b3, rgw, rg_b4, rg_lambda, h0_all)
        yb = _sgu(l, z, sg_norm_g3, sg_w, sg_b4)
        yc = _conformer(l, z, cf_conv_w, cf_conv_b3, cf_ln_g3, cf_ln_b3)
        x = _merge(l, x, ya, yb, yc, z, mod4, g_norm, wb_bf16, wo_bf16)
        x = _ffn(l, x, mod4, g_norm, ffn_up_chunks, ffn_conv_w, ffn_conv_b3, ffn_down_bf16)
        states.append(st[0, :CTX_TILES])
    states = jnp.stack(states)

    x = x.reshape(N_TILES, STEPS, SUBLANES, D_MODEL)
    y_prompt = x[:CTX_TILES].transpose(0, 2, 1, 3).reshape(BATCH, SEQ, D_MODEL)
    y_sample = x[CTX_TILES:].transpose(0, 2, 1, 3).reshape(DEC_BATCH, DEC_SEQ, D_MODEL)
    new_state = states.reshape(DEPTH, BATCH, 2, D_RNN).transpose(1, 0, 2, 3)
    return (y_prompt, y_sample, new_state)
```

```python
import functools

import jax
import jax.numpy as jnp
import numpy as np
from jax import lax
from jax.experimental import pallas as pl
from jax.experimental.pallas import tpu as pltpu

F32 = jnp.float32
BF16 = jnp.bfloat16

D_MODEL = 1024
BATCH = 16
SEQ = 256
DEPTH = 4
DEC_BATCH = 4
DEC_SEQ = 1024
GRID_W = 64
D_RNN = 1024
H_RNN = 16
BS_RNN = D_RNN // H_RNN
RNN_CONV = 4
C_RG = 8.0
D_SG = 1024
SG_GROUPS = 8
SG_GD = D_SG // SG_GROUPS
CHUNK = 128
D_CF = 1024
CF_CONV = 31
D_FF = 4096
FFN_CONV = 3
N_IN = 2 * D_RNN + 2 * D_SG + 2 * D_CF + 3 * D_MODEL
EPS = 1e-6
POS_BASE = 10000.0

LANES = 128
SUBLANES = 8
MXU_DIM = 256

N_TOK = BATCH * SEQ + DEC_BATCH * DEC_SEQ
TM = 1024
N_TILES = N_TOK // TM
CTX_TILES = BATCH * SEQ // TM
MOD_ROWS = 8
CTX_MOD_ROW = DEC_BATCH
CONV_RB = 64
FF_CW = 512
TM_MERGE = 512
VMEM_LIMIT = 56 * 1024 * 1024


def _sigmoid(x):
    return 0.5 * (jnp.tanh(0.5 * x) + 1.0)


def _rms_rows(x, g):
    return x * lax.rsqrt(jnp.mean(x * x, axis=-1, keepdims=True) + EPS) * g


def _mod_row_of_tile(i, tile_rows):
    tiles_per_latent = DEC_SEQ // tile_rows
    ctx_tiles = BATCH * SEQ // tile_rows
    lat = jnp.maximum(i - ctx_tiles, 0) // tiles_per_latent
    return jnp.where(i < ctx_tiles, CTX_MOD_ROW, lat)


def _params(n_grid):
    return pltpu.CompilerParams(dimension_semantics=("arbitrary",) * n_grid,
                                vmem_limit_bytes=VMEM_LIMIT)


def _layer_call(body, grid, in_specs, out_specs, out_shape, scratch_shapes, name):
    return pl.pallas_call(
        body,
        grid_spec=pltpu.PrefetchScalarGridSpec(
            num_scalar_prefetch=1, grid=grid, in_specs=in_specs, out_specs=out_specs,
            scratch_shapes=scratch_shapes),
        out_shape=out_shape,
        compiler_params=_params(len(grid)),
        name=name)


STEPS = CHUNK
HALO_LW = 256


def _interleave(x):
    return x.reshape(N_TILES, SUBLANES, STEPS, -1).transpose(0, 2, 1, 3).reshape(x.shape)


def _deinterleave(x):
    return x.reshape(N_TILES, STEPS, SUBLANES, -1).transpose(0, 2, 1, 3).reshape(x.shape)


def _chunk_edge_masks(is_ctx, shape):
    c = lax.broadcasted_iota(jnp.int32, shape, 0) & (SUBLANES - 1)
    m = jnp.where(is_ctx, SEQ // CHUNK - 1, DEC_SEQ // CHUNK - 1)
    cm = c & m
    return cm == 0, cm == m


def _fill_time_halo(pad_ref, halo, is_ctx):
    n = halo * SUBLANES
    for lo in range(0, pad_ref.shape[1], HALO_LW):
        lanes = slice(lo, lo + HALO_LW)
        first, last = _chunk_edge_masks(is_ctx, (n, HALO_LW))
        tail = pad_ref[TM:TM + n, lanes]
        head = pad_ref[n:2 * n, lanes]
        pad_ref[0:n, lanes] = jnp.where(first, 0.0, pltpu.roll(tail, 1, 0))
        pad_ref[n + TM:2 * n + TM, lanes] = jnp.where(last, 0.0, pltpu.roll(head, n - 1, 0))


def _expand_rows(dst_ref, row0, rows):
    for k in range(rows.shape[0]):
        lo = row0 + k * SUBLANES
        dst_ref[lo:lo + SUBLANES, :] = jnp.broadcast_to(rows[k:k + 1, :], (SUBLANES, rows.shape[1]))


def _conv_block(pad_ref, row0, lanes, wexp_ref, w0, n_taps):
    def wrow(k):
        return wexp_ref[w0 + k * SUBLANES:w0 + (k + 1) * SUBLANES, lanes]

    groups = CONV_RB // SUBLANES
    acc = jnp.broadcast_to(wrow(n_taps), (groups, SUBLANES, LANES))
    for k in range(n_taps):
        x = pad_ref[pl.ds(row0 + k * SUBLANES, CONV_RB), lanes]
        acc = acc + x.reshape(groups, SUBLANES, LANES) * wrow(k)
    return acc.reshape(CONV_RB, LANES)


def _dwconv(pad_ref, halo, first_offset, n_taps, wexp_ref, emit, post=None):
    def body(b, carry):
        base = pl.multiple_of(b * CONV_RB, CONV_RB)
        for lb in range(pad_ref.shape[1] // LANES):
            lanes = slice(lb * LANES, (lb + 1) * LANES)
            emit(base, lb, _conv_block(pad_ref, base + (halo + first_offset) * SUBLANES, lanes,
                                       wexp_ref, 0, n_taps))
        if post is not None:
            post(base)
        return carry

    lax.fori_loop(0, TM // CONV_RB, body, 0)


def _prep_body(xp_ref, xs_ref, pos_ref, o_ref):
    i = pl.program_id(0)

    @pl.when(i < CTX_TILES)
    def _():
        o_ref[...] = xp_ref[...]

    @pl.when(i >= CTX_TILES)
    def _():
        o_ref[...] = xs_ref[...] + pos_ref[...]


def _prep(xp, xs, pos):
    return pl.pallas_call(
        _prep_body,
        grid=(N_TILES,),
        in_specs=[pl.BlockSpec((TM, D_MODEL), lambda i: (jnp.minimum(i, CTX_TILES - 1), 0)),
                  pl.BlockSpec((TM, D_MODEL), lambda i: (jnp.maximum(i - CTX_TILES, 0), 0)),
                  pl.BlockSpec((DEC_SEQ, D_MODEL), lambda i: (0, 0))],
        out_specs=pl.BlockSpec((TM, D_MODEL), lambda i: (i, 0)),
        out_shape=jax.ShapeDtypeStruct((N_TOK, D_MODEL), F32),
        compiler_params=_params(1),
        name="prep")(xp, xs, pos)


MOD_TN = 1536


def _mod_body(cond_ref, w_ref, b_ref, o_ref):
    c = cond_ref[...]
    s = (c * _sigmoid(c)).astype(BF16)
    o_ref[0] = jnp.dot(s, w_ref[0].astype(BF16), preferred_element_type=F32) + b_ref[0]


def _mod(cond, w_mod, b_mod):
    n_mod = 6 * D_MODEL
    return pl.pallas_call(
        _mod_body,
        grid=(DEPTH, n_mod // MOD_TN),
        in_specs=[pl.BlockSpec((MOD_ROWS, D_MODEL), lambda l, j: (0, 0)),
                  pl.BlockSpec((1, D_MODEL, MOD_TN), lambda l, j: (l, 0, j)),
                  pl.BlockSpec((1, 1, MOD_TN), lambda l, j: (l, 0, j))],
        out_specs=pl.BlockSpec((1, MOD_ROWS, MOD_TN), lambda l, j: (l, 0, j)),
        out_shape=jax.ShapeDtypeStruct((DEPTH, MOD_ROWS, n_mod), F32),
        compiler_params=_params(2),
        name="mod")(cond, w_mod, b_mod.reshape(DEPTH, 1, n_mod))


IN_TN = 1024
NORM_RB = 64


def _modulated_norm_to(h_ref, x_ref, g, scale, shift):
    def body(rb, carry):
        rows = pl.ds(pl.multiple_of(rb * NORM_RB, NORM_RB), NORM_RB)
        h = _rms_rows(x_ref[rows, :], g) * (1.0 + scale) + shift
        h_ref[rows, :] = h.astype(h_ref.dtype)
        return carry
    lax.fori_loop(0, x_ref.shape[0] // NORM_RB, body, 0, unroll=2)


def _inproj_body(l_ref, x_ref, mod_ref, gn_ref, w_ref, b_ref, z_ref, h_ref):
    i = pl.program_id(0)
    j = pl.program_id(1)

    @pl.when(j == 0)
    def _():
        m = mod_ref[0, _mod_row_of_tile(i, TM)]
        _modulated_norm_to(h_ref, x_ref, gn_ref[0, 0:1, :],
                           m[:, D_MODEL:2 * D_MODEL], m[:, 0:D_MODEL])

    acc = jnp.dot(h_ref[...], w_ref[0].astype(BF16), preferred_element_type=F32)
    z_ref[...] = (acc + b_ref[0]).astype(z_ref.dtype)


def _inproj(l, x, mod4, g_norm, w_in, b_in3):
    return _layer_call(
        _inproj_body,
        grid=(N_TILES, N_IN // IN_TN),
        in_specs=[pl.BlockSpec((TM, D_MODEL), lambda i, j, l: (i, 0)),
                  pl.BlockSpec((1, MOD_ROWS, 1, 6 * D_MODEL), lambda i, j, l: (l[0], 0, 0, 0)),
                  pl.BlockSpec((1, 4, D_MODEL), lambda i, j, l: (l[0], 0, 0)),
                  pl.BlockSpec((1, D_MODEL, IN_TN), lambda i, j, l: (l[0], 0, j)),
                  pl.BlockSpec((1, 1, IN_TN), lambda i, j, l: (l[0], 0, j))],
        out_specs=pl.BlockSpec((TM, IN_TN), lambda i, j, l: (i, j)),
        out_shape=jax.ShapeDtypeStruct((N_TOK, N_IN), BF16),
        scratch_shapes=[pltpu.VMEM((TM, D_MODEL), BF16)],
        name="inproj")(l, x, mod4, g_norm, w_in, b_in3)


RNN_HALO = 2
GATE_RB = 256
FILL_RB = 64


def _rows_of(value):
    return [value[r:r + 1, :] for r in range(SUBLANES)]


def _chain_chunks(a_end, b_end, h0_rows, is_ctx, reverse):
    per_ctx_seq = SEQ // CHUNK
    a_rows, b_rows = _rows_of(a_end), _rows_of(b_end)
    h_in, h_out = [None] * SUBLANES, [None] * SUBLANES
    order = range(SUBLANES - 1, -1, -1) if reverse else range(SUBLANES)
    prev = None
    for c in order:
        starts_ctx_seq = (c % per_ctx_seq == per_ctx_seq - 1) if reverse else (c % per_ctx_seq == 0)
        if prev is None:
            h = jnp.where(is_ctx, h0_rows[c // per_ctx_seq], h0_rows[0])
        elif starts_ctx_seq:
            h = jnp.where(is_ctx, h0_rows[c // per_ctx_seq], prev)
        else:
            h = prev
        h_in[c] = h
        prev = a_rows[c] * h + b_rows[c]
        h_out[c] = prev
    return jnp.concatenate(h_in, axis=0), h_out


def _rnn_body(l_ref, zx_ref, zg_ref, cw_ref, cb_ref, rgw_ref, rgb_ref, lam_ref, h0_ref,
              ya_ref, st_ref, pad_ref, wexp_ref, xc_ref, af_ref, uf_ref, ab_ref, ub_ref):
    i = pl.program_id(0)
    is_ctx = i < CTX_TILES
    halo_rows = RNN_HALO * SUBLANES

    def fill(rb, carry):
        r0 = pl.multiple_of(rb * FILL_RB, FILL_RB)
        pad_ref[pl.ds(r0 + halo_rows, FILL_RB), :] = zx_ref[pl.ds(r0, FILL_RB), :].astype(F32)
        return carry

    lax.fori_loop(0, TM // FILL_RB, fill, 0)
    _fill_time_halo(pad_ref, RNN_HALO, is_ctx)

    _expand_rows(wexp_ref, 0, cw_ref[0])
    _expand_rows(wexp_ref, RNN_CONV * SUBLANES, cb_ref[0])

    def emit_xc(row0, lb, block):
        xc_ref[pl.ds(row0, CONV_RB), lb * LANES:(lb + 1) * LANES] = block

    _dwconv(pad_ref, RNN_HALO, -2, RNN_CONV, wexp_ref, emit_xc)

    neg_lam = -lam_ref[0]
    softplus = jnp.maximum(neg_lam, 0.0) + jnp.log1p(jnp.exp(-jnp.abs(neg_lam)))
    coef = (-0.5 * C_RG * np.log2(np.e)) * softplus
    a_refs = (af_ref, ab_ref)
    u_refs = (uf_ref, ub_ref)
    n_col = D_RNN // MXU_DIM

    def gate_body(rb, carry):
        rows = pl.ds(pl.multiple_of(rb * GATE_RB, GATE_RB), GATE_RB)
        for j in range(n_col):
            lanes = slice(j * MXU_DIM, (j + 1) * MXU_DIM)
            xc = xc_ref[rows, lanes]
            xcb = xc.astype(BF16)
            half_xc = 0.5 * xc
            for d in range(2):
                base = d * 2 * n_col
                g_r = jnp.dot(xcb, rgw_ref[0, base + j], preferred_element_type=F32)
                g_i = jnp.dot(xcb, rgw_ref[0, base + n_col + j], preferred_element_type=F32)
                t_r = jnp.tanh(g_r + rgb_ref[0, 2 * d:2 * d + 1, lanes])
                t_i = jnp.tanh(g_i + rgb_ref[0, 2 * d + 1:2 * d + 2, lanes])
                cf = coef[d:d + 1, lanes]
                a = jnp.exp2(cf * t_r + cf)
                y = 1.0 - a * a
                root = jnp.where(y == 0.0, 0.0, y * lax.rsqrt(y))
                a_refs[d][rows, lanes] = a
                u_refs[d][rows, lanes] = root * ((t_i + 1.0) * half_xc)
        return carry

    lax.fori_loop(0, TM // GATE_RB, gate_body, 0)

    def group(t):
        return pl.ds(pl.multiple_of(t * SUBLANES, SUBLANES), SUBLANES)

    def totals(t, carry):
        a_f, b_f, a_b, b_b = carry
        rf, rb = group(t), group(STEPS - 1 - t)
        a = af_ref[rf, :]
        b_f = a * b_f + uf_ref[rf, :]
        a_f = a * a_f
        a = ab_ref[rb, :]
        b_b = a * b_b + ub_ref[rb, :]
        a_b = a * a_b
        return a_f, b_f, a_b, b_b

    ones = jnp.ones((SUBLANES, D_RNN), F32)
    zeros = jnp.zeros((SUBLANES, D_RNN), F32)
    a_f, b_f, a_b, b_b = lax.fori_loop(0, STEPS, totals, (ones, zeros, ones, zeros))

    n_seq = TM // SEQ
    h0_f = [h0_ref[0, 0, 2 * s:2 * s + 1, :] for s in range(n_seq)]
    h0_b = [h0_ref[0, 0, 2 * s + 1:2 * s + 2, :] for s in range(n_seq)]
    hin_f, hout_f = _chain_chunks(a_f, b_f, h0_f, is_ctx, False)
    hin_b, hout_b = _chain_chunks(a_b, b_b, h0_b, is_ctx, True)

    per_ctx_seq = SEQ // CHUNK
    none = jnp.zeros((1, D_RNN), F32)
    for s in range(n_seq):
        last_f = hout_f[s * per_ctx_seq + per_ctx_seq - 1]
        first_b = hout_b[s * per_ctx_seq]
        st_ref[0, 0, 2 * s:2 * s + 1, :] = jnp.where(is_ctx, last_f, hout_f[SUBLANES - 1] if s == 0 else none)
        st_ref[0, 0, 2 * s + 1:2 * s + 2, :] = jnp.where(is_ctx, first_b, hout_b[0] if s == 0 else none)

    def backward(t, h):
        rb = group(STEPS - 1 - t)
        h = ab_ref[rb, :] * h + ub_ref[rb, :]
        ub_ref[rb, :] = h
        return h

    lax.fori_loop(0, STEPS, backward, hin_b)

    def forward(tt, h):
        r0, r1 = group(2 * tt), group(2 * tt + 1)
        h0 = af_ref[r0, :] * h + uf_ref[r0, :]
        h1 = af_ref[r1, :] * h0 + uf_ref[r1, :]
        rows = pl.ds(pl.multiple_of(tt * 2 * SUBLANES, 2 * SUBLANES), 2 * SUBLANES)
        both = jnp.concatenate([h0, h1], axis=0) + ub_ref[rows, :]
        ya_ref[rows, :] = (both * jax.nn.gelu(zg_ref[rows, :].astype(F32))).astype(ya_ref.dtype)
        return h1

    lax.fori_loop(0, STEPS // 2, forward, hin_f)


def _rnn(l, z, rnn_conv_w, rnn_conv_b3, rgw, rg_b4, rg_lambda, h0_all):
    n_rgw = 2 * 2 * (D_RNN // MXU_DIM)
    return _layer_call(
        _rnn_body,
        grid=(N_TILES,),
        in_specs=[pl.BlockSpec((TM, D_RNN), lambda i, l: (i, 0)),
                  pl.BlockSpec((TM, D_RNN), lambda i, l: (i, 1)),
                  pl.BlockSpec((1, RNN_CONV, D_RNN), lambda i, l: (l[0], 0, 0)),
                  pl.BlockSpec((1, 1, D_RNN), lambda i, l: (l[0], 0, 0)),
                  pl.BlockSpec((1, n_rgw, MXU_DIM, MXU_DIM), lambda i, l: (l[0], 0, 0, 0)),
                  pl.BlockSpec((1, 4, D_RNN), lambda i, l: (l[0], 0, 0)),
                  pl.BlockSpec((1, 2, D_RNN), lambda i, l: (l[0], 0, 0)),
                  pl.BlockSpec((1, 1, SUBLANES, D_RNN), lambda i, l: (l[0], i, 0, 0))],
        out_specs=[pl.BlockSpec((TM, D_MODEL), lambda i, l: (i, 0)),
                   pl.BlockSpec((1, 1, SUBLANES, D_RNN), lambda i, l: (0, i, 0, 0))],
        out_shape=[jax.ShapeDtypeStruct((N_TOK, D_RNN), BF16),
                   jax.ShapeDtypeStruct((1, N_TILES, SUBLANES, D_RNN), F32)],
        scratch_shapes=[pltpu.VMEM((TM + 2 * RNN_HALO * SUBLANES, D_RNN), F32),
                        pltpu.VMEM(((RNN_CONV + 1) * SUBLANES, D_RNN), F32),
                        pltpu.VMEM((TM, D_RNN), F32),
                        pltpu.VMEM((TM, D_RNN), F32),
                        pltpu.VMEM((TM, D_RNN), F32),
                        pltpu.VMEM((TM, D_RNN), F32),
                        pltpu.VMEM((TM, D_RNN), F32)],
        name="rnn")(l, z, z, rnn_conv_w, rnn_conv_b3, rgw, rg_b4, rg_lambda, h0_all)


ACT_RB = 64
N_CHUNKS = TM // CHUNK


def _sgu_body(l_ref, z_ref, g_ref, w_ref, bt_ref, yb_ref, su_ref, sv_ref):
    gain = g_ref[0]

    def act_body(rb, carry):
        rows = pl.ds(pl.multiple_of(rb * ACT_RB, ACT_RB), ACT_RB)
        uv = jax.nn.gelu(z_ref[rows, :].astype(F32))
        su_ref[rows, :] = uv[:, :D_SG]
        sv = uv[:, D_SG:]
        mu = jnp.mean(sv, axis=-1, keepdims=True)
        svc = sv - mu
        y = svc * lax.rsqrt(jnp.mean(svc * svc, axis=-1, keepdims=True) + EPS) * gain
        sv_ref[rows, :] = y.astype(sv_ref.dtype)
        return carry

    lax.fori_loop(0, TM // ACT_RB, act_body, 0)

    for g in range(SG_GROUPS):
        lanes = slice(g * SG_GD, (g + 1) * SG_GD)
        rhs = jnp.concatenate([sv_ref[n * CHUNK:(n + 1) * CHUNK, lanes] for n in range(N_CHUNKS)],
                              axis=1)
        mixed = jnp.dot(w_ref[0, g].astype(BF16), rhs, preferred_element_type=F32)
        bias = bt_ref[0, :, g:g + 1]
        for n in range(N_CHUNKS):
            rows = slice(n * CHUNK, (n + 1) * CHUNK)
            m = mixed[:, n * SG_GD:(n + 1) * SG_GD] + bias
            yb_ref[rows, lanes] = (su_ref[rows, lanes] * m).astype(yb_ref.dtype)


def _sgu(l, z_sg, sg_norm_g3, sg_w, sg_bt):
    return _layer_call(
        _sgu_body,
        grid=(N_TILES,),
        in_specs=[pl.BlockSpec((TM, 2 * D_SG), lambda i, l: (i, 0)),
                  pl.BlockSpec((1, 1, D_SG), lambda i, l: (l[0], 0, 0)),
                  pl.BlockSpec((1, SG_GROUPS, CHUNK, CHUNK), lambda i, l: (l[0], 0, 0, 0)),
                  pl.BlockSpec((1, CHUNK, SG_GROUPS), lambda i, l: (l[0], 0, 0))],
        out_specs=pl.BlockSpec((TM, D_SG), lambda i, l: (i, 0)),
        out_shape=jax.ShapeDtypeStruct((N_TOK, D_SG), BF16),
        scratch_shapes=[pltpu.VMEM((TM, D_SG), F32), pltpu.VMEM((TM, D_SG), BF16)],
        name="sgu")(l, z_sg, sg_norm_g3, sg_w, sg_bt)


CF_HALO = CF_CONV // 2


def _conformer_body(l_ref, z_ref, cw_ref, cb_ref, g_ref, b_ref, yc_ref, pad_ref, wexp_ref, blk_ref):
    i = pl.program_id(0)
    halo_rows = CF_HALO * SUBLANES

    def glu(rb, carry):
        r0 = pl.multiple_of(rb * FILL_RB, FILL_RB)
        zz = z_ref[pl.ds(r0, FILL_RB), :].astype(F32)
        pad_ref[pl.ds(r0 + halo_rows, FILL_RB), :] = zz[:, :D_CF] * _sigmoid(zz[:, D_CF:])
        return carry

    lax.fori_loop(0, TM // FILL_RB, glu, 0)
    _fill_time_halo(pad_ref, CF_HALO, i < CTX_TILES)

    _expand_rows(wexp_ref, 0, cw_ref[0])
    _expand_rows(wexp_ref, CF_CONV * SUBLANES, cb_ref[0])

    def emit(row0, lb, block):
        blk_ref[:, lb * LANES:(lb + 1) * LANES] = block

    def post(row0):
        y = blk_ref[...]
        mu = jnp.mean(y, axis=-1, keepdims=True)
        yc = y - mu
        ln = yc * lax.rsqrt(jnp.mean(yc * yc, axis=-1, keepdims=True) + EPS) * g_ref[0] + b_ref[0]
        yc_ref[pl.ds(row0, CONV_RB), :] = (ln * _sigmoid(ln)).astype(yc_ref.dtype)

    _dwconv(pad_ref, CF_HALO, -CF_HALO, CF_CONV, wexp_ref, emit, post)


def _conformer(l, z, cf_conv_w, cf_conv_b3, cf_ln_g3, cf_ln_b3):
    return _layer_call(
        _conformer_body,
        grid=(N_TILES,),
        in_specs=[pl.BlockSpec((TM, 2 * D_CF), lambda i, l: (i, 2)),
                  pl.BlockSpec((1, CF_CONV, D_CF), lambda i, l: (l[0], 0, 0)),
                  pl.BlockSpec((1, 1, D_CF), lambda i, l: (l[0], 0, 0)),
                  pl.BlockSpec((1, 1, D_CF), lambda i, l: (l[0], 0, 0)),
                  pl.BlockSpec((1, 1, D_CF), lambda i, l: (l[0], 0, 0))],
        out_specs=pl.BlockSpec((TM, D_CF), lambda i, l: (i, 0)),
        out_shape=jax.ShapeDtypeStruct((N_TOK, D_CF), BF16),
        scratch_shapes=[pltpu.VMEM((TM + 2 * CF_HALO * SUBLANES, D_CF), F32),
                        pltpu.VMEM(((CF_CONV + 1) * SUBLANES, D_CF), F32),
                        pltpu.VMEM((CONV_RB, D_CF), F32)],
        name="conformer")(l, z, cf_conv_w, cf_conv_b3, cf_ln_g3, cf_ln_b3)


def _merge_body(l_ref, x_ref, ya_ref, yb_ref, yc_ref, zg_ref, mod_ref, gn_ref, wb_ref, wo_ref,
                o_ref):
    i = pl.program_id(0)
    merged = None
    for p, y_ref in enumerate((ya_ref, yb_ref, yc_ref)):
        gate = _sigmoid(zg_ref[:, p * D_MODEL:(p + 1) * D_MODEL].astype(F32))
        term = gate * jnp.dot(y_ref[...], wb_ref[0, p], preferred_element_type=F32)
        merged = term if merged is None else merged + term
    out = jnp.dot(merged.astype(BF16), wo_ref[0], preferred_element_type=F32)
    m = mod_ref[0, _mod_row_of_tile(i, TM_MERGE)]
    gate1 = m[:, 2 * D_MODEL:3 * D_MODEL]
    o_ref[...] = x_ref[...] + gate1 * _rms_rows(out, gn_ref[0, 1:2, :])


def _merge(l, x, ya, yb, yc, z, mod4, g_norm, wb_bf16, wo_bf16):
    tile = lambda i, l: (i, 0)
    return _layer_call(
        _merge_body,
        grid=(N_TOK // TM_MERGE,),
        in_specs=[pl.BlockSpec((TM_MERGE, D_MODEL), tile),
                  pl.BlockSpec((TM_MERGE, D_MODEL), tile),
                  pl.BlockSpec((TM_MERGE, D_MODEL), tile),
                  pl.BlockSpec((TM_MERGE, D_MODEL), tile),
                  pl.BlockSpec((TM_MERGE, 3 * D_MODEL), lambda i, l: (i, 2)),
                  pl.BlockSpec((1, MOD_ROWS, 1, 6 * D_MODEL), lambda i, l: (l[0], 0, 0, 0)),
                  pl.BlockSpec((1, 4, D_MODEL), lambda i, l: (l[0], 0, 0)),
                  pl.BlockSpec((1, 3, D_MODEL, D_MODEL), lambda i, l: (l[0], 0, 0, 0)),
                  pl.BlockSpec((1, D_MODEL, D_MODEL), lambda i, l: (l[0], 0, 0))],
        out_specs=pl.BlockSpec((TM_MERGE, D_MODEL), tile),
        out_shape=jax.ShapeDtypeStruct((N_TOK, D_MODEL), F32),
        scratch_shapes=[],
        name="merge")(l, x, ya, yb, yc, z, mod4, g_norm, wb_bf16, wo_bf16)


FFN_HALO = 1
N_FF_STEPS = D_FF // FF_CW
FFN_BLOCKS = TM // CONV_RB
FFN_SPLIT = (TM // 2 - FFN_HALO * SUBLANES) // CONV_RB


def _ffn_body(l_ref, x_ref, mod_ref, gn_ref, wg_ref, wv_ref, cwg_ref, cwv_ref, cbg_ref, cbv_ref,
              wd_ref, o_ref, h_ref, acc_ref, padg_ref, padv_ref, act_ref, wgb_ref, wvb_ref, wdb_ref,
              wexp_ref):
    i = pl.program_id(0)
    c = pl.program_id(1)
    m = mod_ref[0, _mod_row_of_tile(i, TM)]
    halo_rows = FFN_HALO * SUBLANES

    @pl.when(c == 0)
    def _():
        _modulated_norm_to(h_ref, x_ref, gn_ref[0, 2:3, :],
                           m[:, 4 * D_MODEL:5 * D_MODEL], m[:, 3 * D_MODEL:4 * D_MODEL])
        acc_ref[...] = jnp.zeros_like(acc_ref)

    wgb_ref[...] = wg_ref[0].astype(BF16)
    wvb_ref[...] = wv_ref[0].astype(BF16)
    wdb_ref[...] = wd_ref[0].astype(BF16)

    def up(r0, r1):
        h = h_ref[r0:r1, :]
        padg_ref[halo_rows + r0:halo_rows + r1, :] = jnp.dot(h, wgb_ref[...], preferred_element_type=F32)
        padv_ref[halo_rows + r0:halo_rows + r1, :] = jnp.dot(h, wvb_ref[...], preferred_element_type=F32)

    n_wrows = (FFN_CONV + 1) * SUBLANES
    for s, (cw_ref, cb_ref) in enumerate(((cwg_ref, cbg_ref), (cwv_ref, cbv_ref))):
        _expand_rows(wexp_ref, s * n_wrows, cw_ref[0])
        _expand_rows(wexp_ref, s * n_wrows + FFN_CONV * SUBLANES, cb_ref[0])

    def conv_act(rb):
        base = rb * CONV_RB
        for lb in range(FF_CW // LANES):
            lanes = slice(lb * LANES, (lb + 1) * LANES)
            outs = []
            for s, pad_ref in enumerate((padg_ref, padv_ref)):
                outs.append(_conv_block(pad_ref, base, lanes, wexp_ref, s * n_wrows, FFN_CONV))
            act_ref[base:base + CONV_RB, lanes] = (jax.nn.gelu(outs[0]) * outs[1]).astype(act_ref.dtype)

    def down(r0, r1):
        acc_ref[r0:r1, :] += jnp.dot(act_ref[r0:r1, :], wdb_ref[...], preferred_element_type=F32)

    up(0, TM // 2)
    up(TM // 2, TM)
    is_ctx = i < CTX_TILES
    _fill_time_halo(padg_ref, FFN_HALO, is_ctx)
    _fill_time_halo(padv_ref, FFN_HALO, is_ctx)
    for rb in range(1, FFN_SPLIT):
        conv_act(rb)
    conv_act(0)
    down(0, FFN_SPLIT * CONV_RB)
    for rb in range(FFN_SPLIT, FFN_BLOCKS):
        conv_act(rb)
    down(FFN_SPLIT * CONV_RB, TM)

    @pl.when(c == N_FF_STEPS - 1)
    def _():
        gate2 = m[:, 5 * D_MODEL:6 * D_MODEL]
        g3 = gn_ref[0, 3:4, :]

        def body(rb, carry):
            rows = pl.ds(pl.multiple_of(rb * NORM_RB, NORM_RB), NORM_RB)
            o_ref[rows, :] = x_ref[rows, :] + gate2 * _rms_rows(acc_ref[rows, :], g3)
            return carry
        lax.fori_loop(0, TM // NORM_RB, body, 0, unroll=2)


def _ffn(l, x, mod4, g_norm, ffn_up, ffn_conv_w, ffn_conv_b3, ffn_down):
    return _layer_call(
        _ffn_body,
        grid=(N_TILES, N_FF_STEPS),
        in_specs=[pl.BlockSpec((TM, D_MODEL), lambda i, c, l: (i, 0)),
                  pl.BlockSpec((1, MOD_ROWS, 1, 6 * D_MODEL), lambda i, c, l: (l[0], 0, 0, 0)),
                  pl.BlockSpec((1, 4, D_MODEL), lambda i, c, l: (l[0], 0, 0)),
                  pl.BlockSpec((1, D_MODEL, FF_CW), lambda i, c, l: (l[0], 0, c)),
                  pl.BlockSpec((1, D_MODEL, FF_CW), lambda i, c, l: (l[0], 0, N_FF_STEPS + c)),
                  pl.BlockSpec((1, FFN_CONV, FF_CW), lambda i, c, l: (l[0], 0, c)),
                  pl.BlockSpec((1, FFN_CONV, FF_CW), lambda i, c, l: (l[0], 0, N_FF_STEPS + c)),
                  pl.BlockSpec((1, 1, FF_CW), lambda i, c, l: (l[0], 0, c)),
                  pl.BlockSpec((1, 1, FF_CW), lambda i, c, l: (l[0], 0, N_FF_STEPS + c)),
                  pl.BlockSpec((1, FF_CW, D_MODEL), lambda i, c, l: (l[0], c, 0))],
        out_specs=pl.BlockSpec((TM, D_MODEL), lambda i, c, l: (i, 0)),
        out_shape=jax.ShapeDtypeStruct((N_TOK, D_MODEL), F32),
        scratch_shapes=[pltpu.VMEM((TM, D_MODEL), BF16),
                        pltpu.VMEM((TM, D_MODEL), F32),
                        pltpu.VMEM((TM + 2 * FFN_HALO * SUBLANES, FF_CW), F32),
                        pltpu.VMEM((TM + 2 * FFN_HALO * SUBLANES, FF_CW), F32),
                        pltpu.VMEM((TM, FF_CW), BF16),
                        pltpu.VMEM((D_MODEL, FF_CW), BF16),
                        pltpu.VMEM((D_MODEL, FF_CW), BF16),
                        pltpu.VMEM((FF_CW, D_MODEL), BF16),
                        pltpu.VMEM((2 * (FFN_CONV + 1) * SUBLANES, FF_CW), F32)],
        name="ffn")(l, x, mod4, g_norm, ffn_up, ffn_up, ffn_conv_w, ffn_conv_w,
                    ffn_conv_b3, ffn_conv_b3, ffn_down)


def _pos_table():
    t = jnp.arange(DEC_SEQ)
    r = (t // GRID_W).astype(F32)
    col = (t % GRID_W).astype(F32)
    q = D_MODEL // 4
    omega = 1.0 / (POS_BASE ** (jnp.arange(q, dtype=F32) / q))

    def emb(p):
        ang = p[:, None] * omega[None, :]
        return jnp.concatenate([jnp.sin(ang), jnp.cos(ang)], axis=-1)
    return jnp.concatenate([emb(r), emb(col)], axis=-1).astype(F32)


def _block_diag_gates(rg_w):
    per_tile = MXU_DIM // BS_RNN
    n_col = H_RNN // per_tile
    w = rg_w.reshape(DEPTH, 2, 2, n_col, per_tile, BS_RNN, BS_RNN)
    eye = jnp.eye(per_tile, dtype=rg_w.dtype)
    tiles = jnp.einsum('ldkcaij,ab->ldkcaibj', w, eye)
    return tiles.reshape(DEPTH, 2 * 2 * n_col, MXU_DIM, MXU_DIM).astype(BF16)


def kernel(x_prompt, x_sample, state_rglru, c, c_ctx, w_mod, b_mod, g_norm, w_in, b_in,
           rnn_conv_w, rnn_conv_b, rg_w, rg_b, rg_lambda, sg_norm_g, sg_w, sg_b,
           cf_conv_w, cf_conv_b, cf_ln_g, cf_ln_b, w_branch, w_out, ffn_up, ffn_conv_w,
           ffn_conv_b, ffn_down):
    x = _interleave(_prep(x_prompt.reshape(BATCH * SEQ, D_MODEL),
                          x_sample.reshape(DEC_BATCH * DEC_SEQ, D_MODEL), _pos_table()))

    cond = jnp.zeros((MOD_ROWS, D_MODEL), F32).at[:DEC_BATCH].set(c).at[CTX_MOD_ROW].set(c_ctx)
    mod4 = _mod(cond, w_mod, b_mod).reshape(DEPTH, MOD_ROWS, 1, 6 * D_MODEL)

    h0_all = jnp.zeros((DEPTH, N_TILES, SUBLANES, D_RNN), F32)
    h0_all = h0_all.at[:, CTX_TILES:, 0:2, :].set(jnp.transpose(state_rglru.astype(F32), (1, 0, 2, 3)))

    rgw = _block_diag_gates(0.5 * rg_w)
    wb_bf16 = w_branch.astype(BF16)
    wo_bf16 = w_out.astype(BF16)
    b_in3 = b_in.reshape(DEPTH, 1, N_IN)
    rnn_conv_b3 = rnn_conv_b.reshape(DEPTH, 1, D_RNN)
    rg_b4 = 0.5 * rg_b.reshape(DEPTH, 4, D_RNN)
    sg_norm_g3 = sg_norm_g.reshape(DEPTH, 1, D_SG)
    sg_bt = jnp.transpose(sg_b, (0, 2, 1))
    cf_conv_b3 = cf_conv_b.reshape(DEPTH, 1, D_CF)
    cf_ln_g3 = cf_ln_g.reshape(DEPTH, 1, D_CF)
    cf_ln_b3 = cf_ln_b.reshape(DEPTH, 1, D_CF)
    ffn_conv_b3 = ffn_conv_b.reshape(DEPTH, 1, 2 * D_FF)

    states = []
    for layer in range(DEPTH):
        l = jnp.full((1,), layer, jnp.int32)
        z = _inproj(l, x, mod4, g_norm, w_in, b_in3)
        ya, st = _rnn(l, z, rnn_conv_w, rnn_conv_b3, rgw, rg_b4, rg_lambda, h0_all)
        z_sg = _deinterleave(z[:, 2 * D_RNN:2 * D_RNN + 2 * D_SG])
        yb = _interleave(_sgu(l, z_sg, sg_norm_g3, sg_w, sg_bt))
        yc = _conformer(l, z, cf_conv_w, cf_conv_b3, cf_ln_g3, cf_ln_b3)
        x = _merge(l, x, ya, yb, yc, z, mod4, g_norm, wb_bf16, wo_bf16)
        x = _ffn(l, x, mod4, g_norm, ffn_up, ffn_conv_w, ffn_conv_b3, ffn_down)
        states.append(st[0, :CTX_TILES])
    states = jnp.stack(states)

    x = _deinterleave(x)
    y_prompt = x[:BATCH * SEQ].reshape(BATCH, SEQ, D_MODEL)
    y_sample = x[BATCH * SEQ:].reshape(DEC_BATCH, DEC_SEQ, D_MODEL)
    new_state = states.reshape(DEPTH, BATCH, 2, D_RNN).transpose(1, 0, 2, 3)
    return (y_prompt, y_sample, new_state)
```

```python
import functools

import jax
import jax.numpy as jnp
import numpy as np
from jax import lax
from jax.experimental import pallas as pl
from jax.experimental.pallas import tpu as pltpu

F32 = jnp.float32
BF16 = jnp.bfloat16

D_MODEL = 1024
BATCH = 16
SEQ = 256
DEPTH = 4
DEC_BATCH = 4
DEC_SEQ = 1024
GRID_W = 64
D_RNN = 1024
H_RNN = 16
BS_RNN = D_RNN // H_RNN
RNN_CONV = 4
C_RG = 8.0
D_SG = 1024
SG_GROUPS = 8
SG_GD = D_SG // SG_GROUPS
CHUNK = 128
D_CF = 1024
CF_CONV = 31
D_FF = 4096
FFN_CONV = 3
N_IN = 2 * D_RNN + 2 * D_SG + 2 * D_CF + 3 * D_MODEL
EPS = 1e-6
POS_BASE = 10000.0

LANES = 128
SUBLANES = 8
MXU_DIM = 256

N_TOK = BATCH * SEQ + DEC_BATCH * DEC_SEQ
TM = 1024
N_TILES = N_TOK // TM
CTX_TILES = BATCH * SEQ // TM
MOD_ROWS = 8
CTX_MOD_ROW = DEC_BATCH
CONV_RB = 64
FF_CW = 512
TM_MERGE = 512
VMEM_LIMIT = 56 * 1024 * 1024


def _sigmoid(x):
    return 0.5 * (jnp.tanh(0.5 * x) + 1.0)


def _rms_rows(x, g):
    return x * lax.rsqrt(jnp.mean(x * x, axis=-1, keepdims=True) + EPS) * g


def _mod_row_of_tile(i, tile_rows):
    tiles_per_latent = DEC_SEQ // tile_rows
    ctx_tiles = BATCH * SEQ // tile_rows
    lat = jnp.maximum(i - ctx_tiles, 0) // tiles_per_latent
    return jnp.where(i < ctx_tiles, CTX_MOD_ROW, lat)


def _params(n_grid):
    return pltpu.CompilerParams(dimension_semantics=("arbitrary",) * n_grid,
                                vmem_limit_bytes=VMEM_LIMIT)


def _layer_call(body, grid, in_specs, out_specs, out_shape, scratch_shapes, name):
    return pl.pallas_call(
        body,
        grid_spec=pltpu.PrefetchScalarGridSpec(
            num_scalar_prefetch=1, grid=grid, in_specs=in_specs, out_specs=out_specs,
            scratch_shapes=scratch_shapes),
        out_shape=out_shape,
        compiler_params=_params(len(grid)),
        name=name)


STEPS = CHUNK
HALO_LW = 256


def _interleave(x):
    return x.reshape(N_TILES, SUBLANES, STEPS, -1).transpose(0, 2, 1, 3).reshape(x.shape)


def _deinterleave(x):
    return x.reshape(N_TILES, STEPS, SUBLANES, -1).transpose(0, 2, 1, 3).reshape(x.shape)


def _chunk_edge_masks(is_ctx, shape):
    c = lax.broadcasted_iota(jnp.int32, shape, 0) & (SUBLANES - 1)
    m = jnp.where(is_ctx, SEQ // CHUNK - 1, DEC_SEQ // CHUNK - 1)
    cm = c & m
    return cm == 0, cm == m


def _fill_time_halo(pad_ref, halo, is_ctx):
    n = halo * SUBLANES
    for lo in range(0, pad_ref.shape[1], HALO_LW):
        lanes = slice(lo, lo + HALO_LW)
        first, last = _chunk_edge_masks(is_ctx, (n, HALO_LW))
        tail = pad_ref[TM:TM + n, lanes]
        head = pad_ref[n:2 * n, lanes]
        pad_ref[0:n, lanes] = jnp.where(first, 0.0, pltpu.roll(tail, 1, 0))
        pad_ref[n + TM:2 * n + TM, lanes] = jnp.where(last, 0.0, pltpu.roll(head, n - 1, 0))


def _expand_rows(dst_ref, row0, rows):
    for k in range(rows.shape[0]):
        lo = row0 + k * SUBLANES
        dst_ref[lo:lo + SUBLANES, :] = jnp.broadcast_to(rows[k:k + 1, :], (SUBLANES, rows.shape[1]))


def _conv_block(pad_ref, row0, lanes, wexp_ref, w0, n_taps):
    def wrow(k):
        return wexp_ref[w0 + k * SUBLANES:w0 + (k + 1) * SUBLANES, lanes]

    groups = CONV_RB // SUBLANES
    acc = jnp.broadcast_to(wrow(n_taps), (groups, SUBLANES, LANES))
    for k in range(n_taps):
        x = pad_ref[pl.ds(row0 + k * SUBLANES, CONV_RB), lanes]
        acc = acc + x.reshape(groups, SUBLANES, LANES) * wrow(k)
    return acc.reshape(CONV_RB, LANES)


def _dwconv(pad_ref, halo, first_offset, n_taps, wexp_ref, emit, post=None):
    def body(b, carry):
        base = pl.multiple_of(b * CONV_RB, CONV_RB)
        for lb in range(pad_ref.shape[1] // LANES):
            lanes = slice(lb * LANES, (lb + 1) * LANES)
            emit(base, lb, _conv_block(pad_ref, base + (halo + first_offset) * SUBLANES, lanes,
                                       wexp_ref, 0, n_taps))
        if post is not None:
            post(base)
        return carry

    lax.fori_loop(0, TM // CONV_RB, body, 0)


def _prep_body(xp_ref, xs_ref, pos_ref, o_ref):
    i = pl.program_id(0)

    @pl.when(i < CTX_TILES)
    def _():
        o_ref[...] = xp_ref[...]

    @pl.when(i >= CTX_TILES)
    def _():
        o_ref[...] = xs_ref[...] + pos_ref[...]


def _prep(xp, xs, pos):
    return pl.pallas_call(
        _prep_body,
        grid=(N_TILES,),
        in_specs=[pl.BlockSpec((TM, D_MODEL), lambda i: (jnp.minimum(i, CTX_TILES - 1), 0)),
                  pl.BlockSpec((TM, D_MODEL), lambda i: (jnp.maximum(i - CTX_TILES, 0), 0)),
                  pl.BlockSpec((DEC_SEQ, D_MODEL), lambda i: (0, 0))],
        out_specs=pl.BlockSpec((TM, D_MODEL), lambda i: (i, 0)),
        out_shape=jax.ShapeDtypeStruct((N_TOK, D_MODEL), F32),
        compiler_params=_params(1),
        name="prep")(xp, xs, pos)


MOD_TN = 1536


def _mod_body(cond_ref, w_ref, b_ref, o_ref):
    c = cond_ref[...]
    s = (c * _sigmoid(c)).astype(BF16)
    o_ref[0] = jnp.dot(s, w_ref[0].astype(BF16), preferred_element_type=F32) + b_ref[0]


def _mod(cond, w_mod, b_mod):
    n_mod = 6 * D_MODEL
    return pl.pallas_call(
        _mod_body,
        grid=(DEPTH, n_mod // MOD_TN),
        in_specs=[pl.BlockSpec((MOD_ROWS, D_MODEL), lambda l, j: (0, 0)),
                  pl.BlockSpec((1, D_MODEL, MOD_TN), lambda l, j: (l, 0, j)),
                  pl.BlockSpec((1, 1, MOD_TN), lambda l, j: (l, 0, j))],
        out_specs=pl.BlockSpec((1, MOD_ROWS, MOD_TN), lambda l, j: (l, 0, j)),
        out_shape=jax.ShapeDtypeStruct((DEPTH, MOD_ROWS, n_mod), F32),
        compiler_params=_params(2),
        name="mod")(cond, w_mod, b_mod.reshape(DEPTH, 1, n_mod))


IN_TN = 1024
NORM_RB = 64


def _modulated_norm_to(h_ref, x_ref, g, scale, shift):
    def body(rb, carry):
        rows = pl.ds(pl.multiple_of(rb * NORM_RB, NORM_RB), NORM_RB)
        h = _rms_rows(x_ref[rows, :], g) * (1.0 + scale) + shift
        h_ref[rows, :] = h.astype(h_ref.dtype)
        return carry
    lax.fori_loop(0, x_ref.shape[0] // NORM_RB, body, 0, unroll=2)


SG_COL = 2 * D_RNN // IN_TN


def _inproj_body(l_ref, x_ref, mod_ref, gn_ref, w_ref, b_ref, z_ref, h_ref, hn_ref, perm_ref):
    i = pl.program_id(0)
    j = pl.program_id(1)

    @pl.when(jnp.logical_and(i == 0, j == 0))
    def _():
        nat = lax.broadcasted_iota(jnp.int32, (TM, TM), 0)
        src = lax.broadcasted_iota(jnp.int32, (TM, TM), 1)
        chunk_shift = CHUNK.bit_length() - 1
        row_of_nat = ((nat & (CHUNK - 1)) * SUBLANES) | jnp.right_shift(nat, chunk_shift)
        perm_ref[...] = jnp.where(src == row_of_nat, 1.0, 0.0).astype(BF16)

    @pl.when(j == 0)
    def _():
        m = mod_ref[0, _mod_row_of_tile(i, TM)]
        _modulated_norm_to(h_ref, x_ref, gn_ref[0, 0:1, :],
                           m[:, D_MODEL:2 * D_MODEL], m[:, 0:D_MODEL])
        hn_ref[...] = jnp.dot(perm_ref[...], h_ref[...], preferred_element_type=F32).astype(BF16)

    natural = jnp.logical_and(j >= SG_COL, j < SG_COL + 2 * D_SG // IN_TN)

    @pl.when(natural)
    def _():
        acc = jnp.dot(hn_ref[...], w_ref[0].astype(BF16), preferred_element_type=F32)
        z_ref[...] = (acc + b_ref[0]).astype(z_ref.dtype)

    @pl.when(jnp.logical_not(natural))
    def _():
        acc = jnp.dot(h_ref[...], w_ref[0].astype(BF16), preferred_element_type=F32)
        z_ref[...] = (acc + b_ref[0]).astype(z_ref.dtype)


def _inproj(l, x, mod4, g_norm, w_in, b_in3):
    return _layer_call(
        _inproj_body,
        grid=(N_TILES, N_IN // IN_TN),
        in_specs=[pl.BlockSpec((TM, D_MODEL), lambda i, j, l: (i, 0)),
                  pl.BlockSpec((1, MOD_ROWS, 1, 6 * D_MODEL), lambda i, j, l: (l[0], 0, 0, 0)),
                  pl.BlockSpec((1, 4, D_MODEL), lambda i, j, l: (l[0], 0, 0)),
                  pl.BlockSpec((1, D_MODEL, IN_TN), lambda i, j, l: (l[0], 0, j)),
                  pl.BlockSpec((1, 1, IN_TN), lambda i, j, l: (l[0], 0, j))],
        out_specs=pl.BlockSpec((TM, IN_TN), lambda i, j, l: (i, j)),
        out_shape=jax.ShapeDtypeStruct((N_TOK, N_IN), BF16),
        scratch_shapes=[pltpu.VMEM((TM, D_MODEL), BF16),
                        pltpu.VMEM((TM, D_MODEL), BF16),
                        pltpu.VMEM((TM, TM), BF16)],
        name="inproj")(l, x, mod4, g_norm, w_in, b_in3)


RNN_HALO = 2
GATE_RB = 256
FILL_RB = 64


def _rows_of(value):
    return [value[r:r + 1, :] for r in range(SUBLANES)]


def _chain_chunks(a_end, b_end, h0_rows, is_ctx, reverse):
    per_ctx_seq = SEQ // CHUNK
    a_rows, b_rows = _rows_of(a_end), _rows_of(b_end)
    h_in, h_out = [None] * SUBLANES, [None] * SUBLANES
    order = range(SUBLANES - 1, -1, -1) if reverse else range(SUBLANES)
    prev = None
    for c in order:
        starts_ctx_seq = (c % per_ctx_seq == per_ctx_seq - 1) if reverse else (c % per_ctx_seq == 0)
        if prev is None:
            h = jnp.where(is_ctx, h0_rows[c // per_ctx_seq], h0_rows[0])
        elif starts_ctx_seq:
            h = jnp.where(is_ctx, h0_rows[c // per_ctx_seq], prev)
        else:
            h = prev
        h_in[c] = h
        prev = a_rows[c] * h + b_rows[c]
        h_out[c] = prev
    return jnp.concatenate(h_in, axis=0), h_out


def _rnn_body(l_ref, zx_ref, zg_ref, cw_ref, cb_ref, rgw_ref, rgb_ref, lam_ref, h0_ref,
              ya_ref, st_ref, pad_ref, wexp_ref, xc_ref, af_ref, uf_ref, ab_ref, ub_ref):
    i = pl.program_id(0)
    is_ctx = i < CTX_TILES
    halo_rows = RNN_HALO * SUBLANES

    def fill(rb, carry):
        r0 = pl.multiple_of(rb * FILL_RB, FILL_RB)
        pad_ref[pl.ds(r0 + halo_rows, FILL_RB), :] = zx_ref[pl.ds(r0, FILL_RB), :].astype(F32)
        return carry

    lax.fori_loop(0, TM // FILL_RB, fill, 0)
    _fill_time_halo(pad_ref, RNN_HALO, is_ctx)

    _expand_rows(wexp_ref, 0, cw_ref[0])
    _expand_rows(wexp_ref, RNN_CONV * SUBLANES, cb_ref[0])

    def emit_xc(row0, lb, block):
        xc_ref[pl.ds(row0, CONV_RB), lb * LANES:(lb + 1) * LANES] = block

    _dwconv(pad_ref, RNN_HALO, -2, RNN_CONV, wexp_ref, emit_xc)

    neg_lam = -lam_ref[0]
    softplus = jnp.maximum(neg_lam, 0.0) + jnp.log1p(jnp.exp(-jnp.abs(neg_lam)))
    coef = (-0.5 * C_RG * np.log2(np.e)) * softplus
    a_refs = (af_ref, ab_ref)
    u_refs = (uf_ref, ub_ref)
    n_col = D_RNN // MXU_DIM

    def gate_body(rb, carry):
        rows = pl.ds(pl.multiple_of(rb * GATE_RB, GATE_RB), GATE_RB)
        for j in range(n_col):
            lanes = slice(j * MXU_DIM, (j + 1) * MXU_DIM)
            xc = xc_ref[rows, lanes]
            xcb = xc.astype(BF16)
            half_xc = 0.5 * xc
            for d in range(2):
                base = d * 2 * n_col
                g_r = jnp.dot(xcb, rgw_ref[0, base + j], preferred_element_type=F32)
                g_i = jnp.dot(xcb, rgw_ref[0, base + n_col + j], preferred_element_type=F32)
                t_r = jnp.tanh(g_r + rgb_ref[0, 2 * d:2 * d + 1, lanes])
                t_i = jnp.tanh(g_i + rgb_ref[0, 2 * d + 1:2 * d + 2, lanes])
                cf = coef[d:d + 1, lanes]
                a = jnp.exp2(cf * t_r + cf)
                y = 1.0 - a * a
                root = jnp.where(y == 0.0, 0.0, y * lax.rsqrt(y))
                a_refs[d][rows, lanes] = a
                u_refs[d][rows, lanes] = root * ((t_i + 1.0) * half_xc)
        return carry

    lax.fori_loop(0, TM // GATE_RB, gate_body, 0)

    def group(t):
        return pl.ds(pl.multiple_of(t * SUBLANES, SUBLANES), SUBLANES)

    def totals(t, carry):
        a_f, b_f, a_b, b_b = carry
        rf, rb = group(t), group(STEPS - 1 - t)
        a = af_ref[rf, :]
        b_f = a * b_f + uf_ref[rf, :]
        a_f = a * a_f
        a = ab_ref[rb, :]
        b_b = a * b_b + ub_ref[rb, :]
        a_b = a * a_b
        return a_f, b_f, a_b, b_b

    ones = jnp.ones((SUBLANES, D_RNN), F32)
    zeros = jnp.zeros((SUBLANES, D_RNN), F32)
    a_f, b_f, a_b, b_b = lax.fori_loop(0, STEPS, totals, (ones, zeros, ones, zeros))

    n_seq = TM // SEQ
    h0_f = [h0_ref[0, 0, 2 * s:2 * s + 1, :] for s in range(n_seq)]
    h0_b = [h0_ref[0, 0, 2 * s + 1:2 * s + 2, :] for s in range(n_seq)]
    hin_f, hout_f = _chain_chunks(a_f, b_f, h0_f, is_ctx, False)
    hin_b, hout_b = _chain_chunks(a_b, b_b, h0_b, is_ctx, True)

    per_ctx_seq = SEQ // CHUNK
    none = jnp.zeros((1, D_RNN), F32)
    for s in range(n_seq):
        last_f = hout_f[s * per_ctx_seq + per_ctx_seq - 1]
        first_b = hout_b[s * per_ctx_seq]
        st_ref[0, 0, 2 * s:2 * s + 1, :] = jnp.where(is_ctx, last_f, hout_f[SUBLANES - 1] if s == 0 else none)
        st_ref[0, 0, 2 * s + 1:2 * s + 2, :] = jnp.where(is_ctx, first_b, hout_b[0] if s == 0 else none)

    def backward(t, h):
        rb = group(STEPS - 1 - t)
        h = ab_ref[rb, :] * h + ub_ref[rb, :]
        ub_ref[rb, :] = h
        return h

    lax.fori_loop(0, STEPS, backward, hin_b, unroll=2)

    def forward(tt, h):
        r0, r1 = group(2 * tt), group(2 * tt + 1)
        h0 = af_ref[r0, :] * h + uf_ref[r0, :]
        h1 = af_ref[r1, :] * h0 + uf_ref[r1, :]
        rows = pl.ds(pl.multiple_of(tt * 2 * SUBLANES, 2 * SUBLANES), 2 * SUBLANES)
        both = jnp.concatenate([h0, h1], axis=0) + ub_ref[rows, :]
        ya_ref[rows, :] = (both * jax.nn.gelu(zg_ref[rows, :].astype(F32))).astype(ya_ref.dtype)
        return h1

    lax.fori_loop(0, STEPS // 2, forward, hin_f)


def _rnn(l, z, rnn_conv_w, rnn_conv_b3, rgw, rg_b4, rg_lambda, h0_all):
    n_rgw = 2 * 2 * (D_RNN // MXU_DIM)
    return _layer_call(
        _rnn_body,
        grid=(N_TILES,),
        in_specs=[pl.BlockSpec((TM, D_RNN), lambda i, l: (i, 0)),
                  pl.BlockSpec((TM, D_RNN), lambda i, l: (i, 1)),
                  pl.BlockSpec((1, RNN_CONV, D_RNN), lambda i, l: (l[0], 0, 0)),
                  pl.BlockSpec((1, 1, D_RNN), lambda i, l: (l[0], 0, 0)),
                  pl.BlockSpec((1, n_rgw, MXU_DIM, MXU_DIM), lambda i, l: (l[0], 0, 0, 0)),
                  pl.BlockSpec((1, 4, D_RNN), lambda i, l: (l[0], 0, 0)),
                  pl.BlockSpec((1, 2, D_RNN), lambda i, l: (l[0], 0, 0)),
                  pl.BlockSpec((1, 1, SUBLANES, D_RNN), lambda i, l: (l[0], i, 0, 0))],
        out_specs=[pl.BlockSpec((TM, D_MODEL), lambda i, l: (i, 0)),
                   pl.BlockSpec((1, 1, SUBLANES, D_RNN), lambda i, l: (0, i, 0, 0))],
        out_shape=[jax.ShapeDtypeStruct((N_TOK, D_RNN), BF16),
                   jax.ShapeDtypeStruct((1, N_TILES, SUBLANES, D_RNN), F32)],
        scratch_shapes=[pltpu.VMEM((TM + 2 * RNN_HALO * SUBLANES, D_RNN), F32),
                        pltpu.VMEM(((RNN_CONV + 1) * SUBLANES, D_RNN), F32),
                        pltpu.VMEM((TM, D_RNN), F32),
                        pltpu.VMEM((TM, D_RNN), F32),
                        pltpu.VMEM((TM, D_RNN), F32),
                        pltpu.VMEM((TM, D_RNN), F32),
                        pltpu.VMEM((TM, D_RNN), F32)],
        name="rnn")(l, z, z, rnn_conv_w, rnn_conv_b3, rgw, rg_b4, rg_lambda, h0_all)


ACT_RB = 64
N_CHUNKS = TM // CHUNK


def _sgu_body(l_ref, z_ref, g_ref, w_ref, bt_ref, yb_ref, su_ref, sv_ref):
    gain = g_ref[0]

    def act_body(rb, carry):
        rows = pl.ds(pl.multiple_of(rb * ACT_RB, ACT_RB), ACT_RB)
        uv = jax.nn.gelu(z_ref[rows, :].astype(F32))
        su_ref[rows, :] = uv[:, :D_SG]
        sv = uv[:, D_SG:]
        mu = jnp.mean(sv, axis=-1, keepdims=True)
        svc = sv - mu
        y = svc * lax.rsqrt(jnp.mean(svc * svc, axis=-1, keepdims=True) + EPS) * gain
        sv_ref[rows, :] = y.astype(sv_ref.dtype)
        return carry

    lax.fori_loop(0, TM // ACT_RB, act_body, 0)

    for g in range(SG_GROUPS):
        lanes = slice(g * SG_GD, (g + 1) * SG_GD)
        rhs = jnp.concatenate([sv_ref[n * CHUNK:(n + 1) * CHUNK, lanes] for n in range(N_CHUNKS)],
                              axis=1)
        mixed = jnp.dot(w_ref[0, g].astype(BF16), rhs, preferred_element_type=F32)
        bias = bt_ref[0, :, g:g + 1]
        for n in range(N_CHUNKS):
            rows = slice(n * CHUNK, (n + 1) * CHUNK)
            m = mixed[:, n * SG_GD:(n + 1) * SG_GD] + bias
            yb_ref[rows, lanes] = (su_ref[rows, lanes] * m).astype(yb_ref.dtype)


def _sgu(l, z, sg_norm_g3, sg_w, sg_bt):
    return _layer_call(
        _sgu_body,
        grid=(N_TILES,),
        in_specs=[pl.BlockSpec((TM, 2 * D_SG), lambda i, l: (i, 1)),
                  pl.BlockSpec((1, 1, D_SG), lambda i, l: (l[0], 0, 0)),
                  pl.BlockSpec((1, SG_GROUPS, CHUNK, CHUNK), lambda i, l: (l[0], 0, 0, 0)),
                  pl.BlockSpec((1, CHUNK, SG_GROUPS), lambda i, l: (l[0], 0, 0))],
        out_specs=pl.BlockSpec((TM, D_SG), lambda i, l: (i, 0)),
        out_shape=jax.ShapeDtypeStruct((N_TOK, D_SG), BF16),
        scratch_shapes=[pltpu.VMEM((TM, D_SG), F32), pltpu.VMEM((TM, D_SG), BF16)],
        name="sgu")(l, z, sg_norm_g3, sg_w, sg_bt)


CF_HALO = CF_CONV // 2


def _conformer_body(l_ref, z_ref, cw_ref, cb_ref, g_ref, b_ref, yc_ref, pad_ref, wexp_ref, blk_ref):
    i = pl.program_id(0)
    halo_rows = CF_HALO * SUBLANES

    def glu(rb, carry):
        r0 = pl.multiple_of(rb * FILL_RB, FILL_RB)
        zz = z_ref[pl.ds(r0, FILL_RB), :].astype(F32)
        pad_ref[pl.ds(r0 + halo_rows, FILL_RB), :] = zz[:, :D_CF] * _sigmoid(zz[:, D_CF:])
        return carry

    lax.fori_loop(0, TM // FILL_RB, glu, 0)
    _fill_time_halo(pad_ref, CF_HALO, i < CTX_TILES)

    _expand_rows(wexp_ref, 0, cw_ref[0])
    _expand_rows(wexp_ref, CF_CONV * SUBLANES, cb_ref[0])

    def emit(row0, lb, block):
        blk_ref[:, lb * LANES:(lb + 1) * LANES] = block

    def post(row0):
        y = blk_ref[...]
        mu = jnp.mean(y, axis=-1, keepdims=True)
        yc = y - mu
        ln = yc * lax.rsqrt(jnp.mean(yc * yc, axis=-1, keepdims=True) + EPS) * g_ref[0] + b_ref[0]
        yc_ref[pl.ds(row0, CONV_RB), :] = (ln * _sigmoid(ln)).astype(yc_ref.dtype)

    _dwconv(pad_ref, CF_HALO, -CF_HALO, CF_CONV, wexp_ref, emit, post)


def _conformer(l, z, cf_conv_w, cf_conv_b3, cf_ln_g3, cf_ln_b3):
    return _layer_call(
        _conformer_body,
        grid=(N_TILES,),
        in_specs=[pl.BlockSpec((TM, 2 * D_CF), lambda i, l: (i, 2)),
                  pl.BlockSpec((1, CF_CONV, D_CF), lambda i, l: (l[0], 0, 0)),
                  pl.BlockSpec((1, 1, D_CF), lambda i, l: (l[0], 0, 0)),
                  pl.BlockSpec((1, 1, D_CF), lambda i, l: (l[0], 0, 0)),
                  pl.BlockSpec((1, 1, D_CF), lambda i, l: (l[0], 0, 0))],
        out_specs=pl.BlockSpec((TM, D_CF), lambda i, l: (i, 0)),
        out_shape=jax.ShapeDtypeStruct((N_TOK, D_CF), BF16),
        scratch_shapes=[pltpu.VMEM((TM + 2 * CF_HALO * SUBLANES, D_CF), F32),
                        pltpu.VMEM(((CF_CONV + 1) * SUBLANES, D_CF), F32),
                        pltpu.VMEM((CONV_RB, D_CF), F32)],
        name="conformer")(l, z, cf_conv_w, cf_conv_b3, cf_ln_g3, cf_ln_b3)


def _merge_body(l_ref, x_ref, ya_ref, yb_ref, yc_ref, zg_ref, mod_ref, gn_ref, wb_ref, wo_ref,
                o_ref):
    i = pl.program_id(0)
    merged = None
    for p, y_ref in enumerate((ya_ref, yb_ref, yc_ref)):
        gate = _sigmoid(zg_ref[:, p * D_MODEL:(p + 1) * D_MODEL].astype(F32))
        term = gate * jnp.dot(y_ref[...], wb_ref[0, p], preferred_element_type=F32)
        merged = term if merged is None else merged + term
    out = jnp.dot(merged.astype(BF16), wo_ref[0], preferred_element_type=F32)
    m = mod_ref[0, _mod_row_of_tile(i, TM_MERGE)]
    gate1 = m[:, 2 * D_MODEL:3 * D_MODEL]
    o_ref[...] = x_ref[...] + gate1 * _rms_rows(out, gn_ref[0, 1:2, :])


def _merge(l, x, ya, yb, yc, z, mod4, g_norm, wb_bf16, wo_bf16):
    tile = lambda i, l: (i, 0)
    return _layer_call(
        _merge_body,
        grid=(N_TOK // TM_MERGE,),
        in_specs=[pl.BlockSpec((TM_MERGE, D_MODEL), tile),
                  pl.BlockSpec((TM_MERGE, D_MODEL), tile),
                  pl.BlockSpec((TM_MERGE, D_MODEL), tile),
                  pl.BlockSpec((TM_MERGE, D_MODEL), tile),
                  pl.BlockSpec((TM_MERGE, 3 * D_MODEL), lambda i, l: (i, 2)),
                  pl.BlockSpec((1, MOD_ROWS, 1, 6 * D_MODEL), lambda i, l: (l[0], 0, 0, 0)),
                  pl.BlockSpec((1, 4, D_MODEL), lambda i, l: (l[0], 0, 0)),
                  pl.BlockSpec((1, 3, D_MODEL, D_MODEL), lambda i, l: (l[0], 0, 0, 0)),
                  pl.BlockSpec((1, D_MODEL, D_MODEL), lambda i, l: (l[0], 0, 0))],
        out_specs=pl.BlockSpec((TM_MERGE, D_MODEL), tile),
        out_shape=jax.ShapeDtypeStruct((N_TOK, D_MODEL), F32),
        scratch_shapes=[],
        name="merge")(l, x, ya, yb, yc, z, mod4, g_norm, wb_bf16, wo_bf16)


FFN_HALO = 1
N_FF_STEPS = D_FF // FF_CW
FFN_BLOCKS = TM // CONV_RB
FFN_SPLIT = (TM // 2 - FFN_HALO * SUBLANES) // CONV_RB


def _ffn_body(l_ref, x_ref, mod_ref, gn_ref, wg_ref, wv_ref, cwg_ref, cwv_ref, cbg_ref, cbv_ref,
              wd_ref, o_ref, h_ref, acc_ref, padg_ref, padv_ref, act_ref, wgb_ref, wvb_ref, wdb_ref,
              wexp_ref):
    i = pl.program_id(0)
    c = pl.program_id(1)
    m = mod_ref[0, _mod_row_of_tile(i, TM)]
    halo_rows = FFN_HALO * SUBLANES

    @pl.when(c == 0)
    def _():
        _modulated_norm_to(h_ref, x_ref, gn_ref[0, 2:3, :],
                           m[:, 4 * D_MODEL:5 * D_MODEL], m[:, 3 * D_MODEL:4 * D_MODEL])
        acc_ref[...] = jnp.zeros_like(acc_ref)

    wgb_ref[...] = wg_ref[0].astype(BF16)
    wvb_ref[...] = wv_ref[0].astype(BF16)
    wdb_ref[...] = wd_ref[0].astype(BF16)

    def up(r0, r1):
        h = h_ref[r0:r1, :]
        padg_ref[halo_rows + r0:halo_rows + r1, :] = jnp.dot(h, wgb_ref[...], preferred_element_type=F32)
        padv_ref[halo_rows + r0:halo_rows + r1, :] = jnp.dot(h, wvb_ref[...], preferred_element_type=F32)

    n_wrows = (FFN_CONV + 1) * SUBLANES
    for s, (cw_ref, cb_ref) in enumerate(((cwg_ref, cbg_ref), (cwv_ref, cbv_ref))):
        _expand_rows(wexp_ref, s * n_wrows, cw_ref[0])
        _expand_rows(wexp_ref, s * n_wrows + FFN_CONV * SUBLANES, cb_ref[0])

    def conv_act(rb):
        base = rb * CONV_RB
        for lb in range(FF_CW // LANES):
            lanes = slice(lb * LANES, (lb + 1) * LANES)
            outs = []
            for s, pad_ref in enumerate((padg_ref, padv_ref)):
                outs.append(_conv_block(pad_ref, base, lanes, wexp_ref, s * n_wrows, FFN_CONV))
            act_ref[base:base + CONV_RB, lanes] = (jax.nn.gelu(outs[0]) * outs[1]).astype(act_ref.dtype)

    def down(r0, r1):
        acc_ref[r0:r1, :] += jnp.dot(act_ref[r0:r1, :], wdb_ref[...], preferred_element_type=F32)

    up(0, TM // 2)
    up(TM // 2, TM)
    is_ctx = i < CTX_TILES
    _fill_time_halo(padg_ref, FFN_HALO, is_ctx)
    _fill_time_halo(padv_ref, FFN_HALO, is_ctx)
    for rb in range(1, FFN_SPLIT):
        conv_act(rb)
    conv_act(0)
    down(0, FFN_SPLIT * CONV_RB)
    for rb in range(FFN_SPLIT, FFN_BLOCKS):
        conv_act(rb)
    down(FFN_SPLIT * CONV_RB, TM)

    @pl.when(c == N_FF_STEPS - 1)
    def _():
        gate2 = m[:, 5 * D_MODEL:6 * D_MODEL]
        g3 = gn_ref[0, 3:4, :]

        def body(rb, carry):
            rows = pl.ds(pl.multiple_of(rb * NORM_RB, NORM_RB), NORM_RB)
            o_ref[rows, :] = x_ref[rows, :] + gate2 * _rms_rows(acc_ref[rows, :], g3)
            return carry
        lax.fori_loop(0, TM // NORM_RB, body, 0, unroll=2)


def _ffn(l, x, mod4, g_norm, ffn_up, ffn_conv_w, ffn_conv_b3, ffn_down):
    return _layer_call(
        _ffn_body,
        grid=(N_TILES, N_FF_STEPS),
        in_specs=[pl.BlockSpec((TM, D_MODEL), lambda i, c, l: (i, 0)),
                  pl.BlockSpec((1, MOD_ROWS, 1, 6 * D_MODEL), lambda i, c, l: (l[0], 0, 0, 0)),
                  pl.BlockSpec((1, 4, D_MODEL), lambda i, c, l: (l[0], 0, 0)),
                  pl.BlockSpec((1, D_MODEL, FF_CW), lambda i, c, l: (l[0], 0, c)),
                  pl.BlockSpec((1, D_MODEL, FF_CW), lambda i, c, l: (l[0], 0, N_FF_STEPS + c)),
                  pl.BlockSpec((1, FFN_CONV, FF_CW), lambda i, c, l: (l[0], 0, c)),
                  pl.BlockSpec((1, FFN_CONV, FF_CW), lambda i, c, l: (l[0], 0, N_FF_STEPS + c)),
                  pl.BlockSpec((1, 1, FF_CW), lambda i, c, l: (l[0], 0, c)),
                  pl.BlockSpec((1, 1, FF_CW), lambda i, c, l: (l[0], 0, N_FF_STEPS + c)),
                  pl.BlockSpec((1, FF_CW, D_MODEL), lambda i, c, l: (l[0], c, 0))],
        out_specs=pl.BlockSpec((TM, D_MODEL), lambda i, c, l: (i, 0)),
        out_shape=jax.ShapeDtypeStruct((N_TOK, D_MODEL), F32),
        scratch_shapes=[pltpu.VMEM((TM, D_MODEL), BF16),
                        pltpu.VMEM((TM, D_MODEL), F32),
                        pltpu.VMEM((TM + 2 * FFN_HALO * SUBLANES, FF_CW), F32),
                        pltpu.VMEM((TM + 2 * FFN_HALO * SUBLANES, FF_CW), F32),
                        pltpu.VMEM((TM, FF_CW), BF16),
                        pltpu.VMEM((D_MODEL, FF_CW), BF16),
                        pltpu.VMEM((D_MODEL, FF_CW), BF16),
                        pltpu.VMEM((FF_CW, D_MODEL), BF16),
                        pltpu.VMEM((2 * (FFN_CONV + 1) * SUBLANES, FF_CW), F32)],
        name="ffn")(l, x, mod4, g_norm, ffn_up, ffn_up, ffn_conv_w, ffn_conv_w,
                    ffn_conv_b3, ffn_conv_b3, ffn_down)


def _pos_table():
    t = jnp.arange(DEC_SEQ)
    r = (t // GRID_W).astype(F32)
    col = (t % GRID_W).astype(F32)
    q = D_MODEL // 4
    omega = 1.0 / (POS_BASE ** (jnp.arange(q, dtype=F32) / q))

    def emb(p):
        ang = p[:, None] * omega[None, :]
        return jnp.concatenate([jnp.sin(ang), jnp.cos(ang)], axis=-1)
    return jnp.concatenate([emb(r), emb(col)], axis=-1).astype(F32)


def _block_diag_gates(rg_w):
    per_tile = MXU_DIM // BS_RNN
    n_col = H_RNN // per_tile
    w = rg_w.reshape(DEPTH, 2, 2, n_col, per_tile, BS_RNN, BS_RNN)
    eye = jnp.eye(per_tile, dtype=rg_w.dtype)
    tiles = jnp.einsum('ldkcaij,ab->ldkcaibj', w, eye)
    return tiles.reshape(DEPTH, 2 * 2 * n_col, MXU_DIM, MXU_DIM).astype(BF16)


def kernel(x_prompt, x_sample, state_rglru, c, c_ctx, w_mod, b_mod, g_norm, w_in, b_in,
           rnn_conv_w, rnn_conv_b, rg_w, rg_b, rg_lambda, sg_norm_g, sg_w, sg_b,
           cf_conv_w, cf_conv_b, cf_ln_g, cf_ln_b, w_branch, w_out, ffn_up, ffn_conv_w,
           ffn_conv_b, ffn_down):
    x = _interleave(_prep(x_prompt.reshape(BATCH * SEQ, D_MODEL),
                          x_sample.reshape(DEC_BATCH * DEC_SEQ, D_MODEL), _pos_table()))

    cond = jnp.zeros((MOD_ROWS, D_MODEL), F32).at[:DEC_BATCH].set(c).at[CTX_MOD_ROW].set(c_ctx)
    mod4 = _mod(cond, w_mod, b_mod).reshape(DEPTH, MOD_ROWS, 1, 6 * D_MODEL)

    h0_all = jnp.zeros((DEPTH, N_TILES, SUBLANES, D_RNN), F32)
    h0_all = h0_all.at[:, CTX_TILES:, 0:2, :].set(jnp.transpose(state_rglru.astype(F32), (1, 0, 2, 3)))

    rgw = _block_diag_gates(0.5 * rg_w)
    wb_bf16 = w_branch.astype(BF16)
    wo_bf16 = w_out.astype(BF16)
    b_in3 = b_in.reshape(DEPTH, 1, N_IN)
    rnn_conv_b3 = rnn_conv_b.reshape(DEPTH, 1, D_RNN)
    rg_b4 = 0.5 * rg_b.reshape(DEPTH, 4, D_RNN)
    sg_norm_g3 = sg_norm_g.reshape(DEPTH, 1, D_SG)
    sg_bt = jnp.transpose(sg_b, (0, 2, 1))
    cf_conv_b3 = cf_conv_b.reshape(DEPTH, 1, D_CF)
    cf_ln_g3 = cf_ln_g.reshape(DEPTH, 1, D_CF)
    cf_ln_b3 = cf_ln_b.reshape(DEPTH, 1, D_CF)
    ffn_conv_b3 = ffn_conv_b.reshape(DEPTH, 1, 2 * D_FF)

    states = []
    for layer in range(DEPTH):
        l = jnp.full((1,), layer, jnp.int32)
        z = _inproj(l, x, mod4, g_norm, w_in, b_in3)
        ya, st = _rnn(l, z, rnn_conv_w, rnn_conv_b3, rgw, rg_b4, rg_lambda, h0_all)
        yb = _interleave(_sgu(l, z, sg_norm_g3, sg_w, sg_bt))
        yc = _conformer(l, z, cf_conv_w, cf_conv_b3, cf_ln_g3, cf_ln_b3)
        x = _merge(l, x, ya, yb, yc, z, mod4, g_norm, wb_bf16, wo_bf16)
        x = _ffn(l, x, mod4, g_norm, ffn_up, ffn_conv_w, ffn_conv_b3, ffn_down)
        states.append(st[0, :CTX_TILES])
    states = jnp.stack(states)

    x = _deinterleave(x)
    y_prompt = x[:BATCH * SEQ].reshape(BATCH, SEQ, D_MODEL)
    y_sample = x[BATCH * SEQ:].reshape(DEC_BATCH, DEC_SEQ, D_MODEL)
    new_state = states.reshape(DEPTH, BATCH, 2, D_RNN).transpose(1, 0, 2, 3)
    return (y_prompt, y_sample, new_state)
```

```python
import functools

import jax
import jax.numpy as jnp
import numpy as np
from jax import lax
from jax.experimental import pallas as pl
from jax.experimental.pallas import tpu as pltpu

F32 = jnp.float32
BF16 = jnp.bfloat16

D_MODEL = 1024
BATCH = 16
SEQ = 256
DEPTH = 4
DEC_BATCH = 4
DEC_SEQ = 1024
GRID_W = 64
D_RNN = 1024
H_RNN = 16
BS_RNN = D_RNN // H_RNN
RNN_CONV = 4
C_RG = 8.0
D_SG = 1024
SG_GROUPS = 8
SG_GD = D_SG // SG_GROUPS
CHUNK = 128
D_CF = 1024
CF_CONV = 31
D_FF = 4096
FFN_CONV = 3
N_IN = 2 * D_RNN + 2 * D_SG + 2 * D_CF + 3 * D_MODEL
EPS = 1e-6
POS_BASE = 10000.0

LANES = 128
SUBLANES = 8
MXU_DIM = 256

N_TOK = BATCH * SEQ + DEC_BATCH * DEC_SEQ
TM = 1024
N_TILES = N_TOK // TM
CTX_TILES = BATCH * SEQ // TM
MOD_ROWS = 8
CTX_MOD_ROW = DEC_BATCH
CONV_RB = 64
FF_CW = 512
TM_MERGE = 512
VMEM_LIMIT = 56 * 1024 * 1024


def _sigmoid(x):
    return 0.5 * (jnp.tanh(0.5 * x) + 1.0)


def _rms_rows(x, g):
    return x * lax.rsqrt(jnp.mean(x * x, axis=-1, keepdims=True) + EPS) * g


def _mod_row_of_tile(i, tile_rows):
    tiles_per_latent = DEC_SEQ // tile_rows
    ctx_tiles = BATCH * SEQ // tile_rows
    lat = jnp.maximum(i - ctx_tiles, 0) // tiles_per_latent
    return jnp.where(i < ctx_tiles, CTX_MOD_ROW, lat)


def _params(n_grid):
    return pltpu.CompilerParams(dimension_semantics=("arbitrary",) * n_grid,
                                vmem_limit_bytes=VMEM_LIMIT)


def _layer_call(body, grid, in_specs, out_specs, out_shape, scratch_shapes, name):
    return pl.pallas_call(
        body,
        grid_spec=pltpu.PrefetchScalarGridSpec(
            num_scalar_prefetch=1, grid=grid, in_specs=in_specs, out_specs=out_specs,
            scratch_shapes=scratch_shapes),
        out_shape=out_shape,
        compiler_params=_params(len(grid)),
        name=name)


STEPS = CHUNK
HALO_LW = 256


def _interleave(x):
    return x.reshape(N_TILES, SUBLANES, STEPS, -1).transpose(0, 2, 1, 3).reshape(x.shape)


def _deinterleave(x):
    return x.reshape(N_TILES, STEPS, SUBLANES, -1).transpose(0, 2, 1, 3).reshape(x.shape)


def _chunk_edge_masks(is_ctx, shape):
    c = lax.broadcasted_iota(jnp.int32, shape, 0) & (SUBLANES - 1)
    m = jnp.where(is_ctx, SEQ // CHUNK - 1, DEC_SEQ // CHUNK - 1)
    cm = c & m
    return cm == 0, cm == m


def _fill_time_halo(pad_ref, halo, is_ctx):
    n = halo * SUBLANES
    for lo in range(0, pad_ref.shape[1], HALO_LW):
        lanes = slice(lo, lo + HALO_LW)
        first, last = _chunk_edge_masks(is_ctx, (n, HALO_LW))
        tail = pad_ref[TM:TM + n, lanes]
        head = pad_ref[n:2 * n, lanes]
        pad_ref[0:n, lanes] = jnp.where(first, 0.0, pltpu.roll(tail, 1, 0))
        pad_ref[n + TM:2 * n + TM, lanes] = jnp.where(last, 0.0, pltpu.roll(head, n - 1, 0))


def _expand_rows(dst_ref, row0, rows):
    for k in range(rows.shape[0]):
        lo = row0 + k * SUBLANES
        dst_ref[lo:lo + SUBLANES, :] = jnp.broadcast_to(rows[k:k + 1, :], (SUBLANES, rows.shape[1]))


def _conv_block(pad_ref, row0, lanes, wexp_ref, w0, n_taps):
    def wrow(k):
        return wexp_ref[w0 + k * SUBLANES:w0 + (k + 1) * SUBLANES, lanes]

    groups = CONV_RB // SUBLANES
    if n_taps <= SUBLANES:
        taps = [wrow(k) for k in range(n_taps)]
        accs = [wrow(n_taps)] * groups
        for j in range(n_taps + groups - 1):
            xj = pad_ref[pl.ds(row0 + j * SUBLANES, SUBLANES), lanes]
            for r in range(groups):
                if 0 <= j - r < n_taps:
                    accs[r] = accs[r] + xj * taps[j - r]
        return jnp.concatenate(accs, axis=0)
    acc = jnp.broadcast_to(wrow(n_taps), (groups, SUBLANES, LANES))
    for k in range(n_taps):
        x = pad_ref[pl.ds(row0 + k * SUBLANES, CONV_RB), lanes]
        acc = acc + x.reshape(groups, SUBLANES, LANES) * wrow(k)
    return acc.reshape(CONV_RB, LANES)


def _dwconv(pad_ref, halo, first_offset, n_taps, wexp_ref, emit, post=None):
    def body(b, carry):
        base = pl.multiple_of(b * CONV_RB, CONV_RB)
        for lb in range(pad_ref.shape[1] // LANES):
            lanes = slice(lb * LANES, (lb + 1) * LANES)
            emit(base, lb, _conv_block(pad_ref, base + (halo + first_offset) * SUBLANES, lanes,
                                       wexp_ref, 0, n_taps))
        if post is not None:
            post(base)
        return carry

    lax.fori_loop(0, TM // CONV_RB, body, 0)


def _prep_body(xp_ref, xs_ref, pos_ref, o_ref):
    i = pl.program_id(0)

    @pl.when(i < CTX_TILES)
    def _():
        o_ref[...] = xp_ref[...]

    @pl.when(i >= CTX_TILES)
    def _():
        o_ref[...] = xs_ref[...] + pos_ref[...]


def _prep(xp, xs, pos):
    return pl.pallas_call(
        _prep_body,
        grid=(N_TILES,),
        in_specs=[pl.BlockSpec((TM, D_MODEL), lambda i: (jnp.minimum(i, CTX_TILES - 1), 0)),
                  pl.BlockSpec((TM, D_MODEL), lambda i: (jnp.maximum(i - CTX_TILES, 0), 0)),
                  pl.BlockSpec((DEC_SEQ, D_MODEL), lambda i: (0, 0))],
        out_specs=pl.BlockSpec((TM, D_MODEL), lambda i: (i, 0)),
        out_shape=jax.ShapeDtypeStruct((N_TOK, D_MODEL), F32),
        compiler_params=_params(1),
        name="prep")(xp, xs, pos)


MOD_TN = 3072


def _mod_body(cond_ref, w_ref, b_ref, o_ref):
    c = cond_ref[...]
    s = (c * _sigmoid(c)).astype(BF16)
    o_ref[0] = jnp.dot(s, w_ref[0].astype(BF16), preferred_element_type=F32) + b_ref[0]


def _mod(cond, w_mod, b_mod):
    n_mod = 6 * D_MODEL
    return pl.pallas_call(
        _mod_body,
        grid=(DEPTH, n_mod // MOD_TN),
        in_specs=[pl.BlockSpec((MOD_ROWS, D_MODEL), lambda l, j: (0, 0)),
                  pl.BlockSpec((1, D_MODEL, MOD_TN), lambda l, j: (l, 0, j)),
                  pl.BlockSpec((1, 1, MOD_TN), lambda l, j: (l, 0, j))],
        out_specs=pl.BlockSpec((1, MOD_ROWS, MOD_TN), lambda l, j: (l, 0, j)),
        out_shape=jax.ShapeDtypeStruct((DEPTH, MOD_ROWS, n_mod), F32),
        compiler_params=_params(2),
        name="mod")(cond, w_mod, b_mod.reshape(DEPTH, 1, n_mod))


IN_TN = 1024
NORM_RB = 64


def _modulated_norm_to(h_ref, x_ref, g, scale, shift):
    def body(rb, carry):
        rows = pl.ds(pl.multiple_of(rb * NORM_RB, NORM_RB), NORM_RB)
        h = _rms_rows(x_ref[rows, :], g) * (1.0 + scale) + shift
        h_ref[rows, :] = h.astype(h_ref.dtype)
        return carry
    lax.fori_loop(0, x_ref.shape[0] // NORM_RB, body, 0, unroll=2)


SG_COL = 2 * D_RNN // IN_TN


def _inproj_body(l_ref, x_ref, mod_ref, gn_ref, w_ref, b_ref, z_ref, h_ref, hn_ref, perm_ref):
    i = pl.program_id(0)
    j = pl.program_id(1)

    @pl.when(jnp.logical_and(i == 0, j == 0))
    def _():
        nat = lax.broadcasted_iota(jnp.int32, (TM, TM), 0)
        src = lax.broadcasted_iota(jnp.int32, (TM, TM), 1)
        chunk_shift = CHUNK.bit_length() - 1
        row_of_nat = ((nat & (CHUNK - 1)) * SUBLANES) | jnp.right_shift(nat, chunk_shift)
        perm_ref[...] = jnp.where(src == row_of_nat, 1.0, 0.0).astype(BF16)

    @pl.when(j == 0)
    def _():
        m = mod_ref[0, _mod_row_of_tile(i, TM)]
        _modulated_norm_to(h_ref, x_ref, gn_ref[0, 0:1, :],
                           m[:, D_MODEL:2 * D_MODEL], m[:, 0:D_MODEL])
        hn_ref[...] = jnp.dot(perm_ref[...], h_ref[...], preferred_element_type=F32).astype(BF16)

    natural = jnp.logical_and(j >= SG_COL, j < SG_COL + 2 * D_SG // IN_TN)

    @pl.when(natural)
    def _():
        acc = jnp.dot(hn_ref[...], w_ref[0].astype(BF16), preferred_element_type=F32)
        z_ref[...] = (acc + b_ref[0]).astype(z_ref.dtype)

    @pl.when(jnp.logical_not(natural))
    def _():
        acc = jnp.dot(h_ref[...], w_ref[0].astype(BF16), preferred_element_type=F32)
        z_ref[...] = (acc + b_ref[0]).astype(z_ref.dtype)


def _inproj(l, x, mod4, g_norm, w_in, b_in3):
    return _layer_call(
        _inproj_body,
        grid=(N_TILES, N_IN // IN_TN),
        in_specs=[pl.BlockSpec((TM, D_MODEL), lambda i, j, l: (i, 0)),
                  pl.BlockSpec((1, MOD_ROWS, 1, 6 * D_MODEL), lambda i, j, l: (l[0], 0, 0, 0)),
                  pl.BlockSpec((1, 4, D_MODEL), lambda i, j, l: (l[0], 0, 0)),
                  pl.BlockSpec((1, D_MODEL, IN_TN), lambda i, j, l: (l[0], 0, j)),
                  pl.BlockSpec((1, 1, IN_TN), lambda i, j, l: (l[0], 0, j))],
        out_specs=pl.BlockSpec((TM, IN_TN), lambda i, j, l: (i, j)),
        out_shape=jax.ShapeDtypeStruct((N_TOK, N_IN), BF16),
        scratch_shapes=[pltpu.VMEM((TM, D_MODEL), BF16),
                        pltpu.VMEM((TM, D_MODEL), BF16),
                        pltpu.VMEM((TM, TM), BF16)],
        name="inproj")(l, x, mod4, g_norm, w_in, b_in3)


RNN_HALO = 2
GATE_RB = 256
FILL_RB = 64


def _rows_of(value):
    return [value[r:r + 1, :] for r in range(SUBLANES)]


def _chain_chunks(a_end, b_end, h0_rows, is_ctx, reverse):
    per_ctx_seq = SEQ // CHUNK
    a_rows, b_rows = _rows_of(a_end), _rows_of(b_end)
    h_in, h_out = [None] * SUBLANES, [None] * SUBLANES
    order = range(SUBLANES - 1, -1, -1) if reverse else range(SUBLANES)
    prev = None
    for c in order:
        starts_ctx_seq = (c % per_ctx_seq == per_ctx_seq - 1) if reverse else (c % per_ctx_seq == 0)
        if prev is None:
            h = jnp.where(is_ctx, h0_rows[c // per_ctx_seq], h0_rows[0])
        elif starts_ctx_seq:
            h = jnp.where(is_ctx, h0_rows[c // per_ctx_seq], prev)
        else:
            h = prev
        h_in[c] = h
        prev = a_rows[c] * h + b_rows[c]
        h_out[c] = prev
    return jnp.concatenate(h_in, axis=0), h_out


def _rnn_body(l_ref, zx_ref, zg_ref, cw_ref, cb_ref, rgw_ref, rgb_ref, lam_ref, h0_ref,
              ya_ref, st_ref, pad_ref, wexp_ref, xc_ref, af_ref, uf_ref, ab_ref, ub_ref):
    i = pl.program_id(0)
    is_ctx = i < CTX_TILES
    halo_rows = RNN_HALO * SUBLANES

    def fill(rb, carry):
        r0 = pl.multiple_of(rb * FILL_RB, FILL_RB)
        pad_ref[pl.ds(r0 + halo_rows, FILL_RB), :] = zx_ref[pl.ds(r0, FILL_RB), :].astype(F32)
        return carry

    lax.fori_loop(0, TM // FILL_RB, fill, 0)
    _fill_time_halo(pad_ref, RNN_HALO, is_ctx)

    _expand_rows(wexp_ref, 0, cw_ref[0])
    _expand_rows(wexp_ref, RNN_CONV * SUBLANES, cb_ref[0])

    def emit_xc(row0, lb, block):
        xc_ref[pl.ds(row0, CONV_RB), lb * LANES:(lb + 1) * LANES] = block

    _dwconv(pad_ref, RNN_HALO, -2, RNN_CONV, wexp_ref, emit_xc)

    neg_lam = -lam_ref[0]
    softplus = jnp.maximum(neg_lam, 0.0) + jnp.log1p(jnp.exp(-jnp.abs(neg_lam)))
    coef = (-0.5 * C_RG * np.log2(np.e)) * softplus
    a_refs = (af_ref, ab_ref)
    u_refs = (uf_ref, ub_ref)
    n_col = D_RNN // MXU_DIM

    def gate_body(rb, carry):
        rows = pl.ds(pl.multiple_of(rb * GATE_RB, GATE_RB), GATE_RB)
        for j in range(n_col):
            lanes = slice(j * MXU_DIM, (j + 1) * MXU_DIM)
            xc = xc_ref[rows, lanes]
            xcb = xc.astype(BF16)
            half_xc = 0.5 * xc
            for d in range(2):
                base = d * 2 * n_col
                g_r = jnp.dot(xcb, rgw_ref[0, base + j], preferred_element_type=F32)
                g_i = jnp.dot(xcb, rgw_ref[0, base + n_col + j], preferred_element_type=F32)
                t_r = jnp.tanh(g_r + rgb_ref[0, 2 * d:2 * d + 1, lanes])
                t_i = jnp.tanh(g_i + rgb_ref[0, 2 * d + 1:2 * d + 2, lanes])
                cf = coef[d:d + 1, lanes]
                a = jnp.exp2(cf * t_r + cf)
                y = 1.0 - a * a
                root = jnp.where(y == 0.0, 0.0, y * lax.rsqrt(y))
                a_refs[d][rows, lanes] = a
                u_refs[d][rows, lanes] = root * ((t_i + 1.0) * half_xc)
        return carry

    lax.fori_loop(0, TM // GATE_RB, gate_body, 0)

    def group(t):
        return pl.ds(pl.multiple_of(t * SUBLANES, SUBLANES), SUBLANES)

    def totals(t, carry):
        a_f, b_f, a_b, b_b = carry
        rf, rb = group(t), group(STEPS - 1 - t)
        a = af_ref[rf, :]
        b_f = a * b_f + uf_ref[rf, :]
        a_f = a * a_f
        a = ab_ref[rb, :]
        b_b = a * b_b + ub_ref[rb, :]
        a_b = a * a_b
        return a_f, b_f, a_b, b_b

    ones = jnp.ones((SUBLANES, D_RNN), F32)
    zeros = jnp.zeros((SUBLANES, D_RNN), F32)
    a_f, b_f, a_b, b_b = lax.fori_loop(0, STEPS, totals, (ones, zeros, ones, zeros), unroll=2)

    n_seq = TM // SEQ
    h0_f = [h0_ref[0, 0, 2 * s:2 * s + 1, :] for s in range(n_seq)]
    h0_b = [h0_ref[0, 0, 2 * s + 1:2 * s + 2, :] for s in range(n_seq)]
    hin_f, hout_f = _chain_chunks(a_f, b_f, h0_f, is_ctx, False)
    hin_b, hout_b = _chain_chunks(a_b, b_b, h0_b, is_ctx, True)

    per_ctx_seq = SEQ // CHUNK
    none = jnp.zeros((1, D_RNN), F32)
    for s in range(n_seq):
        last_f = hout_f[s * per_ctx_seq + per_ctx_seq - 1]
        first_b = hout_b[s * per_ctx_seq]
        st_ref[0, 0, 2 * s:2 * s + 1, :] = jnp.where(is_ctx, last_f, hout_f[SUBLANES - 1] if s == 0 else none)
        st_ref[0, 0, 2 * s + 1:2 * s + 2, :] = jnp.where(is_ctx, first_b, hout_b[0] if s == 0 else none)

    def backward(t, h):
        rb = group(STEPS - 1 - t)
        h = ab_ref[rb, :] * h + ub_ref[rb, :]
        ub_ref[rb, :] = h
        return h

    lax.fori_loop(0, STEPS, backward, hin_b, unroll=2)

    def forward(tt, h):
        r0, r1 = group(2 * tt), group(2 * tt + 1)
        h0 = af_ref[r0, :] * h + uf_ref[r0, :]
        h1 = af_ref[r1, :] * h0 + uf_ref[r1, :]
        rows = pl.ds(pl.multiple_of(tt * 2 * SUBLANES, 2 * SUBLANES), 2 * SUBLANES)
        both = jnp.concatenate([h0, h1], axis=0) + ub_ref[rows, :]
        ya_ref[rows, :] = (both * jax.nn.gelu(zg_ref[rows, :].astype(F32))).astype(ya_ref.dtype)
        return h1

    lax.fori_loop(0, STEPS // 2, forward, hin_f, unroll=2)


def _rnn(l, z, rnn_conv_w, rnn_conv_b3, rgw, rg_b4, rg_lambda, h0_all):
    n_rgw = 2 * 2 * (D_RNN // MXU_DIM)
    return _layer_call(
        _rnn_body,
        grid=(N_TILES,),
        in_specs=[pl.BlockSpec((TM, D_RNN), lambda i, l: (i, 0)),
                  pl.BlockSpec((TM, D_RNN), lambda i, l: (i, 1)),
                  pl.BlockSpec((1, RNN_CONV, D_RNN), lambda i, l: (l[0], 0, 0)),
                  pl.BlockSpec((1, 1, D_RNN), lambda i, l: (l[0], 0, 0)),
                  pl.BlockSpec((1, n_rgw, MXU_DIM, MXU_DIM), lambda i, l: (l[0], 0, 0, 0)),
                  pl.BlockSpec((1, 4, D_RNN), lambda i, l: (l[0], 0, 0)),
                  pl.BlockSpec((1, 2, D_RNN), lambda i, l: (l[0], 0, 0)),
                  pl.BlockSpec((1, 1, SUBLANES, D_RNN), lambda i, l: (l[0], i, 0, 0))],
        out_specs=[pl.BlockSpec((TM, D_MODEL), lambda i, l: (i, 0)),
                   pl.BlockSpec((1, 1, SUBLANES, D_RNN), lambda i, l: (0, i, 0, 0))],
        out_shape=[jax.ShapeDtypeStruct((N_TOK, D_RNN), BF16),
                   jax.ShapeDtypeStruct((1, N_TILES, SUBLANES, D_RNN), F32)],
        scratch_shapes=[pltpu.VMEM((TM + 2 * RNN_HALO * SUBLANES, D_RNN), F32),
                        pltpu.VMEM(((RNN_CONV + 1) * SUBLANES, D_RNN), F32),
                        pltpu.VMEM((TM, D_RNN), F32),
                        pltpu.VMEM((TM, D_RNN), F32),
                        pltpu.VMEM((TM, D_RNN), F32),
                        pltpu.VMEM((TM, D_RNN), F32),
                        pltpu.VMEM((TM, D_RNN), F32)],
        name="rnn")(l, z, z, rnn_conv_w, rnn_conv_b3, rgw, rg_b4, rg_lambda, h0_all)


ACT_RB = 64
N_CHUNKS = TM // CHUNK


def _sgu_body(l_ref, z_ref, g_ref, w_ref, bt_ref, yb_ref, su_ref, sv_ref):
    gain = g_ref[0]

    def act_body(rb, carry):
        rows = pl.ds(pl.multiple_of(rb * ACT_RB, ACT_RB), ACT_RB)
        uv = jax.nn.gelu(z_ref[rows, :].astype(F32))
        su_ref[rows, :] = uv[:, :D_SG]
        sv = uv[:, D_SG:]
        mu = jnp.mean(sv, axis=-1, keepdims=True)
        svc = sv - mu
        y = svc * lax.rsqrt(jnp.mean(svc * svc, axis=-1, keepdims=True) + EPS) * gain
        sv_ref[rows, :] = y.astype(sv_ref.dtype)
        return carry

    lax.fori_loop(0, TM // ACT_RB, act_body, 0, unroll=2)

    for g in range(SG_GROUPS):
        lanes = slice(g * SG_GD, (g + 1) * SG_GD)
        rhs = jnp.concatenate([sv_ref[n * CHUNK:(n + 1) * CHUNK, lanes] for n in range(N_CHUNKS)],
                              axis=1)
        mixed = jnp.dot(w_ref[0, g].astype(BF16), rhs, preferred_element_type=F32)
        bias = bt_ref[0, :, g:g + 1]
        for n in range(N_CHUNKS):
            rows = slice(n * CHUNK, (n + 1) * CHUNK)
            m = mixed[:, n * SG_GD:(n + 1) * SG_GD] + bias
            yb_ref[rows, lanes] = (su_ref[rows, lanes] * m).astype(yb_ref.dtype)


def _sgu(l, z, sg_norm_g3, sg_w, sg_bt):
    return _layer_call(
        _sgu_body,
        grid=(N_TILES,),
        in_specs=[pl.BlockSpec((TM, 2 * D_SG), lambda i, l: (i, 1)),
                  pl.BlockSpec((1, 1, D_SG), lambda i, l: (l[0], 0, 0)),
                  pl.BlockSpec((1, SG_GROUPS, CHUNK, CHUNK), lambda i, l: (l[0], 0, 0, 0)),
                  pl.BlockSpec((1, CHUNK, SG_GROUPS), lambda i, l: (l[0], 0, 0))],
        out_specs=pl.BlockSpec((TM, D_SG), lambda i, l: (i, 0)),
        out_shape=jax.ShapeDtypeStruct((N_TOK, D_SG), BF16),
        scratch_shapes=[pltpu.VMEM((TM, D_SG), F32), pltpu.VMEM((TM, D_SG), BF16)],
        name="sgu")(l, z, sg_norm_g3, sg_w, sg_bt)


CF_HALO = CF_CONV // 2


def _conformer_body(l_ref, z_ref, cw_ref, cb_ref, g_ref, b_ref, yc_ref, pad_ref, wexp_ref, blk_ref):
    i = pl.program_id(0)
    halo_rows = CF_HALO * SUBLANES

    def glu(rb, carry):
        r0 = pl.multiple_of(rb * FILL_RB, FILL_RB)
        zz = z_ref[pl.ds(r0, FILL_RB), :].astype(F32)
        pad_ref[pl.ds(r0 + halo_rows, FILL_RB), :] = zz[:, :D_CF] * _sigmoid(zz[:, D_CF:])
        return carry

    lax.fori_loop(0, TM // FILL_RB, glu, 0)
    _fill_time_halo(pad_ref, CF_HALO, i < CTX_TILES)

    _expand_rows(wexp_ref, 0, cw_ref[0])
    _expand_rows(wexp_ref, CF_CONV * SUBLANES, cb_ref[0])

    def emit(row0, lb, block):
        blk_ref[:, lb * LANES:(lb + 1) * LANES] = block

    def post(row0):
        y = blk_ref[...]
        mu = jnp.mean(y, axis=-1, keepdims=True)
        yc = y - mu
        ln = yc * lax.rsqrt(jnp.mean(yc * yc, axis=-1, keepdims=True) + EPS) * g_ref[0] + b_ref[0]
        yc_ref[pl.ds(row0, CONV_RB), :] = (ln * _sigmoid(ln)).astype(yc_ref.dtype)

    _dwconv(pad_ref, CF_HALO, -CF_HALO, CF_CONV, wexp_ref, emit, post)


def _conformer(l, z, cf_conv_w, cf_conv_b3, cf_ln_g3, cf_ln_b3):
    return _layer_call(
        _conformer_body,
        grid=(N_TILES,),
        in_specs=[pl.BlockSpec((TM, 2 * D_CF), lambda i, l: (i, 2)),
                  pl.BlockSpec((1, CF_CONV, D_CF), lambda i, l: (l[0], 0, 0)),
                  pl.BlockSpec((1, 1, D_CF), lambda i, l: (l[0], 0, 0)),
                  pl.BlockSpec((1, 1, D_CF), lambda i, l: (l[0], 0, 0)),
                  pl.BlockSpec((1, 1, D_CF), lambda i, l: (l[0], 0, 0))],
        out_specs=pl.BlockSpec((TM, D_CF), lambda i, l: (i, 0)),
        out_shape=jax.ShapeDtypeStruct((N_TOK, D_CF), BF16),
        scratch_shapes=[pltpu.VMEM((TM + 2 * CF_HALO * SUBLANES, D_CF), F32),
                        pltpu.VMEM(((CF_CONV + 1) * SUBLANES, D_CF), F32),
                        pltpu.VMEM((CONV_RB, D_CF), F32)],
        name="conformer")(l, z, cf_conv_w, cf_conv_b3, cf_ln_g3, cf_ln_b3)


def _merge_body(l_ref, x_ref, ya_ref, yb_ref, yc_ref, zg_ref, mod_ref, gn_ref, wb_ref, wo_ref,
                o_ref):
    i = pl.program_id(0)
    merged = None
    for p, y_ref in enumerate((ya_ref, yb_ref, yc_ref)):
        gate = _sigmoid(zg_ref[:, p * D_MODEL:(p + 1) * D_MODEL].astype(F32))
        term = gate * jnp.dot(y_ref[...], wb_ref[0, p], preferred_element_type=F32)
        merged = term if merged is None else merged + term
    out = jnp.dot(merged.astype(BF16), wo_ref[0], preferred_element_type=F32)
    m = mod_ref[0, _mod_row_of_tile(i, TM_MERGE)]
    gate1 = m[:, 2 * D_MODEL:3 * D_MODEL]
    o_ref[...] = x_ref[...] + gate1 * _rms_rows(out, gn_ref[0, 1:2, :])


def _merge(l, x, ya, yb, yc, z, mod4, g_norm, wb_bf16, wo_bf16):
    tile = lambda i, l: (i, 0)
    return _layer_call(
        _merge_body,
        grid=(N_TOK // TM_MERGE,),
        in_specs=[pl.BlockSpec((TM_MERGE, D_MODEL), tile),
                  pl.BlockSpec((TM_MERGE, D_MODEL), tile),
                  pl.BlockSpec((TM_MERGE, D_MODEL), tile),
                  pl.BlockSpec((TM_MERGE, D_MODEL), tile),
                  pl.BlockSpec((TM_MERGE, 3 * D_MODEL), lambda i, l: (i, 2)),
                  pl.BlockSpec((1, MOD_ROWS, 1, 6 * D_MODEL), lambda i, l: (l[0], 0, 0, 0)),
                  pl.BlockSpec((1, 4, D_MODEL), lambda i, l: (l[0], 0, 0)),
                  pl.BlockSpec((1, 3, D_MODEL, D_MODEL), lambda i, l: (l[0], 0, 0, 0)),
                  pl.BlockSpec((1, D_MODEL, D_MODEL), lambda i, l: (l[0], 0, 0))],
        out_specs=pl.BlockSpec((TM_MERGE, D_MODEL), tile),
        out_shape=jax.ShapeDtypeStruct((N_TOK, D_MODEL), F32),
        scratch_shapes=[],
        name="merge")(l, x, ya, yb, yc, z, mod4, g_norm, wb_bf16, wo_bf16)


FFN_HALO = 1
N_FF_STEPS = D_FF // FF_CW
FFN_BLOCKS = TM // CONV_RB
FFN_SPLIT = (TM // 2 - FFN_HALO * SUBLANES) // CONV_RB


def _ffn_body(l_ref, x_ref, mod_ref, gn_ref, wg_ref, wv_ref, cwg_ref, cwv_ref, cbg_ref, cbv_ref,
              wd_ref, o_ref, h_ref, acc_ref, padg_ref, padv_ref, act_ref, wgb_ref, wvb_ref, wdb_ref,
              wexp_ref):
    i = pl.program_id(0)
    c = pl.program_id(1)
    m = mod_ref[0, _mod_row_of_tile(i, TM)]
    halo_rows = FFN_HALO * SUBLANES

    @pl.when(c == 0)
    def _():
        _modulated_norm_to(h_ref, x_ref, gn_ref[0, 2:3, :],
                           m[:, 4 * D_MODEL:5 * D_MODEL], m[:, 3 * D_MODEL:4 * D_MODEL])
        acc_ref[...] = jnp.zeros_like(acc_ref)

    wgb_ref[...] = wg_ref[0].astype(BF16)
    wvb_ref[...] = wv_ref[0].astype(BF16)
    wdb_ref[...] = wd_ref[0].astype(BF16)

    def up(r0, r1):
        h = h_ref[r0:r1, :]
        padg_ref[halo_rows + r0:halo_rows + r1, :] = jnp.dot(h, wgb_ref[...], preferred_element_type=F32)
        padv_ref[halo_rows + r0:halo_rows + r1, :] = jnp.dot(h, wvb_ref[...], preferred_element_type=F32)

    n_wrows = (FFN_CONV + 1) * SUBLANES
    for s, (cw_ref, cb_ref) in enumerate(((cwg_ref, cbg_ref), (cwv_ref, cbv_ref))):
        _expand_rows(wexp_ref, s * n_wrows, cw_ref[0])
        _expand_rows(wexp_ref, s * n_wrows + FFN_CONV * SUBLANES, cb_ref[0])

    def conv_act(rb):
        base = rb * CONV_RB
        for lb in range(FF_CW // LANES):
            lanes = slice(lb * LANES, (lb + 1) * LANES)
            outs = []
            for s, pad_ref in enumerate((padg_ref, padv_ref)):
                outs.append(_conv_block(pad_ref, base, lanes, wexp_ref, s * n_wrows, FFN_CONV))
            act_ref[base:base + CONV_RB, lanes] = (jax.nn.gelu(outs[0]) * outs[1]).astype(act_ref.dtype)

    def down(r0, r1):
        acc_ref[r0:r1, :] += jnp.dot(act_ref[r0:r1, :], wdb_ref[...], preferred_element_type=F32)

    up(0, TM // 2)
    up(TM // 2, TM)
    is_ctx = i < CTX_TILES
    _fill_time_halo(padg_ref, FFN_HALO, is_ctx)
    _fill_time_halo(padv_ref, FFN_HALO, is_ctx)
    for rb in range(1, FFN_SPLIT):
        conv_act(rb)
    conv_act(0)
    down(0, FFN_SPLIT * CONV_RB)
    for rb in range(FFN_SPLIT, FFN_BLOCKS):
        conv_act(rb)
    down(FFN_SPLIT * CONV_RB, TM)

    @pl.when(c == N_FF_STEPS - 1)
    def _():
        gate2 = m[:, 5 * D_MODEL:6 * D_MODEL]
        g3 = gn_ref[0, 3:4, :]

        def body(rb, carry):
            rows = pl.ds(pl.multiple_of(rb * NORM_RB, NORM_RB), NORM_RB)
            o_ref[rows, :] = x_ref[rows, :] + gate2 * _rms_rows(acc_ref[rows, :], g3)
            return carry
        lax.fori_loop(0, TM // NORM_RB, body, 0, unroll=2)


def _ffn(l, x, mod4, g_norm, ffn_up, ffn_conv_w, ffn_conv_b3, ffn_down):
    return _layer_call(
        _ffn_body,
        grid=(N_TILES, N_FF_STEPS),
        in_specs=[pl.BlockSpec((TM, D_MODEL), lambda i, c, l: (i, 0)),
                  pl.BlockSpec((1, MOD_ROWS, 1, 6 * D_MODEL), lambda i, c, l: (l[0], 0, 0, 0)),
                  pl.BlockSpec((1, 4, D_MODEL), lambda i, c, l: (l[0], 0, 0)),
                  pl.BlockSpec((1, D_MODEL, FF_CW), lambda i, c, l: (l[0], 0, c)),
                  pl.BlockSpec((1, D_MODEL, FF_CW), lambda i, c, l: (l[0], 0, N_FF_STEPS + c)),
                  pl.BlockSpec((1, FFN_CONV, FF_CW), lambda i, c, l: (l[0], 0, c)),
                  pl.BlockSpec((1, FFN_CONV, FF_CW), lambda i, c, l: (l[0], 0, N_FF_STEPS + c)),
                  pl.BlockSpec((1, 1, FF_CW), lambda i, c, l: (l[0], 0, c)),
                  pl.BlockSpec((1, 1, FF_CW), lambda i, c, l: (l[0], 0, N_FF_STEPS + c)),
                  pl.BlockSpec((1, FF_CW, D_MODEL), lambda i, c, l: (l[0], c, 0))],
        out_specs=pl.BlockSpec((TM, D_MODEL), lambda i, c, l: (i, 0)),
        out_shape=jax.ShapeDtypeStruct((N_TOK, D_MODEL), F32),
        scratch_shapes=[pltpu.VMEM((TM, D_MODEL), BF16),
                        pltpu.VMEM((TM, D_MODEL), F32),
                        pltpu.VMEM((TM + 2 * FFN_HALO * SUBLANES, FF_CW), F32),
                        pltpu.VMEM((TM + 2 * FFN_HALO * SUBLANES, FF_CW), F32),
                        pltpu.VMEM((TM, FF_CW), BF16),
                        pltpu.VMEM((D_MODEL, FF_CW), BF16),
                        pltpu.VMEM((D_MODEL, FF_CW), BF16),
                        pltpu.VMEM((FF_CW, D_MODEL), BF16),
                        pltpu.VMEM((2 * (FFN_CONV + 1) * SUBLANES, FF_CW), F32)],
        name="ffn")(l, x, mod4, g_norm, ffn_up, ffn_up, ffn_conv_w, ffn_conv_w,
                    ffn_conv_b3, ffn_conv_b3, ffn_down)


def _pos_table():
    t = jnp.arange(DEC_SEQ)
    r = (t // GRID_W).astype(F32)
    col = (t % GRID_W).astype(F32)
    q = D_MODEL // 4
    omega = 1.0 / (POS_BASE ** (jnp.arange(q, dtype=F32) / q))

    def emb(p):
        ang = p[:, None] * omega[None, :]
        return jnp.concatenate([jnp.sin(ang), jnp.cos(ang)], axis=-1)
    return jnp.concatenate([emb(r), emb(col)], axis=-1).astype(F32)


def _block_diag_gates(rg_w):
    per_tile = MXU_DIM // BS_RNN
    n_col = H_RNN // per_tile
    w = rg_w.reshape(DEPTH, 2, 2, n_col, per_tile, BS_RNN, BS_RNN)
    eye = jnp.eye(per_tile, dtype=rg_w.dtype)
    tiles = jnp.einsum('ldkcaij,ab->ldkcaibj', w, eye)
    return tiles.reshape(DEPTH, 2 * 2 * n_col, MXU_DIM, MXU_DIM).astype(BF16)


def kernel(x_prompt, x_sample, state_rglru, c, c_ctx, w_mod, b_mod, g_norm, w_in, b_in,
           rnn_conv_w, rnn_conv_b, rg_w, rg_b, rg_lambda, sg_norm_g, sg_w, sg_b,
           cf_conv_w, cf_conv_b, cf_ln_g, cf_ln_b, w_branch, w_out, ffn_up, ffn_conv_w,
           ffn_conv_b, ffn_down):
    x = _interleave(_prep(x_prompt.reshape(BATCH * SEQ, D_MODEL),
                          x_sample.reshape(DEC_BATCH * DEC_SEQ, D_MODEL), _pos_table()))

    cond = jnp.zeros((MOD_ROWS, D_MODEL), F32).at[:DEC_BATCH].set(c).at[CTX_MOD_ROW].set(c_ctx)
    mod4 = _mod(cond, w_mod, b_mod).reshape(DEPTH, MOD_ROWS, 1, 6 * D_MODEL)

    h0_all = jnp.zeros((DEPTH, N_TILES, SUBLANES, D_RNN), F32)
    h0_all = h0_all.at[:, CTX_TILES:, 0:2, :].set(jnp.transpose(state_rglru.astype(F32), (1, 0, 2, 3)))

    rgw = _block_diag_gates(0.5 * rg_w)
    wb_bf16 = w_branch.astype(BF16)
    wo_bf16 = w_out.astype(BF16)
    b_in3 = b_in.reshape(DEPTH, 1, N_IN)
    rnn_conv_b3 = rnn_conv_b.reshape(DEPTH, 1, D_RNN)
    rg_b4 = 0.5 * rg_b.reshape(DEPTH, 4, D_RNN)
    sg_norm_g3 = sg_norm_g.reshape(DEPTH, 1, D_SG)
    sg_bt = jnp.transpose(sg_b, (0, 2, 1))
    cf_conv_b3 = cf_conv_b.reshape(DEPTH, 1, D_CF)
    cf_ln_g3 = cf_ln_g.reshape(DEPTH, 1, D_CF)
    cf_ln_b3 = cf_ln_b.reshape(DEPTH, 1, D_CF)
    ffn_conv_b3 = ffn_conv_b.reshape(DEPTH, 1, 2 * D_FF)

    states = []
    for layer in range(DEPTH):
        l = jnp.full((1,), layer, jnp.int32)
        z = _inproj(l, x, mod4, g_norm, w_in, b_in3)
        ya, st = _rnn(l, z, rnn_conv_w, rnn_conv_b3, rgw, rg_b4, rg_lambda, h0_all)
        yb = _interleave(_sgu(l, z, sg_norm_g3, sg_w, sg_bt))
        yc = _conformer(l, z, cf_conv_w, cf_conv_b3, cf_ln_g3, cf_ln_b3)
        x = _merge(l, x, ya, yb, yc, z, mod4, g_norm, wb_bf16, wo_bf16)
        x = _ffn(l, x, mod4, g_norm, ffn_up, ffn_conv_w, ffn_conv_b3, ffn_down)
        states.append(st[0, :CTX_TILES])
    states = jnp.stack(states)

    x = _deinterleave(x)
    y_prompt = x[:BATCH * SEQ].reshape(BATCH, SEQ, D_MODEL)
    y_sample = x[BATCH * SEQ:].reshape(DEC_BATCH, DEC_SEQ, D_MODEL)
    new_state = states.reshape(DEPTH, BATCH, 2, D_RNN).transpose(1, 0, 2, 3)
    return (y_prompt, y_sample, new_state)
```

```python
import functools

import jax
import jax.numpy as jnp
import numpy as np
from jax import lax
from jax.experimental import pallas as pl
from jax.experimental.pallas import tpu as pltpu

F32 = jnp.float32
BF16 = jnp.bfloat16

D_MODEL = 1024
BATCH = 16
SEQ = 256
DEPTH = 4
DEC_BATCH = 4
DEC_SEQ = 1024
GRID_W = 64
D_RNN = 1024
H_RNN = 16
BS_RNN = D_RNN // H_RNN
RNN_CONV = 4
C_RG = 8.0
D_SG = 1024
SG_GROUPS = 8
SG_GD = D_SG // SG_GROUPS
CHUNK = 128
D_CF = 1024
CF_CONV = 31
D_FF = 4096
FFN_CONV = 3
N_IN = 2 * D_RNN + 2 * D_SG + 2 * D_CF + 3 * D_MODEL
EPS = 1e-6
POS_BASE = 10000.0

LANES = 128
SUBLANES = 8
MXU_DIM = 256

N_TOK = BATCH * SEQ + DEC_BATCH * DEC_SEQ
TM = 1024
N_TILES = N_TOK // TM
CTX_TILES = BATCH * SEQ // TM
MOD_ROWS = 8
CTX_MOD_ROW = DEC_BATCH
CONV_RB = 64
FF_CW = 512
TM_MERGE = 512
VMEM_LIMIT = 56 * 1024 * 1024


def _sigmoid(x):
    return 0.5 * (jnp.tanh(0.5 * x) + 1.0)


def _rms_rows(x, g):
    return x * lax.rsqrt(jnp.mean(x * x, axis=-1, keepdims=True) + EPS) * g


def _mod_row_of_tile(i, tile_rows):
    tiles_per_latent = DEC_SEQ // tile_rows
    ctx_tiles = BATCH * SEQ // tile_rows
    lat = jnp.maximum(i - ctx_tiles, 0) // tiles_per_latent
    return jnp.where(i < ctx_tiles, CTX_MOD_ROW, lat)


def _params(n_grid):
    return pltpu.CompilerParams(dimension_semantics=("arbitrary",) * n_grid,
                                vmem_limit_bytes=VMEM_LIMIT)


def _layer_call(body, grid, in_specs, out_specs, out_shape, scratch_shapes, name):
    return pl.pallas_call(
        body,
        grid_spec=pltpu.PrefetchScalarGridSpec(
            num_scalar_prefetch=1, grid=grid, in_specs=in_specs, out_specs=out_specs,
            scratch_shapes=scratch_shapes),
        out_shape=out_shape,
        compiler_params=_params(len(grid)),
        name=name)


STEPS = CHUNK
HALO_LW = 256


def _interleave(x):
    return x.reshape(N_TILES, SUBLANES, STEPS, -1).transpose(0, 2, 1, 3).reshape(x.shape)


def _deinterleave(x):
    return x.reshape(N_TILES, STEPS, SUBLANES, -1).transpose(0, 2, 1, 3).reshape(x.shape)


def _chunk_edge_masks(is_ctx, shape):
    c = lax.broadcasted_iota(jnp.int32, shape, 0) & (SUBLANES - 1)
    m = jnp.where(is_ctx, SEQ // CHUNK - 1, DEC_SEQ // CHUNK - 1)
    cm = c & m
    return cm == 0, cm == m


def _fill_time_halo(pad_ref, halo, is_ctx):
    n = halo * SUBLANES
    for lo in range(0, pad_ref.shape[1], HALO_LW):
        lanes = slice(lo, lo + HALO_LW)
        first, last = _chunk_edge_masks(is_ctx, (n, HALO_LW))
        tail = pad_ref[TM:TM + n, lanes]
        head = pad_ref[n:2 * n, lanes]
        pad_ref[0:n, lanes] = jnp.where(first, 0.0, pltpu.roll(tail, 1, 0))
        pad_ref[n + TM:2 * n + TM, lanes] = jnp.where(last, 0.0, pltpu.roll(head, n - 1, 0))


def _expand_rows(dst_ref, row0, rows):
    for k in range(rows.shape[0]):
        lo = row0 + k * SUBLANES
        dst_ref[lo:lo + SUBLANES, :] = jnp.broadcast_to(rows[k:k + 1, :], (SUBLANES, rows.shape[1]))


def _conv_block(pad_ref, row0, lanes, wexp_ref, w0, n_taps):
    def wrow(k):
        return wexp_ref[w0 + k * SUBLANES:w0 + (k + 1) * SUBLANES, lanes]

    groups = CONV_RB // SUBLANES
    if n_taps <= SUBLANES:
        taps = [wrow(k) for k in range(n_taps)]
        accs = [wrow(n_taps)] * groups
        for j in range(n_taps + groups - 1):
            xj = pad_ref[pl.ds(row0 + j * SUBLANES, SUBLANES), lanes]
            for r in range(groups):
                if 0 <= j - r < n_taps:
                    accs[r] = accs[r] + xj * taps[j - r]
        return jnp.concatenate(accs, axis=0)
    acc = jnp.broadcast_to(wrow(n_taps), (groups, SUBLANES, LANES))
    for k in range(n_taps):
        x = pad_ref[pl.ds(row0 + k * SUBLANES, CONV_RB), lanes]
        acc = acc + x.reshape(groups, SUBLANES, LANES) * wrow(k)
    return acc.reshape(CONV_RB, LANES)


def _dwconv(pad_ref, halo, first_offset, n_taps, wexp_ref, emit, post=None):
    def body(b, carry):
        base = pl.multiple_of(b * CONV_RB, CONV_RB)
        for lb in range(pad_ref.shape[1] // LANES):
            lanes = slice(lb * LANES, (lb + 1) * LANES)
            emit(base, lb, _conv_block(pad_ref, base + (halo + first_offset) * SUBLANES, lanes,
                                       wexp_ref, 0, n_taps))
        if post is not None:
            post(base)
        return carry

    lax.fori_loop(0, TM // CONV_RB, body, 0)


def _prep_body(xp_ref, xs_ref, pos_ref, o_ref):
    i = pl.program_id(0)

    @pl.when(i < CTX_TILES)
    def _():
        o_ref[...] = xp_ref[...]

    @pl.when(i >= CTX_TILES)
    def _():
        o_ref[...] = xs_ref[...] + pos_ref[...]


def _prep(xp, xs, pos):
    return pl.pallas_call(
        _prep_body,
        grid=(N_TILES,),
        in_specs=[pl.BlockSpec((TM, D_MODEL), lambda i: (jnp.minimum(i, CTX_TILES - 1), 0)),
                  pl.BlockSpec((TM, D_MODEL), lambda i: (jnp.maximum(i - CTX_TILES, 0), 0)),
                  pl.BlockSpec((DEC_SEQ, D_MODEL), lambda i: (0, 0))],
        out_specs=pl.BlockSpec((TM, D_MODEL), lambda i: (i, 0)),
        out_shape=jax.ShapeDtypeStruct((N_TOK, D_MODEL), F32),
        compiler_params=_params(1),
        name="prep")(xp, xs, pos)


MOD_TN = 3072


def _mod_body(cond_ref, w_ref, b_ref, o_ref):
    c = cond_ref[...]
    s = (c * _sigmoid(c)).astype(BF16)
    o_ref[0] = jnp.dot(s, w_ref[0].astype(BF16), preferred_element_type=F32) + b_ref[0]


def _mod(cond, w_mod, b_mod):
    n_mod = 6 * D_MODEL
    return pl.pallas_call(
        _mod_body,
        grid=(DEPTH, n_mod // MOD_TN),
        in_specs=[pl.BlockSpec((MOD_ROWS, D_MODEL), lambda l, j: (0, 0)),
                  pl.BlockSpec((1, D_MODEL, MOD_TN), lambda l, j: (l, 0, j)),
                  pl.BlockSpec((1, 1, MOD_TN), lambda l, j: (l, 0, j))],
        out_specs=pl.BlockSpec((1, MOD_ROWS, MOD_TN), lambda l, j: (l, 0, j)),
        out_shape=jax.ShapeDtypeStruct((DEPTH, MOD_ROWS, n_mod), F32),
        compiler_params=_params(2),
        name="mod")(cond, w_mod, b_mod.reshape(DEPTH, 1, n_mod))


IN_TN = 1024
NORM_RB = 64


def _modulated_norm_to(h_ref, x_ref, g, scale, shift):
    def body(rb, carry):
        rows = pl.ds(pl.multiple_of(rb * NORM_RB, NORM_RB), NORM_RB)
        h = _rms_rows(x_ref[rows, :], g) * (1.0 + scale) + shift
        h_ref[rows, :] = h.astype(h_ref.dtype)
        return carry
    lax.fori_loop(0, x_ref.shape[0] // NORM_RB, body, 0, unroll=2)


SG_COL = 2 * D_RNN // IN_TN
IN_TILES = 2
IN_ROWS = IN_TILES * TM


def _inproj_body(l_ref, x_ref, mod_ref, gn_ref, w_ref, b_ref, z_ref, h_ref, hn_ref, perm_ref):
    i = pl.program_id(0)
    j = pl.program_id(1)

    @pl.when(jnp.logical_and(i == 0, j == 0))
    def _():
        nat = lax.broadcasted_iota(jnp.int32, (TM, TM), 0)
        src = lax.broadcasted_iota(jnp.int32, (TM, TM), 1)
        chunk_shift = CHUNK.bit_length() - 1
        row_of_nat = ((nat & (CHUNK - 1)) * SUBLANES) | jnp.right_shift(nat, chunk_shift)
        perm_ref[...] = jnp.where(src == row_of_nat, 1.0, 0.0).astype(BF16)

    @pl.when(j == 0)
    def _():
        for t in range(IN_TILES):
            rows = pl.ds(t * TM, TM)
            m = mod_ref[0, _mod_row_of_tile(IN_TILES * i + t, TM)]
            _modulated_norm_to(h_ref.at[rows], x_ref.at[rows], gn_ref[0, 0:1, :],
                               m[:, D_MODEL:2 * D_MODEL], m[:, 0:D_MODEL])
            hn_ref[rows, :] = jnp.dot(perm_ref[...], h_ref[rows, :],
                                      preferred_element_type=F32).astype(BF16)

    natural = jnp.logical_and(j >= SG_COL, j < SG_COL + 2 * D_SG // IN_TN)

    @pl.when(natural)
    def _():
        acc = jnp.dot(hn_ref[...], w_ref[0].astype(BF16), preferred_element_type=F32)
        z_ref[...] = (acc + b_ref[0]).astype(z_ref.dtype)

    @pl.when(jnp.logical_not(natural))
    def _():
        acc = jnp.dot(h_ref[...], w_ref[0].astype(BF16), preferred_element_type=F32)
        z_ref[...] = (acc + b_ref[0]).astype(z_ref.dtype)


def _inproj(l, x, mod4, g_norm, w_in, b_in3):
    return _layer_call(
        _inproj_body,
        grid=(N_TOK // IN_ROWS, N_IN // IN_TN),
        in_specs=[pl.BlockSpec((IN_ROWS, D_MODEL), lambda i, j, l: (i, 0)),
                  pl.BlockSpec((1, MOD_ROWS, 1, 6 * D_MODEL), lambda i, j, l: (l[0], 0, 0, 0)),
                  pl.BlockSpec((1, 4, D_MODEL), lambda i, j, l: (l[0], 0, 0)),
                  pl.BlockSpec((1, D_MODEL, IN_TN), lambda i, j, l: (l[0], 0, j)),
                  pl.BlockSpec((1, 1, IN_TN), lambda i, j, l: (l[0], 0, j))],
        out_specs=pl.BlockSpec((IN_ROWS, IN_TN), lambda i, j, l: (i, j)),
        out_shape=jax.ShapeDtypeStruct((N_TOK, N_IN), BF16),
        scratch_shapes=[pltpu.VMEM((IN_ROWS, D_MODEL), BF16),
                        pltpu.VMEM((IN_ROWS, D_MODEL), BF16),
                        pltpu.VMEM((TM, TM), BF16)],
        name="inproj")(l, x, mod4, g_norm, w_in, b_in3)


RNN_HALO = 2
GATE_RB = 256
FILL_RB = 64


def _rows_of(value):
    return [value[r:r + 1, :] for r in range(SUBLANES)]


def _chain_chunks(a_end, b_end, h0_rows, is_ctx, reverse):
    per_ctx_seq = SEQ // CHUNK
    a_rows, b_rows = _rows_of(a_end), _rows_of(b_end)
    h_in, h_out = [None] * SUBLANES, [None] * SUBLANES
    order = range(SUBLANES - 1, -1, -1) if reverse else range(SUBLANES)
    prev = None
    for c in order:
        starts_ctx_seq = (c % per_ctx_seq == per_ctx_seq - 1) if reverse else (c % per_ctx_seq == 0)
        if prev is None:
            h = jnp.where(is_ctx, h0_rows[c // per_ctx_seq], h0_rows[0])
        elif starts_ctx_seq:
            h = jnp.where(is_ctx, h0_rows[c // per_ctx_seq], prev)
        else:
            h = prev
        h_in[c] = h
        prev = a_rows[c] * h + b_rows[c]
        h_out[c] = prev
    return jnp.concatenate(h_in, axis=0), h_out


def _rnn_body(l_ref, zx_ref, zg_ref, cw_ref, cb_ref, rgw_ref, rgb_ref, lam_ref, h0_ref,
              ya_ref, st_ref, pad_ref, wexp_ref, xc_ref, af_ref, uf_ref, ab_ref, ub_ref):
    i = pl.program_id(0)
    is_ctx = i < CTX_TILES
    halo_rows = RNN_HALO * SUBLANES

    def fill(rb, carry):
        r0 = pl.multiple_of(rb * FILL_RB, FILL_RB)
        pad_ref[pl.ds(r0 + halo_rows, FILL_RB), :] = zx_ref[pl.ds(r0, FILL_RB), :].astype(F32)
        return carry

    lax.fori_loop(0, TM // FILL_RB, fill, 0)
    _fill_time_halo(pad_ref, RNN_HALO, is_ctx)

    _expand_rows(wexp_ref, 0, cw_ref[0])
    _expand_rows(wexp_ref, RNN_CONV * SUBLANES, cb_ref[0])

    def emit_xc(row0, lb, block):
        xc_ref[pl.ds(row0, CONV_RB), lb * LANES:(lb + 1) * LANES] = block

    _dwconv(pad_ref, RNN_HALO, -2, RNN_CONV, wexp_ref, emit_xc)

    neg_lam = -lam_ref[0]
    softplus = jnp.maximum(neg_lam, 0.0) + jnp.log1p(jnp.exp(-jnp.abs(neg_lam)))
    coef = (-0.5 * C_RG * np.log2(np.e)) * softplus
    a_refs = (af_ref, ab_ref)
    u_refs = (uf_ref, ub_ref)
    n_col = D_RNN // MXU_DIM

    def gate_body(rb, carry):
        rows = pl.ds(pl.multiple_of(rb * GATE_RB, GATE_RB), GATE_RB)
        for j in range(n_col):
            lanes = slice(j * MXU_DIM, (j + 1) * MXU_DIM)
            xc = xc_ref[rows, lanes]
            xcb = xc.astype(BF16)
            half_xc = 0.5 * xc
            for d in range(2):
                base = d * 2 * n_col
                g_r = jnp.dot(xcb, rgw_ref[0, base + j], preferred_element_type=F32)
                g_i = jnp.dot(xcb, rgw_ref[0, base + n_col + j], preferred_element_type=F32)
                t_r = jnp.tanh(g_r + rgb_ref[0, 2 * d:2 * d + 1, lanes])
                t_i = jnp.tanh(g_i + rgb_ref[0, 2 * d + 1:2 * d + 2, lanes])
                cf = coef[d:d + 1, lanes]
                a = jnp.exp2(cf * t_r + cf)
                y = 1.0 - a * a
                root = jnp.where(y == 0.0, 0.0, y * lax.rsqrt(y))
                a_refs[d][rows, lanes] = a
                u_refs[d][rows, lanes] = root * ((t_i + 1.0) * half_xc)
        return carry

    lax.fori_loop(0, TM // GATE_RB, gate_body, 0)

    def group(t):
        return pl.ds(pl.multiple_of(t * SUBLANES, SUBLANES), SUBLANES)

    def totals(t, carry):
        a_f, b_f, a_b, b_b = carry
        rf, rb = group(t), group(STEPS - 1 - t)
        a = af_ref[rf, :]
        b_f = a * b_f + uf_ref[rf, :]
        a_f = a * a_f
        a = ab_ref[rb, :]
        b_b = a * b_b + ub_ref[rb, :]
        a_b = a * a_b
        return a_f, b_f, a_b, b_b

    ones = jnp.ones((SUBLANES, D_RNN), F32)
    zeros = jnp.zeros((SUBLANES, D_RNN), F32)
    a_f, b_f, a_b, b_b = lax.fori_loop(0, STEPS, totals, (ones, zeros, ones, zeros), unroll=2)

    n_seq = TM // SEQ
    h0_f = [h0_ref[0, 0, 2 * s:2 * s + 1, :] for s in range(n_seq)]
    h0_b = [h0_ref[0, 0, 2 * s + 1:2 * s + 2, :] for s in range(n_seq)]
    hin_f, hout_f = _chain_chunks(a_f, b_f, h0_f, is_ctx, False)
    hin_b, hout_b = _chain_chunks(a_b, b_b, h0_b, is_ctx, True)

    per_ctx_seq = SEQ // CHUNK
    none = jnp.zeros((1, D_RNN), F32)
    for s in range(n_seq):
        last_f = hout_f[s * per_ctx_seq + per_ctx_seq - 1]
        first_b = hout_b[s * per_ctx_seq]
        st_ref[0, 0, 2 * s:2 * s + 1, :] = jnp.where(is_ctx, last_f, hout_f[SUBLANES - 1] if s == 0 else none)
        st_ref[0, 0, 2 * s + 1:2 * s + 2, :] = jnp.where(is_ctx, first_b, hout_b[0] if s == 0 else none)

    def backward(t, h):
        rb = group(STEPS - 1 - t)
        h = ab_ref[rb, :] * h + ub_ref[rb, :]
        ub_ref[rb, :] = h
        return h

    lax.fori_loop(0, STEPS, backward, hin_b, unroll=2)

    def forward(tt, h):
        r0, r1 = group(2 * tt), group(2 * tt + 1)
        h0 = af_ref[r0, :] * h + uf_ref[r0, :]
        h1 = af_ref[r1, :] * h0 + uf_ref[r1, :]
        rows = pl.ds(pl.multiple_of(tt * 2 * SUBLANES, 2 * SUBLANES), 2 * SUBLANES)
        both = jnp.concatenate([h0, h1], axis=0) + ub_ref[rows, :]
        ya_ref[rows, :] = (both * jax.nn.gelu(zg_ref[rows, :].astype(F32))).astype(ya_ref.dtype)
        return h1

    lax.fori_loop(0, STEPS // 2, forward, hin_f, unroll=2)


def _rnn(l, z, rnn_conv_w, rnn_conv_b3, rgw, rg_b4, rg_lambda, h0_all):
    n_rgw = 2 * 2 * (D_RNN // MXU_DIM)
    return _layer_call(
        _rnn_body,
        grid=(N_TILES,),
        in_specs=[pl.BlockSpec((TM, D_RNN), lambda i, l: (i, 0)),
                  pl.BlockSpec((TM, D_RNN), lambda i, l: (i, 1)),
                  pl.BlockSpec((1, RNN_CONV, D_RNN), lambda i, l: (l[0], 0, 0)),
                  pl.BlockSpec((1, 1, D_RNN), lambda i, l: (l[0], 0, 0)),
                  pl.BlockSpec((1, n_rgw, MXU_DIM, MXU_DIM), lambda i, l: (l[0], 0, 0, 0)),
                  pl.BlockSpec((1, 4, D_RNN), lambda i, l: (l[0], 0, 0)),
                  pl.BlockSpec((1, 2, D_RNN), lambda i, l: (l[0], 0, 0)),
                  pl.BlockSpec((1, 1, SUBLANES, D_RNN), lambda i, l: (l[0], i, 0, 0))],
        out_specs=[pl.BlockSpec((TM, D_MODEL), lambda i, l: (i, 0)),
                   pl.BlockSpec((1, 1, SUBLANES, D_RNN), lambda i, l: (0, i, 0, 0))],
        out_shape=[jax.ShapeDtypeStruct((N_TOK, D_RNN), BF16),
                   jax.ShapeDtypeStruct((1, N_TILES, SUBLANES, D_RNN), F32)],
        scratch_shapes=[pltpu.VMEM((TM + 2 * RNN_HALO * SUBLANES, D_RNN), F32),
                        pltpu.VMEM(((RNN_CONV + 1) * SUBLANES, D_RNN), F32),
                        pltpu.VMEM((TM, D_RNN), F32),
                        pltpu.VMEM((TM, D_RNN), F32),
                        pltpu.VMEM((TM, D_RNN), F32),
                        pltpu.VMEM((TM, D_RNN), F32),
                        pltpu.VMEM((TM, D_RNN), F32)],
        name="rnn")(l, z, z, rnn_conv_w, rnn_conv_b3, rgw, rg_b4, rg_lambda, h0_all)


ACT_RB = 64
N_CHUNKS = TM // CHUNK


def _sgu_body(l_ref, z_ref, g_ref, w_ref, bt_ref, yb_ref, su_ref, sv_ref):
    gain = g_ref[0]

    def act_body(rb, carry):
        rows = pl.ds(pl.multiple_of(rb * ACT_RB, ACT_RB), ACT_RB)
        uv = jax.nn.gelu(z_ref[rows, :].astype(F32))
        su_ref[rows, :] = uv[:, :D_SG]
        sv = uv[:, D_SG:]
        mu = jnp.mean(sv, axis=-1, keepdims=True)
        svc = sv - mu
        y = svc * lax.rsqrt(jnp.mean(svc * svc, axis=-1, keepdims=True) + EPS) * gain
        sv_ref[rows, :] = y.astype(sv_ref.dtype)
        return carry

    lax.fori_loop(0, TM // ACT_RB, act_body, 0, unroll=2)

    for g in range(SG_GROUPS):
        lanes = slice(g * SG_GD, (g + 1) * SG_GD)
        rhs = jnp.concatenate([sv_ref[n * CHUNK:(n + 1) * CHUNK, lanes] for n in range(N_CHUNKS)],
                              axis=1)
        mixed = jnp.dot(w_ref[0, g].astype(BF16), rhs, preferred_element_type=F32)
        bias = bt_ref[0, :, g:g + 1]
        for n in range(N_CHUNKS):
            rows = slice(n * CHUNK, (n + 1) * CHUNK)
            m = mixed[:, n * SG_GD:(n + 1) * SG_GD] + bias
            yb_ref[rows, lanes] = (su_ref[rows, lanes] * m).astype(yb_ref.dtype)


def _sgu(l, z, sg_norm_g3, sg_w, sg_bt):
    return _layer_call(
        _sgu_body,
        grid=(N_TILES,),
        in_specs=[pl.BlockSpec((TM, 2 * D_SG), lambda i, l: (i, 1)),
                  pl.BlockSpec((1, 1, D_SG), lambda i, l: (l[0], 0, 0)),
                  pl.BlockSpec((1, SG_GROUPS, CHUNK, CHUNK), lambda i, l: (l[0], 0, 0, 0)),
                  pl.BlockSpec((1, CHUNK, SG_GROUPS), lambda i, l: (l[0], 0, 0))],
        out_specs=pl.BlockSpec((TM, D_SG), lambda i, l: (i, 0)),
        out_shape=jax.ShapeDtypeStruct((N_TOK, D_SG), BF16),
        scratch_shapes=[pltpu.VMEM((TM, D_SG), F32), pltpu.VMEM((TM, D_SG), BF16)],
        name="sgu")(l, z, sg_norm_g3, sg_w, sg_bt)


CF_HALO = CF_CONV // 2


def _conformer_body(l_ref, z_ref, cw_ref, cb_ref, g_ref, b_ref, yc_ref, pad_ref, wexp_ref, blk_ref):
    i = pl.program_id(0)
    halo_rows = CF_HALO * SUBLANES

    def glu(rb, carry):
        r0 = pl.multiple_of(rb * FILL_RB, FILL_RB)
        zz = z_ref[pl.ds(r0, FILL_RB), :].astype(F32)
        pad_ref[pl.ds(r0 + halo_rows, FILL_RB), :] = zz[:, :D_CF] * _sigmoid(zz[:, D_CF:])
        return carry

    lax.fori_loop(0, TM // FILL_RB, glu, 0)
    _fill_time_halo(pad_ref, CF_HALO, i < CTX_TILES)

    _expand_rows(wexp_ref, 0, cw_ref[0])
    _expand_rows(wexp_ref, CF_CONV * SUBLANES, cb_ref[0])

    def emit(row0, lb, block):
        blk_ref[:, lb * LANES:(lb + 1) * LANES] = block

    def post(row0):
        y = blk_ref[...]
        mu = jnp.mean(y, axis=-1, keepdims=True)
        yc = y - mu
        ln = yc * lax.rsqrt(jnp.mean(yc * yc, axis=-1, keepdims=True) + EPS) * g_ref[0] + b_ref[0]
        yc_ref[pl.ds(row0, CONV_RB), :] = (ln * _sigmoid(ln)).astype(yc_ref.dtype)

    _dwconv(pad_ref, CF_HALO, -CF_HALO, CF_CONV, wexp_ref, emit, post)


def _conformer(l, z, cf_conv_w, cf_conv_b3, cf_ln_g3, cf_ln_b3):
    return _layer_call(
        _conformer_body,
        grid=(N_TILES,),
        in_specs=[pl.BlockSpec((TM, 2 * D_CF), lambda i, l: (i, 2)),
                  pl.BlockSpec((1, CF_CONV, D_CF), lambda i, l: (l[0], 0, 0)),
                  pl.BlockSpec((1, 1, D_CF), lambda i, l: (l[0], 0, 0)),
                  pl.BlockSpec((1, 1, D_CF), lambda i, l: (l[0], 0, 0)),
                  pl.BlockSpec((1, 1, D_CF), lambda i, l: (l[0], 0, 0))],
        out_specs=pl.BlockSpec((TM, D_CF), lambda i, l: (i, 0)),
        out_shape=jax.ShapeDtypeStruct((N_TOK, D_CF), BF16),
        scratch_shapes=[pltpu.VMEM((TM + 2 * CF_HALO * SUBLANES, D_CF), F32),
                        pltpu.VMEM(((CF_CONV + 1) * SUBLANES, D_CF), F32),
                        pltpu.VMEM((CONV_RB, D_CF), F32)],
        name="conformer")(l, z, cf_conv_w, cf_conv_b3, cf_ln_g3, cf_ln_b3)


def _merge_body(l_ref, x_ref, ya_ref, yb_ref, yc_ref, zg_ref, mod_ref, gn_ref, wb_ref, wo_ref,
                o_ref):
    i = pl.program_id(0)
    merged = None
    for p, y_ref in enumerate((ya_ref, yb_ref, yc_ref)):
        gate = _sigmoid(zg_ref[:, p * D_MODEL:(p + 1) * D_MODEL].astype(F32))
        term = gate * jnp.dot(y_ref[...], wb_ref[0, p], preferred_element_type=F32)
        merged = term if merged is None else merged + term
    out = jnp.dot(merged.astype(BF16), wo_ref[0], preferred_element_type=F32)
    m = mod_ref[0, _mod_row_of_tile(i, TM_MERGE)]
    gate1 = m[:, 2 * D_MODEL:3 * D_MODEL]
    o_ref[...] = x_ref[...] + gate1 * _rms_rows(out, gn_ref[0, 1:2, :])


def _merge(l, x, ya, yb, yc, z, mod4, g_norm, wb_bf16, wo_bf16):
    tile = lambda i, l: (i, 0)
    return _layer_call(
        _merge_body,
        grid=(N_TOK // TM_MERGE,),
        in_specs=[pl.BlockSpec((TM_MERGE, D_MODEL), tile),
                  pl.BlockSpec((TM_MERGE, D_MODEL), tile),
                  pl.BlockSpec((TM_MERGE, D_MODEL), tile),
                  pl.BlockSpec((TM_MERGE, D_MODEL), tile),
                  pl.BlockSpec((TM_MERGE, 3 * D_MODEL), lambda i, l: (i, 2)),
                  pl.BlockSpec((1, MOD_ROWS, 1, 6 * D_MODEL), lambda i, l: (l[0], 0, 0, 0)),
                  pl.BlockSpec((1, 4, D_MODEL), lambda i, l: (l[0], 0, 0)),
                  pl.BlockSpec((1, 3, D_MODEL, D_MODEL), lambda i, l: (l[0], 0, 0, 0)),
                  pl.BlockSpec((1, D_MODEL, D_MODEL), lambda i, l: (l[0], 0, 0))],
        out_specs=pl.BlockSpec((TM_MERGE, D_MODEL), tile),
        out_shape=jax.ShapeDtypeStruct((N_TOK, D_MODEL), F32),
        scratch_shapes=[],
        name="merge")(l, x, ya, yb, yc, z, mod4, g_norm, wb_bf16, wo_bf16)


FFN_HALO = 1
N_FF_STEPS = D_FF // FF_CW
FFN_BLOCKS = TM // CONV_RB
FFN_SPLIT = (TM // 2 - FFN_HALO * SUBLANES) // CONV_RB


def _ffn_body(l_ref, x_ref, mod_ref, gn_ref, wg_ref, wv_ref, cwg_ref, cwv_ref, cbg_ref, cbv_ref,
              wd_ref, o_ref, h_ref, acc_ref, padg_ref, padv_ref, act_ref, wgb_ref, wvb_ref, wdb_ref,
              wexp_ref):
    i = pl.program_id(0)
    c = pl.program_id(1)
    m = mod_ref[0, _mod_row_of_tile(i, TM)]
    halo_rows = FFN_HALO * SUBLANES

    @pl.when(c == 0)
    def _():
        _modulated_norm_to(h_ref, x_ref, gn_ref[0, 2:3, :],
                           m[:, 4 * D_MODEL:5 * D_MODEL], m[:, 3 * D_MODEL:4 * D_MODEL])
        acc_ref[...] = jnp.zeros_like(acc_ref)

    wgb_ref[...] = wg_ref[0].astype(BF16)
    wvb_ref[...] = wv_ref[0].astype(BF16)
    wdb_ref[...] = wd_ref[0].astype(BF16)

    def up(r0, r1):
        h = h_ref[r0:r1, :]
        padg_ref[halo_rows + r0:halo_rows + r1, :] = jnp.dot(h, wgb_ref[...], preferred_element_type=F32)
        padv_ref[halo_rows + r0:halo_rows + r1, :] = jnp.dot(h, wvb_ref[...], preferred_element_type=F32)

    n_wrows = (FFN_CONV + 1) * SUBLANES
    for s, (cw_ref, cb_ref) in enumerate(((cwg_ref, cbg_ref), (cwv_ref, cbv_ref))):
        _expand_rows(wexp_ref, s * n_wrows, cw_ref[0])
        _expand_rows(wexp_ref, s * n_wrows + FFN_CONV * SUBLANES, cb_ref[0])

    def conv_act(rb):
        base = rb * CONV_RB
        for lb in range(FF_CW // LANES):
            lanes = slice(lb * LANES, (lb + 1) * LANES)
            outs = []
            for s, pad_ref in enumerate((padg_ref, padv_ref)):
                outs.append(_conv_block(pad_ref, base, lanes, wexp_ref, s * n_wrows, FFN_CONV))
            act_ref[base:base + CONV_RB, lanes] = (jax.nn.gelu(outs[0]) * outs[1]).astype(act_ref.dtype)

    def down(r0, r1):
        acc_ref[r0:r1, :] += jnp.dot(act_ref[r0:r1, :], wdb_ref[...], preferred_element_type=F32)

    up(0, TM // 2)
    up(TM // 2, TM)
    is_ctx = i < CTX_TILES
    _fill_time_halo(padg_ref, FFN_HALO, is_ctx)
    _fill_time_halo(padv_ref, FFN_HALO, is_ctx)
    for rb in range(1, FFN_SPLIT):
        conv_act(rb)
    conv_act(0)
    down(0, FFN_SPLIT * CONV_RB)
    for rb in range(FFN_SPLIT, FFN_BLOCKS):
        conv_act(rb)
    down(FFN_SPLIT * CONV_RB, TM)

    @pl.when(c == N_FF_STEPS - 1)
    def _():
        gate2 = m[:, 5 * D_MODEL:6 * D_MODEL]
        g3 = gn_ref[0, 3:4, :]

        def body(rb, carry):
            rows = pl.ds(pl.multiple_of(rb * NORM_RB, NORM_RB), NORM_RB)
            o_ref[rows, :] = x_ref[rows, :] + gate2 * _rms_rows(acc_ref[rows, :], g3)
            return carry
        lax.fori_loop(0, TM // NORM_RB, body, 0, unroll=2)


def _ffn(l, x, mod4, g_norm, ffn_up, ffn_conv_w, ffn_conv_b3, ffn_down):
    return _layer_call(
        _ffn_body,
        grid=(N_TILES, N_FF_STEPS),
        in_specs=[pl.BlockSpec((TM, D_MODEL), lambda i, c, l: (i, 0)),
                  pl.BlockSpec((1, MOD_ROWS, 1, 6 * D_MODEL), lambda i, c, l: (l[0], 0, 0, 0)),
                  pl.BlockSpec((1, 4, D_MODEL), lambda i, c, l: (l[0], 0, 0)),
                  pl.BlockSpec((1, D_MODEL, FF_CW), lambda i, c, l: (l[0], 0, c)),
                  pl.BlockSpec((1, D_MODEL, FF_CW), lambda i, c, l: (l[0], 0, N_FF_STEPS + c)),
                  pl.BlockSpec((1, FFN_CONV, FF_CW), lambda i, c, l: (l[0], 0, c)),
                  pl.BlockSpec((1, FFN_CONV, FF_CW), lambda i, c, l: (l[0], 0, N_FF_STEPS + c)),
                  pl.BlockSpec((1, 1, FF_CW), lambda i, c, l: (l[0], 0, c)),
                  pl.BlockSpec((1, 1, FF_CW), lambda i, c, l: (l[0], 0, N_FF_STEPS + c)),
                  pl.BlockSpec((1, FF_CW, D_MODEL), lambda i, c, l: (l[0], c, 0))],
        out_specs=pl.BlockSpec((TM, D_MODEL), lambda i, c, l: (i, 0)),
        out_shape=jax.ShapeDtypeStruct((N_TOK, D_MODEL), F32),
        scratch_shapes=[pltpu.VMEM((TM, D_MODEL), BF16),
                        pltpu.VMEM((TM, D_MODEL), F32),
                        pltpu.VMEM((TM + 2 * FFN_HALO * SUBLANES, FF_CW), F32),
                        pltpu.VMEM((TM + 2 * FFN_HALO * SUBLANES, FF_CW), F32),
                        pltpu.VMEM((TM, FF_CW), BF16),
                        pltpu.VMEM((D_MODEL, FF_CW), BF16),
                        pltpu.VMEM((D_MODEL, FF_CW), BF16),
                        pltpu.VMEM((FF_CW, D_MODEL), BF16),
                        pltpu.VMEM((2 * (FFN_CONV + 1) * SUBLANES, FF_CW), F32)],
        name="ffn")(l, x, mod4, g_norm, ffn_up, ffn_up, ffn_conv_w, ffn_conv_w,
                    ffn_conv_b3, ffn_conv_b3, ffn_down)


def _pos_table():
    t = jnp.arange(DEC_SEQ)
    r = (t // GRID_W).astype(F32)
    col = (t % GRID_W).astype(F32)
    q = D_MODEL // 4
    omega = 1.0 / (POS_BASE ** (jnp.arange(q, dtype=F32) / q))

    def emb(p):
        ang = p[:, None] * omega[None, :]
        return jnp.concatenate([jnp.sin(ang), jnp.cos(ang)], axis=-1)
    return jnp.concatenate([emb(r), emb(col)], axis=-1).astype(F32)


def _block_diag_gates(rg_w):
    per_tile = MXU_DIM // BS_RNN
    n_col = H_RNN // per_tile
    w = rg_w.reshape(DEPTH, 2, 2, n_col, per_tile, BS_RNN, BS_RNN)
    eye = jnp.eye(per_tile, dtype=rg_w.dtype)
    tiles = jnp.einsum('ldkcaij,ab->ldkcaibj', w, eye)
    return tiles.reshape(DEPTH, 2 * 2 * n_col, MXU_DIM, MXU_DIM).astype(BF16)


def kernel(x_prompt, x_sample, state_rglru, c, c_ctx, w_mod, b_mod, g_norm, w_in, b_in,
           rnn_conv_w, rnn_conv_b, rg_w, rg_b, rg_lambda, sg_norm_g, sg_w, sg_b,
           cf_conv_w, cf_conv_b, cf_ln_g, cf_ln_b, w_branch, w_out, ffn_up, ffn_conv_w,
           ffn_conv_b, ffn_down):
    x = _interleave(_prep(x_prompt.reshape(BATCH * SEQ, D_MODEL),
                          x_sample.reshape(DEC_BATCH * DEC_SEQ, D_MODEL), _pos_table()))

    cond = jnp.zeros((MOD_ROWS, D_MODEL), F32).at[:DEC_BATCH].set(c).at[CTX_MOD_ROW].set(c_ctx)
    mod4 = _mod(cond, w_mod, b_mod).reshape(DEPTH, MOD_ROWS, 1, 6 * D_MODEL)

    h0_all = jnp.zeros((DEPTH, N_TILES, SUBLANES, D_RNN), F32)
    h0_all = h0_all.at[:, CTX_TILES:, 0:2, :].set(jnp.transpose(state_rglru.astype(F32), (1, 0, 2, 3)))

    rgw = _block_diag_gates(0.5 * rg_w)
    wb_bf16 = w_branch.astype(BF16)
    wo_bf16 = w_out.astype(BF16)
    b_in3 = b_in.reshape(DEPTH, 1, N_IN)
    rnn_conv_b3 = rnn_conv_b.reshape(DEPTH, 1, D_RNN)
    rg_b4 = 0.5 * rg_b.reshape(DEPTH, 4, D_RNN)
    sg_norm_g3 = sg_norm_g.reshape(DEPTH, 1, D_SG)
    sg_bt = jnp.transpose(sg_b, (0, 2, 1))
    cf_conv_b3 = cf_conv_b.reshape(DEPTH, 1, D_CF)
    cf_ln_g3 = cf_ln_g.reshape(DEPTH, 1, D_CF)
    cf_ln_b3 = cf_ln_b.reshape(DEPTH, 1, D_CF)
    ffn_conv_b3 = ffn_conv_b.reshape(DEPTH, 1, 2 * D_FF)

    states = []
    for layer in range(DEPTH):
        l = jnp.full((1,), layer, jnp.int32)
        z = _inproj(l, x, mod4, g_norm, w_in, b_in3)
        ya, st = _rnn(l, z, rnn_conv_w, rnn_conv_b3, rgw, rg_b4, rg_lambda, h0_all)
        yb = _interleave(_sgu(l, z, sg_norm_g3, sg_w, sg_bt))
        yc = _conformer(l, z, cf_conv_w, cf_conv_b3, cf_ln_g3, cf_ln_b3)
        x = _merge(l, x, ya, yb, yc, z, mod4, g_norm, wb_bf16, wo_bf16)
        x = _ffn(l, x, mod4, g_norm, ffn_up, ffn_conv_w, ffn_conv_b3, ffn_down)
        states.append(st[0, :CTX_TILES])
    states = jnp.stack(states)

    x = _deinterleave(x)
    y_prompt = x[:BATCH * SEQ].reshape(BATCH, SEQ, D_MODEL)
    y_sample = x[BATCH * SEQ:].reshape(DEC_BATCH, DEC_SEQ, D_MODEL)
    new_state = states.reshape(DEPTH, BATCH, 2, D_RNN).transpose(1, 0, 2, 3)
    return (y_prompt, y_sample, new_state)
```

```python
import functools

import jax
import jax.numpy as jnp
import numpy as np
from jax import lax
from jax.experimental import pallas as pl
from jax.experimental.pallas import tpu as pltpu

F32 = jnp.float32
BF16 = jnp.bfloat16

D_MODEL = 1024
BATCH = 16
SEQ = 256
DEPTH = 4
DEC_BATCH = 4
DEC_SEQ = 1024
GRID_W = 64
D_RNN = 1024
H_RNN = 16
BS_RNN = D_RNN // H_RNN
RNN_CONV = 4
C_RG = 8.0
D_SG = 1024
SG_GROUPS = 8
SG_GD = D_SG // SG_GROUPS
CHUNK = 128
D_CF = 1024
CF_CONV = 31
D_FF = 4096
FFN_CONV = 3
N_IN = 2 * D_RNN + 2 * D_SG + 2 * D_CF + 3 * D_MODEL
EPS = 1e-6
POS_BASE = 10000.0

LANES = 128
SUBLANES = 8
MXU_DIM = 256

N_TOK = BATCH * SEQ + DEC_BATCH * DEC_SEQ
TM = 1024
N_TILES = N_TOK // TM
CTX_TILES = BATCH * SEQ // TM
MOD_ROWS = 8
CTX_MOD_ROW = DEC_BATCH
CONV_RB = 64
FF_CW = 512
TM_MERGE = 512
VMEM_LIMIT = 56 * 1024 * 1024


def _sigmoid(x):
    return 0.5 * (jnp.tanh(0.5 * x) + 1.0)


def _rms_rows(x, g):
    return x * lax.rsqrt(jnp.mean(x * x, axis=-1, keepdims=True) + EPS) * g


def _mod_row_of_tile(i, tile_rows):
    tiles_per_latent = DEC_SEQ // tile_rows
    ctx_tiles = BATCH * SEQ // tile_rows
    lat = jnp.maximum(i - ctx_tiles, 0) // tiles_per_latent
    return jnp.where(i < ctx_tiles, CTX_MOD_ROW, lat)


def _params(n_grid):
    return pltpu.CompilerParams(dimension_semantics=("arbitrary",) * n_grid,
                                vmem_limit_bytes=VMEM_LIMIT)


def _layer_call(body, grid, in_specs, out_specs, out_shape, scratch_shapes, name):
    return pl.pallas_call(
        body,
        grid_spec=pltpu.PrefetchScalarGridSpec(
            num_scalar_prefetch=1, grid=grid, in_specs=in_specs, out_specs=out_specs,
            scratch_shapes=scratch_shapes),
        out_shape=out_shape,
        compiler_params=_params(len(grid)),
        name=name)


STEPS = CHUNK
HALO_LW = 256


def _interleave(x):
    return x.reshape(N_TILES, SUBLANES, STEPS, -1).transpose(0, 2, 1, 3).reshape(x.shape)


def _deinterleave(x):
    return x.reshape(N_TILES, STEPS, SUBLANES, -1).transpose(0, 2, 1, 3).reshape(x.shape)


def _chunk_edge_masks(is_ctx, shape):
    c = lax.broadcasted_iota(jnp.int32, shape, 0) & (SUBLANES - 1)
    m = jnp.where(is_ctx, SEQ // CHUNK - 1, DEC_SEQ // CHUNK - 1)
    cm = c & m
    return cm == 0, cm == m


def _fill_time_halo(pad_ref, halo, is_ctx, width=None):
    n = halo * SUBLANES
    for lo in range(0, pad_ref.shape[1] if width is None else width, HALO_LW):
        lanes = slice(lo, lo + HALO_LW)
        first, last = _chunk_edge_masks(is_ctx, (n, HALO_LW))
        tail = pad_ref[TM:TM + n, lanes]
        head = pad_ref[n:2 * n, lanes]
        pad_ref[0:n, lanes] = jnp.where(first, 0.0, pltpu.roll(tail, 1, 0))
        pad_ref[n + TM:2 * n + TM, lanes] = jnp.where(last, 0.0, pltpu.roll(head, n - 1, 0))


def _expand_rows(dst_ref, row0, rows):
    for k in range(rows.shape[0]):
        lo = row0 + k * SUBLANES
        dst_ref[lo:lo + SUBLANES, :] = jnp.broadcast_to(rows[k:k + 1, :], (SUBLANES, rows.shape[1]))


def _conv_block(pad_ref, row0, lanes, wexp_ref, w0, n_taps):
    def wrow(k):
        return wexp_ref[w0 + k * SUBLANES:w0 + (k + 1) * SUBLANES, lanes]

    groups = CONV_RB // SUBLANES
    if n_taps <= SUBLANES:
        taps = [wrow(k) for k in range(n_taps)]
        accs = [wrow(n_taps)] * groups
        for j in range(n_taps + groups - 1):
            xj = pad_ref[pl.ds(row0 + j * SUBLANES, SUBLANES), lanes]
            for r in range(groups):
                if 0 <= j - r < n_taps:
                    accs[r] = accs[r] + xj * taps[j - r]
        return jnp.concatenate(accs, axis=0)
    acc = jnp.broadcast_to(wrow(n_taps), (groups, SUBLANES, LANES))
    for k in range(n_taps):
        x = pad_ref[pl.ds(row0 + k * SUBLANES, CONV_RB), lanes]
        acc = acc + x.reshape(groups, SUBLANES, LANES) * wrow(k)
    return acc.reshape(CONV_RB, LANES)


def _dwconv(pad_ref, halo, first_offset, n_taps, wexp_ref, emit, post=None):
    def body(b, carry):
        base = pl.multiple_of(b * CONV_RB, CONV_RB)
        for lb in range(pad_ref.shape[1] // LANES):
            lanes = slice(lb * LANES, (lb + 1) * LANES)
            emit(base, lb, _conv_block(pad_ref, base + (halo + first_offset) * SUBLANES, lanes,
                                       wexp_ref, 0, n_taps))
        if post is not None:
            post(base)
        return carry

    lax.fori_loop(0, TM // CONV_RB, body, 0)


def _prep_body(xp_ref, xs_ref, pos_ref, o_ref):
    i = pl.program_id(0)

    @pl.when(i < CTX_TILES)
    def _():
        o_ref[...] = xp_ref[...]

    @pl.when(i >= CTX_TILES)
    def _():
        o_ref[...] = xs_ref[...] + pos_ref[...]


def _prep(xp, xs, pos):
    return pl.pallas_call(
        _prep_body,
        grid=(N_TILES,),
        in_specs=[pl.BlockSpec((TM, D_MODEL), lambda i: (jnp.minimum(i, CTX_TILES - 1), 0)),
                  pl.BlockSpec((TM, D_MODEL), lambda i: (jnp.maximum(i - CTX_TILES, 0), 0)),
                  pl.BlockSpec((DEC_SEQ, D_MODEL), lambda i: (0, 0))],
        out_specs=pl.BlockSpec((TM, D_MODEL), lambda i: (i, 0)),
        out_shape=jax.ShapeDtypeStruct((N_TOK, D_MODEL), F32),
        compiler_params=_params(1),
        name="prep")(xp, xs, pos)


MOD_TN = 3072


def _mod_body(cond_ref, w_ref, b_ref, o_ref):
    c = cond_ref[...]
    s = (c * _sigmoid(c)).astype(BF16)
    o_ref[0] = jnp.dot(s, w_ref[0].astype(BF16), preferred_element_type=F32) + b_ref[0]


def _mod(cond, w_mod, b_mod):
    n_mod = 6 * D_MODEL
    return pl.pallas_call(
        _mod_body,
        grid=(DEPTH, n_mod // MOD_TN),
        in_specs=[pl.BlockSpec((MOD_ROWS, D_MODEL), lambda l, j: (0, 0)),
                  pl.BlockSpec((1, D_MODEL, MOD_TN), lambda l, j: (l, 0, j)),
                  pl.BlockSpec((1, 1, MOD_TN), lambda l, j: (l, 0, j))],
        out_specs=pl.BlockSpec((1, MOD_ROWS, MOD_TN), lambda l, j: (l, 0, j)),
        out_shape=jax.ShapeDtypeStruct((DEPTH, MOD_ROWS, n_mod), F32),
        compiler_params=_params(2),
        name="mod")(cond, w_mod, b_mod.reshape(DEPTH, 1, n_mod))


IN_TN = 1024
NORM_RB = 64


def _modulated_norm_to(h_ref, x_ref, g, scale, shift):
    def body(rb, carry):
        rows = pl.ds(pl.multiple_of(rb * NORM_RB, NORM_RB), NORM_RB)
        h = _rms_rows(x_ref[rows, :], g) * (1.0 + scale) + shift
        h_ref[rows, :] = h.astype(h_ref.dtype)
        return carry
    lax.fori_loop(0, x_ref.shape[0] // NORM_RB, body, 0, unroll=2)


SG_COL = 2 * D_RNN // IN_TN
IN_TILES = 2
IN_ROWS = IN_TILES * TM


def _inproj_body(l_ref, x_ref, mod_ref, gn_ref, w_ref, b_ref, z_ref, h_ref, hn_ref, perm_ref):
    i = pl.program_id(0)
    j = pl.program_id(1)

    @pl.when(jnp.logical_and(i == 0, j == 0))
    def _():
        nat = lax.broadcasted_iota(jnp.int32, (TM, TM), 0)
        src = lax.broadcasted_iota(jnp.int32, (TM, TM), 1)
        chunk_shift = CHUNK.bit_length() - 1
        row_of_nat = ((nat & (CHUNK - 1)) * SUBLANES) | jnp.right_shift(nat, chunk_shift)
        perm_ref[...] = jnp.where(src == row_of_nat, 1.0, 0.0).astype(BF16)

    @pl.when(j == 0)
    def _():
        for t in range(IN_TILES):
            rows = pl.ds(t * TM, TM)
            m = mod_ref[0, _mod_row_of_tile(IN_TILES * i + t, TM)]
            _modulated_norm_to(h_ref.at[rows], x_ref.at[rows], gn_ref[0, 0:1, :],
                               m[:, D_MODEL:2 * D_MODEL], m[:, 0:D_MODEL])
            hn_ref[rows, :] = jnp.dot(perm_ref[...], h_ref[rows, :],
                                      preferred_element_type=F32).astype(BF16)

    natural = jnp.logical_and(j >= SG_COL, j < SG_COL + 2 * D_SG // IN_TN)

    @pl.when(natural)
    def _():
        acc = jnp.dot(hn_ref[...], w_ref[0].astype(BF16), preferred_element_type=F32)
        z_ref[...] = (acc + b_ref[0]).astype(z_ref.dtype)

    @pl.when(jnp.logical_not(natural))
    def _():
        acc = jnp.dot(h_ref[...], w_ref[0].astype(BF16), preferred_element_type=F32)
        z_ref[...] = (acc + b_ref[0]).astype(z_ref.dtype)


def _inproj(l, x, mod4, g_norm, w_in, b_in3):
    return _layer_call(
        _inproj_body,
        grid=(N_TOK // IN_ROWS, N_IN // IN_TN),
        in_specs=[pl.BlockSpec((IN_ROWS, D_MODEL), lambda i, j, l: (i, 0)),
                  pl.BlockSpec((1, MOD_ROWS, 1, 6 * D_MODEL), lambda i, j, l: (l[0], 0, 0, 0)),
                  pl.BlockSpec((1, 4, D_MODEL), lambda i, j, l: (l[0], 0, 0)),
                  pl.BlockSpec((1, D_MODEL, IN_TN), lambda i, j, l: (l[0], 0, j)),
                  pl.BlockSpec((1, 1, IN_TN), lambda i, j, l: (l[0], 0, j))],
        out_specs=pl.BlockSpec((IN_ROWS, IN_TN), lambda i, j, l: (i, j)),
        out_shape=jax.ShapeDtypeStruct((N_TOK, N_IN), BF16),
        scratch_shapes=[pltpu.VMEM((IN_ROWS, D_MODEL), BF16),
                        pltpu.VMEM((IN_ROWS, D_MODEL), BF16),
                        pltpu.VMEM((TM, TM), BF16)],
        name="inproj")(l, x, mod4, g_norm, w_in, b_in3)


RNN_HALO = 2
GATE_RB = 512
FILL_RB = 64


def _rows_of(value):
    return [value[r:r + 1, :] for r in range(SUBLANES)]


def _chain_chunks(a_end, b_end, h0_rows, is_ctx, reverse):
    per_ctx_seq = SEQ // CHUNK
    a_rows, b_rows = _rows_of(a_end), _rows_of(b_end)
    h_in, h_out = [None] * SUBLANES, [None] * SUBLANES
    order = range(SUBLANES - 1, -1, -1) if reverse else range(SUBLANES)
    prev = None
    for c in order:
        starts_ctx_seq = (c % per_ctx_seq == per_ctx_seq - 1) if reverse else (c % per_ctx_seq == 0)
        if prev is None:
            h = jnp.where(is_ctx, h0_rows[c // per_ctx_seq], h0_rows[0])
        elif starts_ctx_seq:
            h = jnp.where(is_ctx, h0_rows[c // per_ctx_seq], prev)
        else:
            h = prev
        h_in[c] = h
        prev = a_rows[c] * h + b_rows[c]
        h_out[c] = prev
    return jnp.concatenate(h_in, axis=0), h_out


def _rnn_body(l_ref, zx_ref, zg_ref, cw_ref, cb_ref, rgw_ref, rgb_ref, lam_ref, h0_ref,
              ya_ref, st_ref, pad_ref, wexp_ref, xc_ref, af_ref, uf_ref, ab_ref, ub_ref):
    i = pl.program_id(0)
    is_ctx = i < CTX_TILES
    halo_rows = RNN_HALO * SUBLANES

    def fill(rb, carry):
        r0 = pl.multiple_of(rb * FILL_RB, FILL_RB)
        pad_ref[pl.ds(r0 + halo_rows, FILL_RB), :] = zx_ref[pl.ds(r0, FILL_RB), :].astype(F32)
        return carry

    lax.fori_loop(0, TM // FILL_RB, fill, 0)
    _fill_time_halo(pad_ref, RNN_HALO, is_ctx)

    _expand_rows(wexp_ref, 0, cw_ref[0])
    _expand_rows(wexp_ref, RNN_CONV * SUBLANES, cb_ref[0])

    def emit_xc(row0, lb, block):
        xc_ref[pl.ds(row0, CONV_RB), lb * LANES:(lb + 1) * LANES] = block

    _dwconv(pad_ref, RNN_HALO, -2, RNN_CONV, wexp_ref, emit_xc)

    neg_lam = -lam_ref[0]
    softplus = jnp.maximum(neg_lam, 0.0) + jnp.log1p(jnp.exp(-jnp.abs(neg_lam)))
    coef = (-0.5 * C_RG * np.log2(np.e)) * softplus
    a_refs = (af_ref, ab_ref)
    u_refs = (uf_ref, ub_ref)
    n_col = D_RNN // MXU_DIM

    def gate_body(rb, carry):
        rows = pl.ds(pl.multiple_of(rb * GATE_RB, GATE_RB), GATE_RB)
        for j in range(n_col):
            lanes = slice(j * MXU_DIM, (j + 1) * MXU_DIM)
            xc = xc_ref[rows, lanes]
            xcb = xc.astype(BF16)
            half_xc = 0.5 * xc
            for d in range(2):
                base = d * 2 * n_col
                g_r = jnp.dot(xcb, rgw_ref[0, base + j], preferred_element_type=F32)
                g_i = jnp.dot(xcb, rgw_ref[0, base + n_col + j], preferred_element_type=F32)
                t_r = jnp.tanh(g_r + rgb_ref[0, 2 * d:2 * d + 1, lanes])
                t_i = jnp.tanh(g_i + rgb_ref[0, 2 * d + 1:2 * d + 2, lanes])
                cf = coef[d:d + 1, lanes]
                a = jnp.exp2(cf * t_r + cf)
                y = 1.0 - a * a
                root = jnp.where(y == 0.0, 0.0, y * lax.rsqrt(y))
                a_refs[d][rows, lanes] = a
                u_refs[d][rows, lanes] = root * ((t_i + 1.0) * half_xc)
        return carry

    lax.fori_loop(0, TM // GATE_RB, gate_body, 0)

    def group(t):
        return pl.ds(pl.multiple_of(t * SUBLANES, SUBLANES), SUBLANES)

    def totals(t, carry):
        a_f, b_f, a_b, b_b = carry
        rf, rb = group(t), group(STEPS - 1 - t)
        a = af_ref[rf, :]
        b_f = a * b_f + uf_ref[rf, :]
        a_f = a * a_f
        a = ab_ref[rb, :]
        b_b = a * b_b + ub_ref[rb, :]
        a_b = a * a_b
        return a_f, b_f, a_b, b_b

    ones = jnp.ones((SUBLANES, D_RNN), F32)
    zeros = jnp.zeros((SUBLANES, D_RNN), F32)
    a_f, b_f, a_b, b_b = lax.fori_loop(0, STEPS, totals, (ones, zeros, ones, zeros), unroll=2)

    n_seq = TM // SEQ
    h0_f = [h0_ref[0, 0, 2 * s:2 * s + 1, :] for s in range(n_seq)]
    h0_b = [h0_ref[0, 0, 2 * s + 1:2 * s + 2, :] for s in range(n_seq)]
    hin_f, hout_f = _chain_chunks(a_f, b_f, h0_f, is_ctx, False)
    hin_b, hout_b = _chain_chunks(a_b, b_b, h0_b, is_ctx, True)

    per_ctx_seq = SEQ // CHUNK
    none = jnp.zeros((1, D_RNN), F32)
    for s in range(n_seq):
        last_f = hout_f[s * per_ctx_seq + per_ctx_seq - 1]
        first_b = hout_b[s * per_ctx_seq]
        st_ref[0, 0, 2 * s:2 * s + 1, :] = jnp.where(is_ctx, last_f, hout_f[SUBLANES - 1] if s == 0 else none)
        st_ref[0, 0, 2 * s + 1:2 * s + 2, :] = jnp.where(is_ctx, first_b, hout_b[0] if s == 0 else none)

    def backward(t, h):
        rb = group(STEPS - 1 - t)
        h = ab_ref[rb, :] * h + ub_ref[rb, :]
        ub_ref[rb, :] = h
        return h

    lax.fori_loop(0, STEPS, backward, hin_b, unroll=2)

    def forward(tt, h):
        r0, r1 = group(2 * tt), group(2 * tt + 1)
        h0 = af_ref[r0, :] * h + uf_ref[r0, :]
        h1 = af_ref[r1, :] * h0 + uf_ref[r1, :]
        rows = pl.ds(pl.multiple_of(tt * 2 * SUBLANES, 2 * SUBLANES), 2 * SUBLANES)
        both = jnp.concatenate([h0, h1], axis=0) + ub_ref[rows, :]
        ya_ref[rows, :] = (both * jax.nn.gelu(zg_ref[rows, :].astype(F32))).astype(ya_ref.dtype)
        return h1

    lax.fori_loop(0, STEPS // 2, forward, hin_f, unroll=2)


def _rnn(l, z, rnn_conv_w, rnn_conv_b3, rgw, rg_b4, rg_lambda, h0_all):
    n_rgw = 2 * 2 * (D_RNN // MXU_DIM)
    return _layer_call(
        _rnn_body,
        grid=(N_TILES,),
        in_specs=[pl.BlockSpec((TM, D_RNN), lambda i, l: (i, 0)),
                  pl.BlockSpec((TM, D_RNN), lambda i, l: (i, 1)),
                  pl.BlockSpec((1, RNN_CONV, D_RNN), lambda i, l: (l[0], 0, 0)),
                  pl.BlockSpec((1, 1, D_RNN), lambda i, l: (l[0], 0, 0)),
                  pl.BlockSpec((1, n_rgw, MXU_DIM, MXU_DIM), lambda i, l: (l[0], 0, 0, 0)),
                  pl.BlockSpec((1, 4, D_RNN), lambda i, l: (l[0], 0, 0)),
                  pl.BlockSpec((1, 2, D_RNN), lambda i, l: (l[0], 0, 0)),
                  pl.BlockSpec((1, 1, SUBLANES, D_RNN), lambda i, l: (l[0], i, 0, 0))],
        out_specs=[pl.BlockSpec((TM, D_MODEL), lambda i, l: (i, 0)),
                   pl.BlockSpec((1, 1, SUBLANES, D_RNN), lambda i, l: (0, i, 0, 0))],
        out_shape=[jax.ShapeDtypeStruct((N_TOK, D_RNN), BF16),
                   jax.ShapeDtypeStruct((1, N_TILES, SUBLANES, D_RNN), F32)],
        scratch_shapes=[pltpu.VMEM((TM + 2 * RNN_HALO * SUBLANES, D_RNN), F32),
                        pltpu.VMEM(((RNN_CONV + 1) * SUBLANES, D_RNN), F32),
                        pltpu.VMEM((TM, D_RNN), F32),
                        pltpu.VMEM((TM, D_RNN), F32),
                        pltpu.VMEM((TM, D_RNN), F32),
                        pltpu.VMEM((TM, D_RNN), F32),
                        pltpu.VMEM((TM, D_RNN), F32)],
        name="rnn")(l, z, z, rnn_conv_w, rnn_conv_b3, rgw, rg_b4, rg_lambda, h0_all)


ACT_RB = 64
N_CHUNKS = TM // CHUNK


def _sgu_body(l_ref, z_ref, g_ref, w_ref, bt_ref, yb_ref, su_ref, sv_ref):
    gain = g_ref[0]

    def act_body(rb, carry):
        rows = pl.ds(pl.multiple_of(rb * ACT_RB, ACT_RB), ACT_RB)
        uv = jax.nn.gelu(z_ref[rows, :].astype(F32))
        su_ref[rows, :] = uv[:, :D_SG]
        sv = uv[:, D_SG:]
        mu = jnp.mean(sv, axis=-1, keepdims=True)
        svc = sv - mu
        y = svc * lax.rsqrt(jnp.mean(svc * svc, axis=-1, keepdims=True) + EPS) * gain
        sv_ref[rows, :] = y.astype(sv_ref.dtype)
        return carry

    lax.fori_loop(0, TM // ACT_RB, act_body, 0, unroll=2)

    for g in range(SG_GROUPS):
        lanes = slice(g * SG_GD, (g + 1) * SG_GD)
        rhs = jnp.concatenate([sv_ref[n * CHUNK:(n + 1) * CHUNK, lanes] for n in range(N_CHUNKS)],
                              axis=1)
        mixed = jnp.dot(w_ref[0, g].astype(BF16), rhs, preferred_element_type=F32)
        bias = bt_ref[0, :, g:g + 1]
        for n in range(N_CHUNKS):
            rows = slice(n * CHUNK, (n + 1) * CHUNK)
            m = mixed[:, n * SG_GD:(n + 1) * SG_GD] + bias
            yb_ref[rows, lanes] = (su_ref[rows, lanes] * m).astype(yb_ref.dtype)


def _sgu(l, z, sg_norm_g3, sg_w, sg_bt):
    return _layer_call(
        _sgu_body,
        grid=(N_TILES,),
        in_specs=[pl.BlockSpec((TM, 2 * D_SG), lambda i, l: (i, 1)),
                  pl.BlockSpec((1, 1, D_SG), lambda i, l: (l[0], 0, 0)),
                  pl.BlockSpec((1, SG_GROUPS, CHUNK, CHUNK), lambda i, l: (l[0], 0, 0, 0)),
                  pl.BlockSpec((1, CHUNK, SG_GROUPS), lambda i, l: (l[0], 0, 0))],
        out_specs=pl.BlockSpec((TM, D_SG), lambda i, l: (i, 0)),
        out_shape=jax.ShapeDtypeStruct((N_TOK, D_SG), BF16),
        scratch_shapes=[pltpu.VMEM((TM, D_SG), F32), pltpu.VMEM((TM, D_SG), BF16)],
        name="sgu")(l, z, sg_norm_g3, sg_w, sg_bt)


CF_HALO = CF_CONV // 2


CF_SUMS = 4


def _conformer_body(l_ref, z_ref, cw_ref, cb_ref, g_ref, b_ref, yc_ref, pad_ref, wexp_ref, conv_ref):
    i = pl.program_id(0)
    halo_rows = CF_HALO * SUBLANES

    def glu(rb, carry):
        r0 = pl.multiple_of(rb * FILL_RB, FILL_RB)
        zz = z_ref[pl.ds(r0, FILL_RB), :].astype(F32)
        pad_ref[pl.ds(r0 + halo_rows, FILL_RB), 0:D_CF] = zz[:, :D_CF] * _sigmoid(zz[:, D_CF:])
        return carry

    lax.fori_loop(0, TM // FILL_RB, glu, 0)
    _fill_time_halo(pad_ref, CF_HALO, i < CTX_TILES, D_CF)

    _expand_rows(wexp_ref, 0, cw_ref[0])
    _expand_rows(wexp_ref, CF_CONV * SUBLANES, cb_ref[0])

    groups = CONV_RB // SUBLANES
    for lb in range(D_CF // LANES):
        lanes = slice(lb * LANES, (lb + 1) * LANES)
        weights = tuple(wexp_ref[k * SUBLANES:(k + 1) * SUBLANES, lanes] for k in range(CF_CONV + 1))

        def body(b, ws, lanes=lanes):
            base = pl.multiple_of(b * CONV_RB, CONV_RB)
            sums = [jnp.broadcast_to(ws[CF_CONV], (groups, SUBLANES, LANES))] + [None] * (CF_SUMS - 1)
            for k in range(CF_CONV):
                x = pad_ref[pl.ds(base + k * SUBLANES, CONV_RB), lanes]
                term = x.reshape(groups, SUBLANES, LANES) * ws[k]
                sums[k % CF_SUMS] = term if sums[k % CF_SUMS] is None else sums[k % CF_SUMS] + term
            acc = (sums[0] + sums[1]) + (sums[2] + sums[3])
            conv_ref[pl.ds(base, CONV_RB), lanes] = acc.reshape(CONV_RB, LANES)
            return ws

        lax.fori_loop(0, TM // CONV_RB, body, weights)

    def post(b, carry):
        rows = pl.ds(pl.multiple_of(b * CONV_RB, CONV_RB), CONV_RB)
        y = conv_ref[rows, :]
        mu = jnp.mean(y, axis=-1, keepdims=True)
        yc = y - mu
        ln = yc * lax.rsqrt(jnp.mean(yc * yc, axis=-1, keepdims=True) + EPS) * g_ref[0] + b_ref[0]
        yc_ref[rows, :] = (ln * _sigmoid(ln)).astype(yc_ref.dtype)
        return carry

    lax.fori_loop(0, TM // CONV_RB, post, 0, unroll=2)


def _conformer(l, z, cf_conv_w, cf_conv_b3, cf_ln_g3, cf_ln_b3):
    return _layer_call(
        _conformer_body,
        grid=(N_TILES,),
        in_specs=[pl.BlockSpec((TM, 2 * D_CF), lambda i, l: (i, 2)),
                  pl.BlockSpec((1, CF_CONV, D_CF), lambda i, l: (l[0], 0, 0)),
                  pl.BlockSpec((1, 1, D_CF), lambda i, l: (l[0], 0, 0)),
                  pl.BlockSpec((1, 1, D_CF), lambda i, l: (l[0], 0, 0)),
                  pl.BlockSpec((1, 1, D_CF), lambda i, l: (l[0], 0, 0))],
        out_specs=pl.BlockSpec((TM, D_CF), lambda i, l: (i, 0)),
        out_shape=jax.ShapeDtypeStruct((N_TOK, D_CF), BF16),
        scratch_shapes=[pltpu.VMEM((TM + 2 * CF_HALO * SUBLANES, D_CF + LANES), F32),
                        pltpu.VMEM(((CF_CONV + 1) * SUBLANES, D_CF), F32),
                        pltpu.VMEM((TM, D_CF), F32)],
        name="conformer")(l, z, cf_conv_w, cf_conv_b3, cf_ln_g3, cf_ln_b3)


def _merge_body(l_ref, x_ref, ya_ref, yb_ref, yc_ref, zg_ref, mod_ref, gn_ref, wb_ref, wo_ref,
                o_ref):
    i = pl.program_id(0)
    merged = None
    for p, y_ref in enumerate((ya_ref, yb_ref, yc_ref)):
        gate = _sigmoid(zg_ref[:, p * D_MODEL:(p + 1) * D_MODEL].astype(F32))
        term = gate * jnp.dot(y_ref[...], wb_ref[0, p], preferred_element_type=F32)
        merged = term if merged is None else merged + term
    out = jnp.dot(merged.astype(BF16), wo_ref[0], preferred_element_type=F32)
    m = mod_ref[0, _mod_row_of_tile(i, TM_MERGE)]
    gate1 = m[:, 2 * D_MODEL:3 * D_MODEL]
    o_ref[...] = x_ref[...] + gate1 * _rms_rows(out, gn_ref[0, 1:2, :])


def _merge(l, x, ya, yb, yc, z, mod4, g_norm, wb_bf16, wo_bf16):
    tile = lambda i, l: (i, 0)
    return _layer_call(
        _merge_body,
        grid=(N_TOK // TM_MERGE,),
        in_specs=[pl.BlockSpec((TM_MERGE, D_MODEL), tile),
                  pl.BlockSpec((TM_MERGE, D_MODEL), tile),
                  pl.BlockSpec((TM_MERGE, D_MODEL), tile),
                  pl.BlockSpec((TM_MERGE, D_MODEL), tile),
                  pl.BlockSpec((TM_MERGE, 3 * D_MODEL), lambda i, l: (i, 2)),
                  pl.BlockSpec((1, MOD_ROWS, 1, 6 * D_MODEL), lambda i, l: (l[0], 0, 0, 0)),
                  pl.BlockSpec((1, 4, D_MODEL), lambda i, l: (l[0], 0, 0)),
                  pl.BlockSpec((1, 3, D_MODEL, D_MODEL), lambda i, l: (l[0], 0, 0, 0)),
                  pl.BlockSpec((1, D_MODEL, D_MODEL), lambda i, l: (l[0], 0, 0))],
        out_specs=pl.BlockSpec((TM_MERGE, D_MODEL), tile),
        out_shape=jax.ShapeDtypeStruct((N_TOK, D_MODEL), F32),
        scratch_shapes=[],
        name="merge")(l, x, ya, yb, yc, z, mod4, g_norm, wb_bf16, wo_bf16)


FFN_HALO = 1
N_FF_STEPS = D_FF // FF_CW
FFN_BLOCKS = TM // CONV_RB
FFN_SPLIT = (TM // 2 - FFN_HALO * SUBLANES) // CONV_RB


def _ffn_body(l_ref, x_ref, mod_ref, gn_ref, wg_ref, wv_ref, cwg_ref, cwv_ref, cbg_ref, cbv_ref,
              wd_ref, o_ref, h_ref, acc_ref, padg_ref, padv_ref, act_ref, wgb_ref, wvb_ref, wdb_ref,
              wexp_ref):
    i = pl.program_id(0)
    c = pl.program_id(1)
    m = mod_ref[0, _mod_row_of_tile(i, TM)]
    halo_rows = FFN_HALO * SUBLANES

    @pl.when(c == 0)
    def _():
        _modulated_norm_to(h_ref, x_ref, gn_ref[0, 2:3, :],
                           m[:, 4 * D_MODEL:5 * D_MODEL], m[:, 3 * D_MODEL:4 * D_MODEL])
        acc_ref[...] = jnp.zeros_like(acc_ref)

    wgb_ref[...] = wg_ref[0].astype(BF16)
    wvb_ref[...] = wv_ref[0].astype(BF16)
    wdb_ref[...] = wd_ref[0].astype(BF16)

    def up(r0, r1):
        h = h_ref[r0:r1, :]
        padg_ref[halo_rows + r0:halo_rows + r1, :] = jnp.dot(h, wgb_ref[...], preferred_element_type=F32)
        padv_ref[halo_rows + r0:halo_rows + r1, :] = jnp.dot(h, wvb_ref[...], preferred_element_type=F32)

    n_wrows = (FFN_CONV + 1) * SUBLANES
    for s, (cw_ref, cb_ref) in enumerate(((cwg_ref, cbg_ref), (cwv_ref, cbv_ref))):
        _expand_rows(wexp_ref, s * n_wrows, cw_ref[0])
        _expand_rows(wexp_ref, s * n_wrows + FFN_CONV * SUBLANES, cb_ref[0])

    def conv_act(rb):
        base = rb * CONV_RB
        for lb in range(FF_CW // LANES):
            lanes = slice(lb * LANES, (lb + 1) * LANES)
            outs = []
            for s, pad_ref in enumerate((padg_ref, padv_ref)):
                outs.append(_conv_block(pad_ref, base, lanes, wexp_ref, s * n_wrows, FFN_CONV))
            act_ref[base:base + CONV_RB, lanes] = (jax.nn.gelu(outs[0]) * outs[1]).astype(act_ref.dtype)

    def down(r0, r1):
        acc_ref[r0:r1, :] += jnp.dot(act_ref[r0:r1, :], wdb_ref[...], preferred_element_type=F32)

    up(0, TM // 2)
    up(TM // 2, TM)
    is_ctx = i < CTX_TILES
    _fill_time_halo(padg_ref, FFN_HALO, is_ctx)
    _fill_time_halo(padv_ref, FFN_HALO, is_ctx)
    for rb in range(1, FFN_SPLIT):
        conv_act(rb)
    conv_act(0)
    down(0, FFN_SPLIT * CONV_RB)
    for rb in range(FFN_SPLIT, FFN_BLOCKS):
        conv_act(rb)
    down(FFN_SPLIT * CONV_RB, TM)

    @pl.when(c == N_FF_STEPS - 1)
    def _():
        gate2 = m[:, 5 * D_MODEL:6 * D_MODEL]
        g3 = gn_ref[0, 3:4, :]

        def body(rb, carry):
            rows = pl.ds(pl.multiple_of(rb * NORM_RB, NORM_RB), NORM_RB)
            o_ref[rows, :] = x_ref[rows, :] + gate2 * _rms_rows(acc_ref[rows, :], g3)
            return carry
        lax.fori_loop(0, TM // NORM_RB, body, 0, unroll=2)


def _ffn(l, x, mod4, g_norm, ffn_up, ffn_conv_w, ffn_conv_b3, ffn_down):
    return _layer_call(
        _ffn_body,
        grid=(N_TILES, N_FF_STEPS),
        in_specs=[pl.BlockSpec((TM, D_MODEL), lambda i, c, l: (i, 0)),
                  pl.BlockSpec((1, MOD_ROWS, 1, 6 * D_MODEL), lambda i, c, l: (l[0], 0, 0, 0)),
                  pl.BlockSpec((1, 4, D_MODEL), lambda i, c, l: (l[0], 0, 0)),
                  pl.BlockSpec((1, D_MODEL, FF_CW), lambda i, c, l: (l[0], 0, c)),
                  pl.BlockSpec((1, D_MODEL, FF_CW), lambda i, c, l: (l[0], 0, N_FF_STEPS + c)),
                  pl.BlockSpec((1, FFN_CONV, FF_CW), lambda i, c, l: (l[0], 0, c)),
                  pl.BlockSpec((1, FFN_CONV, FF_CW), lambda i, c, l: (l[0], 0, N_FF_STEPS + c)),
                  pl.BlockSpec((1, 1, FF_CW), lambda i, c, l: (l[0], 0, c)),
                  pl.BlockSpec((1, 1, FF_CW), lambda i, c, l: (l[0], 0, N_FF_STEPS + c)),
                  pl.BlockSpec((1, FF_CW, D_MODEL), lambda i, c, l: (l[0], c, 0))],
        out_specs=pl.BlockSpec((TM, D_MODEL), lambda i, c, l: (i, 0)),
        out_shape=jax.ShapeDtypeStruct((N_TOK, D_MODEL), F32),
        scratch_shapes=[pltpu.VMEM((TM, D_MODEL), BF16),
                        pltpu.VMEM((TM, D_MODEL), F32),
                        pltpu.VMEM((TM + 2 * FFN_HALO * SUBLANES, FF_CW), F32),
                        pltpu.VMEM((TM + 2 * FFN_HALO * SUBLANES, FF_CW), F32),
                        pltpu.VMEM((TM, FF_CW), BF16),
                        pltpu.VMEM((D_MODEL, FF_CW), BF16),
                        pltpu.VMEM((D_MODEL, FF_CW), BF16),
                        pltpu.VMEM((FF_CW, D_MODEL), BF16),
                        pltpu.VMEM((2 * (FFN_CONV + 1) * SUBLANES, FF_CW), F32)],
        name="ffn")(l, x, mod4, g_norm, ffn_up, ffn_up, ffn_conv_w, ffn_conv_w,
                    ffn_conv_b3, ffn_conv_b3, ffn_down)


def _pos_table():
    t = jnp.arange(DEC_SEQ)
    r = (t // GRID_W).astype(F32)
    col = (t % GRID_W).astype(F32)
    q = D_MODEL // 4
    omega = 1.0 / (POS_BASE ** (jnp.arange(q, dtype=F32) / q))

    def emb(p):
        ang = p[:, None] * omega[None, :]
        return jnp.concatenate([jnp.sin(ang), jnp.cos(ang)], axis=-1)
    return jnp.concatenate([emb(r), emb(col)], axis=-1).astype(F32)


def _block_diag_gates(rg_w):
    per_tile = MXU_DIM // BS_RNN
    n_col = H_RNN // per_tile
    w = rg_w.reshape(DEPTH, 2, 2, n_col, per_tile, BS_RNN, BS_RNN)
    eye = jnp.eye(per_tile, dtype=rg_w.dtype)
    tiles = jnp.einsum('ldkcaij,ab->ldkcaibj', w, eye)
    return tiles.reshape(DEPTH, 2 * 2 * n_col, MXU_DIM, MXU_DIM).astype(BF16)


def kernel(x_prompt, x_sample, state_rglru, c, c_ctx, w_mod, b_mod, g_norm, w_in, b_in,
           rnn_conv_w, rnn_conv_b, rg_w, rg_b, rg_lambda, sg_norm_g, sg_w, sg_b,
           cf_conv_w, cf_conv_b, cf_ln_g, cf_ln_b, w_branch, w_out, ffn_up, ffn_conv_w,
           ffn_conv_b, ffn_down):
    x = _interleave(_prep(x_prompt.reshape(BATCH * SEQ, D_MODEL),
                          x_sample.reshape(DEC_BATCH * DEC_SEQ, D_MODEL), _pos_table()))

    cond = jnp.zeros((MOD_ROWS, D_MODEL), F32).at[:DEC_BATCH].set(c).at[CTX_MOD_ROW].set(c_ctx)
    mod4 = _mod(cond, w_mod, b_mod).reshape(DEPTH, MOD_ROWS, 1, 6 * D_MODEL)

    h0_all = jnp.zeros((DEPTH, N_TILES, SUBLANES, D_RNN), F32)
    h0_all = h0_all.at[:, CTX_TILES:, 0:2, :].set(jnp.transpose(state_rglru.astype(F32), (1, 0, 2, 3)))

    rgw = _block_diag_gates(0.5 * rg_w)
    wb_bf16 = w_branch.astype(BF16)
    wo_bf16 = w_out.astype(BF16)
    b_in3 = b_in.reshape(DEPTH, 1, N_IN)
    rnn_conv_b3 = rnn_conv_b.reshape(DEPTH, 1, D_RNN)
    rg_b4 = 0.5 * rg_b.reshape(DEPTH, 4, D_RNN)
    sg_norm_g3 = sg_norm_g.reshape(DEPTH, 1, D_SG)
    sg_bt = jnp.transpose(sg_b, (0, 2, 1))
    cf_conv_b3 = cf_conv_b.reshape(DEPTH, 1, D_CF)
    cf_ln_g3 = cf_ln_g.reshape(DEPTH, 1, D_CF)
    cf_ln_b3 = cf_ln_b.reshape(DEPTH, 1, D_CF)
    ffn_conv_b3 = ffn_conv_b.reshape(DEPTH, 1, 2 * D_FF)

    states = []
    for layer in range(DEPTH):
        l = jnp.full((1,), layer, jnp.int32)
        z = _inproj(l, x, mod4, g_norm, w_in, b_in3)
        ya, st = _rnn(l, z, rnn_conv_w, rnn_conv_b3, rgw, rg_b4, rg_lambda, h0_all)
        yb = _interleave(_sgu(l, z, sg_norm_g3, sg_w, sg_bt))
        yc = _conformer(l, z, cf_conv_w, cf_conv_b3, cf_ln_g3, cf_ln_b3)
        x = _merge(l, x, ya, yb, yc, z, mod4, g_norm, wb_bf16, wo_bf16)
        x = _ffn(l, x, mod4, g_norm, ffn_up, ffn_conv_w, ffn_conv_b3, ffn_down)
        states.append(st[0, :CTX_TILES])
    states = jnp.stack(states)

    x = _deinterleave(x)
    y_prompt = x[:BATCH * SEQ].reshape(BATCH, SEQ, D_MODEL)
    y_sample = x[BATCH * SEQ:].reshape(DEC_BATCH, DEC_SEQ, D_MODEL)
    new_state = states.reshape(DEPTH, BATCH, 2, D_RNN).transpose(1, 0, 2, 3)
    return (y_prompt, y_sample, new_state)
```

```python
import functools

import jax
import jax.numpy as jnp
import numpy as np
from jax import lax
from jax.experimental import pallas as pl
from jax.experimental.pallas import tpu as pltpu

F32 = jnp.float32
BF16 = jnp.bfloat16

D_MODEL = 1024
BATCH = 16
SEQ = 256
DEPTH = 4
DEC_BATCH = 4
DEC_SEQ = 1024
GRID_W = 64
D_RNN = 1024
H_RNN = 16
BS_RNN = D_RNN // H_RNN
RNN_CONV = 4
C_RG = 8.0
D_SG = 1024
SG_GROUPS = 8
SG_GD = D_SG // SG_GROUPS
CHUNK = 128
D_CF = 1024
CF_CONV = 31
D_FF = 4096
FFN_CONV = 3
N_IN = 2 * D_RNN + 2 * D_SG + 2 * D_CF + 3 * D_MODEL
EPS = 1e-6
POS_BASE = 10000.0

LANES = 128
SUBLANES = 8
MXU_DIM = 256

N_TOK = BATCH * SEQ + DEC_BATCH * DEC_SEQ
TM = 1024
N_TILES = N_TOK // TM
CTX_TILES = BATCH * SEQ // TM
MOD_ROWS = 8
CTX_MOD_ROW = DEC_BATCH
CONV_RB = 64
FF_CW = 512
TM_MERGE = 512
VMEM_LIMIT = 56 * 1024 * 1024


def _sigmoid(x):
    return 0.5 * (jnp.tanh(0.5 * x) + 1.0)


def _rms_rows(x, g):
    return x * lax.rsqrt(jnp.mean(x * x, axis=-1, keepdims=True) + EPS) * g


def _mod_row_of_tile(i, tile_rows):
    tiles_per_latent = DEC_SEQ // tile_rows
    ctx_tiles = BATCH * SEQ // tile_rows
    lat = jnp.maximum(i - ctx_tiles, 0) // tiles_per_latent
    return jnp.where(i < ctx_tiles, CTX_MOD_ROW, lat)


def _params(n_grid):
    return pltpu.CompilerParams(dimension_semantics=("arbitrary",) * n_grid,
                                vmem_limit_bytes=VMEM_LIMIT)


def _layer_call(body, grid, in_specs, out_specs, out_shape, scratch_shapes, name):
    return pl.pallas_call(
        body,
        grid_spec=pltpu.PrefetchScalarGridSpec(
            num_scalar_prefetch=1, grid=grid, in_specs=in_specs, out_specs=out_specs,
            scratch_shapes=scratch_shapes),
        out_shape=out_shape,
        compiler_params=_params(len(grid)),
        name=name)


STEPS = CHUNK
HALO_LW = 256


def _interleave(x):
    return x.reshape(N_TILES, SUBLANES, STEPS, -1).transpose(0, 2, 1, 3).reshape(x.shape)


def _deinterleave(x):
    return x.reshape(N_TILES, STEPS, SUBLANES, -1).transpose(0, 2, 1, 3).reshape(x.shape)


def _chunk_edge_masks(is_ctx, shape):
    c = lax.broadcasted_iota(jnp.int32, shape, 0) & (SUBLANES - 1)
    m = jnp.where(is_ctx, SEQ // CHUNK - 1, DEC_SEQ // CHUNK - 1)
    cm = c & m
    return cm == 0, cm == m


def _fill_time_halo(pad_ref, halo, is_ctx, width=None):
    n = halo * SUBLANES
    for lo in range(0, pad_ref.shape[1] if width is None else width, HALO_LW):
        lanes = slice(lo, lo + HALO_LW)
        first, last = _chunk_edge_masks(is_ctx, (n, HALO_LW))
        tail = pad_ref[TM:TM + n, lanes]
        head = pad_ref[n:2 * n, lanes]
        pad_ref[0:n, lanes] = jnp.where(first, 0.0, pltpu.roll(tail, 1, 0))
        pad_ref[n + TM:2 * n + TM, lanes] = jnp.where(last, 0.0, pltpu.roll(head, n - 1, 0))


def _expand_rows(dst_ref, row0, rows):
    for k in range(rows.shape[0]):
        lo = row0 + k * SUBLANES
        dst_ref[lo:lo + SUBLANES, :] = jnp.broadcast_to(rows[k:k + 1, :], (SUBLANES, rows.shape[1]))


def _conv_block(pad_ref, row0, lanes, wexp_ref, w0, n_taps):
    def wrow(k):
        return wexp_ref[w0 + k * SUBLANES:w0 + (k + 1) * SUBLANES, lanes]

    groups = CONV_RB // SUBLANES
    if n_taps <= SUBLANES:
        taps = [wrow(k) for k in range(n_taps)]
        accs = [wrow(n_taps)] * groups
        for j in range(n_taps + groups - 1):
            xj = pad_ref[pl.ds(row0 + j * SUBLANES, SUBLANES), lanes]
            for r in range(groups):
                if 0 <= j - r < n_taps:
                    accs[r] = accs[r] + xj * taps[j - r]
        return jnp.concatenate(accs, axis=0)
    acc = jnp.broadcast_to(wrow(n_taps), (groups, SUBLANES, LANES))
    for k in range(n_taps):
        x = pad_ref[pl.ds(row0 + k * SUBLANES, CONV_RB), lanes]
        acc = acc + x.reshape(groups, SUBLANES, LANES) * wrow(k)
    return acc.reshape(CONV_RB, LANES)


def _dwconv(pad_ref, halo, first_offset, n_taps, wexp_ref, emit, post=None):
    def body(b, carry):
        base = pl.multiple_of(b * CONV_RB, CONV_RB)
        for lb in range(pad_ref.shape[1] // LANES):
            lanes = slice(lb * LANES, (lb + 1) * LANES)
            emit(base, lb, _conv_block(pad_ref, base + (halo + first_offset) * SUBLANES, lanes,
                                       wexp_ref, 0, n_taps))
        if post is not None:
            post(base)
        return carry

    lax.fori_loop(0, TM // CONV_RB, body, 0)


def _prep_body(xp_ref, xs_ref, pos_ref, o_ref):
    i = pl.program_id(0)

    @pl.when(i < CTX_TILES)
    def _():
        o_ref[...] = xp_ref[...]

    @pl.when(i >= CTX_TILES)
    def _():
        o_ref[...] = xs_ref[...] + pos_ref[...]


def _prep(xp, xs, pos):
    return pl.pallas_call(
        _prep_body,
        grid=(N_TILES,),
        in_specs=[pl.BlockSpec((TM, D_MODEL), lambda i: (jnp.minimum(i, CTX_TILES - 1), 0)),
                  pl.BlockSpec((TM, D_MODEL), lambda i: (jnp.maximum(i - CTX_TILES, 0), 0)),
                  pl.BlockSpec((DEC_SEQ, D_MODEL), lambda i: (0, 0))],
        out_specs=pl.BlockSpec((TM, D_MODEL), lambda i: (i, 0)),
        out_shape=jax.ShapeDtypeStruct((N_TOK, D_MODEL), F32),
        compiler_params=_params(1),
        name="prep")(xp, xs, pos)


MOD_TN = 3072


def _mod_body(cond_ref, w_ref, b_ref, o_ref):
    c = cond_ref[...]
    s = (c * _sigmoid(c)).astype(BF16)
    o_ref[0] = jnp.dot(s, w_ref[0].astype(BF16), preferred_element_type=F32) + b_ref[0]


def _mod(cond, w_mod, b_mod):
    n_mod = 6 * D_MODEL
    return pl.pallas_call(
        _mod_body,
        grid=(DEPTH, n_mod // MOD_TN),
        in_specs=[pl.BlockSpec((MOD_ROWS, D_MODEL), lambda l, j: (0, 0)),
                  pl.BlockSpec((1, D_MODEL, MOD_TN), lambda l, j: (l, 0, j)),
                  pl.BlockSpec((1, 1, MOD_TN), lambda l, j: (l, 0, j))],
        out_specs=pl.BlockSpec((1, MOD_ROWS, MOD_TN), lambda l, j: (l, 0, j)),
        out_shape=jax.ShapeDtypeStruct((DEPTH, MOD_ROWS, n_mod), F32),
        compiler_params=_params(2),
        name="mod")(cond, w_mod, b_mod.reshape(DEPTH, 1, n_mod))


IN_TN = 1024
NORM_RB = 64


def _modulated_norm_to(h_ref, x_ref, g, scale, shift):
    def body(rb, carry):
        rows = pl.ds(pl.multiple_of(rb * NORM_RB, NORM_RB), NORM_RB)
        h = _rms_rows(x_ref[rows, :], g) * (1.0 + scale) + shift
        h_ref[rows, :] = h.astype(h_ref.dtype)
        return carry
    lax.fori_loop(0, x_ref.shape[0] // NORM_RB, body, 0, unroll=2)


SG_COL = 2 * D_RNN // IN_TN
IN_TILES = 2
IN_ROWS = IN_TILES * TM


def _inproj_body(l_ref, x_ref, mod_ref, gn_ref, w_ref, b_ref, z_ref, h_ref, hn_ref, perm_ref):
    i = pl.program_id(0)
    j = pl.program_id(1)

    @pl.when(jnp.logical_and(i == 0, j == 0))
    def _():
        nat = lax.broadcasted_iota(jnp.int32, (TM, TM), 0)
        src = lax.broadcasted_iota(jnp.int32, (TM, TM), 1)
        chunk_shift = CHUNK.bit_length() - 1
        row_of_nat = ((nat & (CHUNK - 1)) * SUBLANES) | jnp.right_shift(nat, chunk_shift)
        perm_ref[...] = jnp.where(src == row_of_nat, 1.0, 0.0).astype(BF16)

    @pl.when(j == 0)
    def _():
        for t in range(IN_TILES):
            rows = pl.ds(t * TM, TM)
            m = mod_ref[0, _mod_row_of_tile(IN_TILES * i + t, TM)]
            _modulated_norm_to(h_ref.at[rows], x_ref.at[rows], gn_ref[0, 0:1, :],
                               m[:, D_MODEL:2 * D_MODEL], m[:, 0:D_MODEL])
            hn_ref[rows, :] = jnp.dot(perm_ref[...], h_ref[rows, :],
                                      preferred_element_type=F32).astype(BF16)

    natural = jnp.logical_and(j >= SG_COL, j < SG_COL + 2 * D_SG // IN_TN)

    @pl.when(natural)
    def _():
        acc = jnp.dot(hn_ref[...], w_ref[0].astype(BF16), preferred_element_type=F32)
        z_ref[...] = (acc + b_ref[0]).astype(z_ref.dtype)

    @pl.when(jnp.logical_not(natural))
    def _():
        acc = jnp.dot(h_ref[...], w_ref[0].astype(BF16), preferred_element_type=F32)
        z_ref[...] = (acc + b_ref[0]).astype(z_ref.dtype)


def _inproj(l, x, mod4, g_norm, w_in, b_in3):
    return _layer_call(
        _inproj_body,
        grid=(N_TOK // IN_ROWS, N_IN // IN_TN),
        in_specs=[pl.BlockSpec((IN_ROWS, D_MODEL), lambda i, j, l: (i, 0)),
                  pl.BlockSpec((1, MOD_ROWS, 1, 6 * D_MODEL), lambda i, j, l: (l[0], 0, 0, 0)),
                  pl.BlockSpec((1, 4, D_MODEL), lambda i, j, l: (l[0], 0, 0)),
                  pl.BlockSpec((1, D_MODEL, IN_TN), lambda i, j, l: (l[0], 0, j)),
                  pl.BlockSpec((1, 1, IN_TN), lambda i, j, l: (l[0], 0, j))],
        out_specs=pl.BlockSpec((IN_ROWS, IN_TN), lambda i, j, l: (i, j)),
        out_shape=jax.ShapeDtypeStruct((N_TOK, N_IN), BF16),
        scratch_shapes=[pltpu.VMEM((IN_ROWS, D_MODEL), BF16),
                        pltpu.VMEM((IN_ROWS, D_MODEL), BF16),
                        pltpu.VMEM((TM, TM), BF16)],
        name="inproj")(l, x, mod4, g_norm, w_in, b_in3)


RNN_HALO = 2
GATE_RB = 512
FILL_RB = 64


def _rows_of(value):
    return [value[r:r + 1, :] for r in range(SUBLANES)]


def _chain_chunks(a_end, b_end, h0_rows, is_ctx, reverse):
    per_ctx_seq = SEQ // CHUNK
    a_rows, b_rows = _rows_of(a_end), _rows_of(b_end)
    h_in, h_out = [None] * SUBLANES, [None] * SUBLANES
    order = range(SUBLANES - 1, -1, -1) if reverse else range(SUBLANES)
    prev = None
    for c in order:
        starts_ctx_seq = (c % per_ctx_seq == per_ctx_seq - 1) if reverse else (c % per_ctx_seq == 0)
        if prev is None:
            h = jnp.where(is_ctx, h0_rows[c // per_ctx_seq], h0_rows[0])
        elif starts_ctx_seq:
            h = jnp.where(is_ctx, h0_rows[c // per_ctx_seq], prev)
        else:
            h = prev
        h_in[c] = h
        prev = a_rows[c] * h + b_rows[c]
        h_out[c] = prev
    return jnp.concatenate(h_in, axis=0), h_out


def _rnn_body(l_ref, zx_ref, zg_ref, cw_ref, cb_ref, rgw_ref, rgb_ref, lam_ref, h0_ref,
              ya_ref, st_ref, pad_ref, wexp_ref, xc_ref, af_ref, uf_ref, ab_ref, ub_ref):
    i = pl.program_id(0)
    is_ctx = i < CTX_TILES
    halo_rows = RNN_HALO * SUBLANES

    def fill(rb, carry):
        r0 = pl.multiple_of(rb * FILL_RB, FILL_RB)
        pad_ref[pl.ds(r0 + halo_rows, FILL_RB), :] = zx_ref[pl.ds(r0, FILL_RB), :].astype(F32)
        return carry

    lax.fori_loop(0, TM // FILL_RB, fill, 0)
    _fill_time_halo(pad_ref, RNN_HALO, is_ctx)

    _expand_rows(wexp_ref, 0, cw_ref[0])
    _expand_rows(wexp_ref, RNN_CONV * SUBLANES, cb_ref[0])

    def emit_xc(row0, lb, block):
        xc_ref[pl.ds(row0, CONV_RB), lb * LANES:(lb + 1) * LANES] = block

    _dwconv(pad_ref, RNN_HALO, -2, RNN_CONV, wexp_ref, emit_xc)

    neg_lam = -lam_ref[0]
    softplus = jnp.maximum(neg_lam, 0.0) + jnp.log1p(jnp.exp(-jnp.abs(neg_lam)))
    coef = (-0.5 * C_RG * np.log2(np.e)) * softplus
    a_refs = (af_ref, ab_ref)
    u_refs = (uf_ref, ub_ref)
    n_col = D_RNN // MXU_DIM

    def gate_body(rb, carry):
        rows = pl.ds(pl.multiple_of(rb * GATE_RB, GATE_RB), GATE_RB)
        for j in range(n_col):
            lanes = slice(j * MXU_DIM, (j + 1) * MXU_DIM)
            xc = xc_ref[rows, lanes]
            xcb = xc.astype(BF16)
            half_xc = 0.5 * xc
            for d in range(2):
                base = d * 2 * n_col
                g_r = jnp.dot(xcb, rgw_ref[0, base + j], preferred_element_type=F32)
                g_i = jnp.dot(xcb, rgw_ref[0, base + n_col + j], preferred_element_type=F32)
                t_r = jnp.tanh(g_r + rgb_ref[0, 2 * d:2 * d + 1, lanes])
                t_i = jnp.tanh(g_i + rgb_ref[0, 2 * d + 1:2 * d + 2, lanes])
                cf = coef[d:d + 1, lanes]
                a = jnp.exp2(cf * t_r + cf)
                y = 1.0 - a * a
                root = jnp.where(y == 0.0, 0.0, y * lax.rsqrt(y))
                a_refs[d][rows, lanes] = a
                u_refs[d][rows, lanes] = root * ((t_i + 1.0) * half_xc)
        return carry

    lax.fori_loop(0, TM // GATE_RB, gate_body, 0)

    def group(t):
        return pl.ds(pl.multiple_of(t * SUBLANES, SUBLANES), SUBLANES)

    def totals(t, carry):
        a_f, b_f, a_b, b_b = carry
        rf, rb = group(t), group(STEPS - 1 - t)
        a = af_ref[rf, :]
        b_f = a * b_f + uf_ref[rf, :]
        a_f = a * a_f
        a = ab_ref[rb, :]
        b_b = a * b_b + ub_ref[rb, :]
        a_b = a * a_b
        return a_f, b_f, a_b, b_b

    ones = jnp.ones((SUBLANES, D_RNN), F32)
    zeros = jnp.zeros((SUBLANES, D_RNN), F32)
    a_f, b_f, a_b, b_b = lax.fori_loop(0, STEPS, totals, (ones, zeros, ones, zeros), unroll=2)

    n_seq = TM // SEQ
    h0_f = [h0_ref[0, 0, 2 * s:2 * s + 1, :] for s in range(n_seq)]
    h0_b = [h0_ref[0, 0, 2 * s + 1:2 * s + 2, :] for s in range(n_seq)]
    hin_f, hout_f = _chain_chunks(a_f, b_f, h0_f, is_ctx, False)
    hin_b, hout_b = _chain_chunks(a_b, b_b, h0_b, is_ctx, True)

    per_ctx_seq = SEQ // CHUNK
    none = jnp.zeros((1, D_RNN), F32)
    for s in range(n_seq):
        last_f = hout_f[s * per_ctx_seq + per_ctx_seq - 1]
        first_b = hout_b[s * per_ctx_seq]
        st_ref[0, 0, 2 * s:2 * s + 1, :] = jnp.where(is_ctx, last_f, hout_f[SUBLANES - 1] if s == 0 else none)
        st_ref[0, 0, 2 * s + 1:2 * s + 2, :] = jnp.where(is_ctx, first_b, hout_b[0] if s == 0 else none)

    def backward(t, h):
        rb = group(STEPS - 1 - t)
        h = ab_ref[rb, :] * h + ub_ref[rb, :]
        ub_ref[rb, :] = h
        return h

    lax.fori_loop(0, STEPS, backward, hin_b, unroll=2)

    def forward(tt, h):
        r0, r1 = group(2 * tt), group(2 * tt + 1)
        h0 = af_ref[r0, :] * h + uf_ref[r0, :]
        h1 = af_ref[r1, :] * h0 + uf_ref[r1, :]
        rows = pl.ds(pl.multiple_of(tt * 2 * SUBLANES, 2 * SUBLANES), 2 * SUBLANES)
        both = jnp.concatenate([h0, h1], axis=0) + ub_ref[rows, :]
        ya_ref[rows, :] = (both * jax.nn.gelu(zg_ref[rows, :].astype(F32))).astype(ya_ref.dtype)
        return h1

    lax.fori_loop(0, STEPS // 2, forward, hin_f, unroll=2)


def _rnn(l, z, rnn_conv_w, rnn_conv_b3, rgw, rg_b4, rg_lambda, h0_all):
    n_rgw = 2 * 2 * (D_RNN // MXU_DIM)
    return _layer_call(
        _rnn_body,
        grid=(N_TILES,),
        in_specs=[pl.BlockSpec((TM, D_RNN), lambda i, l: (i, 0)),
                  pl.BlockSpec((TM, D_RNN), lambda i, l: (i, 1)),
                  pl.BlockSpec((1, RNN_CONV, D_RNN), lambda i, l: (l[0], 0, 0)),
                  pl.BlockSpec((1, 1, D_RNN), lambda i, l: (l[0], 0, 0)),
                  pl.BlockSpec((1, n_rgw, MXU_DIM, MXU_DIM), lambda i, l: (l[0], 0, 0, 0)),
                  pl.BlockSpec((1, 4, D_RNN), lambda i, l: (l[0], 0, 0)),
                  pl.BlockSpec((1, 2, D_RNN), lambda i, l: (l[0], 0, 0)),
                  pl.BlockSpec((1, 1, SUBLANES, D_RNN), lambda i, l: (l[0], i, 0, 0))],
        out_specs=[pl.BlockSpec((TM, D_MODEL), lambda i, l: (i, 0)),
                   pl.BlockSpec((1, 1, SUBLANES, D_RNN), lambda i, l: (0, i, 0, 0))],
        out_shape=[jax.ShapeDtypeStruct((N_TOK, D_RNN), BF16),
                   jax.ShapeDtypeStruct((1, N_TILES, SUBLANES, D_RNN), F32)],
        scratch_shapes=[pltpu.VMEM((TM + 2 * RNN_HALO * SUBLANES, D_RNN), F32),
                        pltpu.VMEM(((RNN_CONV + 1) * SUBLANES, D_RNN), F32),
                        pltpu.VMEM((TM, D_RNN), F32),
                        pltpu.VMEM((TM, D_RNN), F32),
                        pltpu.VMEM((TM, D_RNN), F32),
                        pltpu.VMEM((TM, D_RNN), F32),
                        pltpu.VMEM((TM, D_RNN), F32)],
        name="rnn")(l, z, z, rnn_conv_w, rnn_conv_b3, rgw, rg_b4, rg_lambda, h0_all)


ACT_RB = 64
N_CHUNKS = TM // CHUNK


def _sgu_body(l_ref, z_ref, g_ref, w_ref, bt_ref, yb_ref, su_ref, sv_ref):
    gain = g_ref[0]

    def act_body(rb, carry):
        rows = pl.ds(pl.multiple_of(rb * ACT_RB, ACT_RB), ACT_RB)
        uv = jax.nn.gelu(z_ref[rows, :].astype(F32))
        su_ref[rows, :] = uv[:, :D_SG]
        sv = uv[:, D_SG:]
        mu = jnp.mean(sv, axis=-1, keepdims=True)
        svc = sv - mu
        y = svc * lax.rsqrt(jnp.mean(svc * svc, axis=-1, keepdims=True) + EPS) * gain
        sv_ref[rows, :] = y.astype(sv_ref.dtype)
        return carry

    lax.fori_loop(0, TM // ACT_RB, act_body, 0, unroll=2)

    for g in range(SG_GROUPS):
        lanes = slice(g * SG_GD, (g + 1) * SG_GD)
        rhs = jnp.concatenate([sv_ref[n * CHUNK:(n + 1) * CHUNK, lanes] for n in range(N_CHUNKS)],
                              axis=1)
        mixed = jnp.dot(w_ref[0, g].astype(BF16), rhs, preferred_element_type=F32)
        bias = bt_ref[0, :, g:g + 1]
        for n in range(N_CHUNKS):
            rows = slice(n * CHUNK, (n + 1) * CHUNK)
            m = mixed[:, n * SG_GD:(n + 1) * SG_GD] + bias
            yb_ref[rows, lanes] = (su_ref[rows, lanes] * m).astype(yb_ref.dtype)


def _sgu(l, z, sg_norm_g3, sg_w, sg_bt):
    return _layer_call(
        _sgu_body,
        grid=(N_TILES,),
        in_specs=[pl.BlockSpec((TM, 2 * D_SG), lambda i, l: (i, 1)),
                  pl.BlockSpec((1, 1, D_SG), lambda i, l: (l[0], 0, 0)),
                  pl.BlockSpec((1, SG_GROUPS, CHUNK, CHUNK), lambda i, l: (l[0], 0, 0, 0)),
                  pl.BlockSpec((1, CHUNK, SG_GROUPS), lambda i, l: (l[0], 0, 0))],
        out_specs=pl.BlockSpec((TM, D_SG), lambda i, l: (i, 0)),
        out_shape=jax.ShapeDtypeStruct((N_TOK, D_SG), BF16),
        scratch_shapes=[pltpu.VMEM((TM, D_SG), F32), pltpu.VMEM((TM, D_SG), BF16)],
        name="sgu")(l, z, sg_norm_g3, sg_w, sg_bt)


CF_HALO = CF_CONV // 2


CF_SUMS = 4


def _conformer_body(l_ref, z_ref, cw_ref, cb_ref, g_ref, b_ref, yc_ref, pad_ref, wexp_ref, conv_ref):
    i = pl.program_id(0)
    halo_rows = CF_HALO * SUBLANES

    def glu(rb, carry):
        r0 = pl.multiple_of(rb * FILL_RB, FILL_RB)
        zz = z_ref[pl.ds(r0, FILL_RB), :].astype(F32)
        pad_ref[pl.ds(r0 + halo_rows, FILL_RB), 0:D_CF] = zz[:, :D_CF] * _sigmoid(zz[:, D_CF:])
        return carry

    lax.fori_loop(0, TM // FILL_RB, glu, 0)
    _fill_time_halo(pad_ref, CF_HALO, i < CTX_TILES, D_CF)

    _expand_rows(wexp_ref, 0, cw_ref[0])
    _expand_rows(wexp_ref, CF_CONV * SUBLANES, cb_ref[0])

    groups = CONV_RB // SUBLANES
    for lb in range(D_CF // LANES):
        lanes = slice(lb * LANES, (lb + 1) * LANES)
        weights = tuple(wexp_ref[k * SUBLANES:(k + 1) * SUBLANES, lanes] for k in range(CF_CONV + 1))

        def body(b, ws, lanes=lanes):
            base = pl.multiple_of(b * CONV_RB, CONV_RB)
            sums = [jnp.broadcast_to(ws[CF_CONV], (groups, SUBLANES, LANES))] + [None] * (CF_SUMS - 1)
            for k in range(CF_CONV):
                x = pad_ref[pl.ds(base + k * SUBLANES, CONV_RB), lanes]
                term = x.reshape(groups, SUBLANES, LANES) * ws[k]
                sums[k % CF_SUMS] = term if sums[k % CF_SUMS] is None else sums[k % CF_SUMS] + term
            acc = (sums[0] + sums[1]) + (sums[2] + sums[3])
            conv_ref[pl.ds(base, CONV_RB), lanes] = acc.reshape(CONV_RB, LANES)
            return ws

        lax.fori_loop(0, TM // CONV_RB, body, weights)

    def post(b, carry):
        rows = pl.ds(pl.multiple_of(b * CONV_RB, CONV_RB), CONV_RB)
        y = conv_ref[rows, :]
        mu = jnp.mean(y, axis=-1, keepdims=True)
        yc = y - mu
        ln = yc * lax.rsqrt(jnp.mean(yc * yc, axis=-1, keepdims=True) + EPS) * g_ref[0] + b_ref[0]
        yc_ref[rows, :] = (ln * _sigmoid(ln)).astype(yc_ref.dtype)
        return carry

    lax.fori_loop(0, TM // CONV_RB, post, 0, unroll=2)


def _conformer(l, z, cf_conv_w, cf_conv_b3, cf_ln_g3, cf_ln_b3):
    return _layer_call(
        _conformer_body,
        grid=(N_TILES,),
        in_specs=[pl.BlockSpec((TM, 2 * D_CF), lambda i, l: (i, 2)),
                  pl.BlockSpec((1, CF_CONV, D_CF), lambda i, l: (l[0], 0, 0)),
                  pl.BlockSpec((1, 1, D_CF), lambda i, l: (l[0], 0, 0)),
                  pl.BlockSpec((1, 1, D_CF), lambda i, l: (l[0], 0, 0)),
                  pl.BlockSpec((1, 1, D_CF), lambda i, l: (l[0], 0, 0))],
        out_specs=pl.BlockSpec((TM, D_CF), lambda i, l: (i, 0)),
        out_shape=jax.ShapeDtypeStruct((N_TOK, D_CF), BF16),
        scratch_shapes=[pltpu.VMEM((TM + 2 * CF_HALO * SUBLANES, D_CF + LANES), F32),
                        pltpu.VMEM(((CF_CONV + 1) * SUBLANES, D_CF), F32),
                        pltpu.VMEM((TM, D_CF), F32)],
        name="conformer")(l, z, cf_conv_w, cf_conv_b3, cf_ln_g3, cf_ln_b3)


def _merge_body(l_ref, x_ref, ya_ref, yb_ref, yc_ref, zg_ref, mod_ref, gn_ref, wb_ref, wo_ref,
                o_ref, wbb_ref, wob_ref):
    i = pl.program_id(0)

    @pl.when(i == 0)
    def _():
        for p in range(3):
            wbb_ref[p] = wb_ref[0, p].astype(BF16)
        wob_ref[...] = wo_ref[0].astype(BF16)

    merged = None
    for p, y_ref in enumerate((ya_ref, yb_ref, yc_ref)):
        gate = _sigmoid(zg_ref[:, p * D_MODEL:(p + 1) * D_MODEL].astype(F32))
        term = gate * jnp.dot(y_ref[...], wbb_ref[p], preferred_element_type=F32)
        merged = term if merged is None else merged + term
    out = jnp.dot(merged.astype(BF16), wob_ref[...], preferred_element_type=F32)
    m = mod_ref[0, _mod_row_of_tile(i, TM_MERGE)]
    gate1 = m[:, 2 * D_MODEL:3 * D_MODEL]
    o_ref[...] = x_ref[...] + gate1 * _rms_rows(out, gn_ref[0, 1:2, :])


def _merge(l, x, ya, yb, yc, z, mod4, g_norm, w_branch, w_out):
    tile = lambda i, l: (i, 0)
    return _layer_call(
        _merge_body,
        grid=(N_TOK // TM_MERGE,),
        in_specs=[pl.BlockSpec((TM_MERGE, D_MODEL), tile),
                  pl.BlockSpec((TM_MERGE, D_MODEL), tile),
                  pl.BlockSpec((TM_MERGE, D_MODEL), tile),
                  pl.BlockSpec((TM_MERGE, D_MODEL), tile),
                  pl.BlockSpec((TM_MERGE, 3 * D_MODEL), lambda i, l: (i, 2)),
                  pl.BlockSpec((1, MOD_ROWS, 1, 6 * D_MODEL), lambda i, l: (l[0], 0, 0, 0)),
                  pl.BlockSpec((1, 4, D_MODEL), lambda i, l: (l[0], 0, 0)),
                  pl.BlockSpec((1, 3, D_MODEL, D_MODEL), lambda i, l: (l[0], 0, 0, 0),
                               pipeline_mode=pl.Buffered(1)),
                  pl.BlockSpec((1, D_MODEL, D_MODEL), lambda i, l: (l[0], 0, 0),
                               pipeline_mode=pl.Buffered(1))],
        out_specs=pl.BlockSpec((TM_MERGE, D_MODEL), tile),
        out_shape=jax.ShapeDtypeStruct((N_TOK, D_MODEL), F32),
        scratch_shapes=[pltpu.VMEM((3, D_MODEL, D_MODEL), BF16),
                        pltpu.VMEM((D_MODEL, D_MODEL), BF16)],
        name="merge")(l, x, ya, yb, yc, z, mod4, g_norm, w_branch, w_out)


FFN_HALO = 1
N_FF_STEPS = D_FF // FF_CW
FFN_BLOCKS = TM // CONV_RB
FFN_SPLIT = (TM // 2 - FFN_HALO * SUBLANES) // CONV_RB


def _ffn_body(l_ref, x_ref, mod_ref, gn_ref, wg_ref, wv_ref, cwg_ref, cwv_ref, cbg_ref, cbv_ref,
              wd_ref, o_ref, h_ref, acc_ref, padg_ref, padv_ref, act_ref, wgb_ref, wvb_ref, wdb_ref,
              wexp_ref):
    i = pl.program_id(0)
    c = pl.program_id(1)
    m = mod_ref[0, _mod_row_of_tile(i, TM)]
    halo_rows = FFN_HALO * SUBLANES

    @pl.when(c == 0)
    def _():
        _modulated_norm_to(h_ref, x_ref, gn_ref[0, 2:3, :],
                           m[:, 4 * D_MODEL:5 * D_MODEL], m[:, 3 * D_MODEL:4 * D_MODEL])
        acc_ref[...] = jnp.zeros_like(acc_ref)

    wgb_ref[...] = wg_ref[0].astype(BF16)
    wvb_ref[...] = wv_ref[0].astype(BF16)
    wdb_ref[...] = wd_ref[0].astype(BF16)

    def up(r0, r1):
        h = h_ref[r0:r1, :]
        padg_ref[halo_rows + r0:halo_rows + r1, :] = jnp.dot(h, wgb_ref[...], preferred_element_type=F32)
        padv_ref[halo_rows + r0:halo_rows + r1, :] = jnp.dot(h, wvb_ref[...], preferred_element_type=F32)

    n_wrows = (FFN_CONV + 1) * SUBLANES
    for s, (cw_ref, cb_ref) in enumerate(((cwg_ref, cbg_ref), (cwv_ref, cbv_ref))):
        _expand_rows(wexp_ref, s * n_wrows, cw_ref[0])
        _expand_rows(wexp_ref, s * n_wrows + FFN_CONV * SUBLANES, cb_ref[0])

    def conv_act(rb):
        base = rb * CONV_RB
        for lb in range(FF_CW // LANES):
            lanes = slice(lb * LANES, (lb + 1) * LANES)
            outs = []
            for s, pad_ref in enumerate((padg_ref, padv_ref)):
                outs.append(_conv_block(pad_ref, base, lanes, wexp_ref, s * n_wrows, FFN_CONV))
            act_ref[base:base + CONV_RB, lanes] = (jax.nn.gelu(outs[0]) * outs[1]).astype(act_ref.dtype)

    def down(r0, r1):
        acc_ref[r0:r1, :] += jnp.dot(act_ref[r0:r1, :], wdb_ref[...], preferred_element_type=F32)

    up(0, TM // 2)
    up(TM // 2, TM)
    is_ctx = i < CTX_TILES
    _fill_time_halo(padg_ref, FFN_HALO, is_ctx)
    _fill_time_halo(padv_ref, FFN_HALO, is_ctx)
    for rb in range(1, FFN_SPLIT):
        conv_act(rb)
    conv_act(0)
    down(0, FFN_SPLIT * CONV_RB)
    for rb in range(FFN_SPLIT, FFN_BLOCKS):
        conv_act(rb)
    down(FFN_SPLIT * CONV_RB, TM)

    @pl.when(c == N_FF_STEPS - 1)
    def _():
        gate2 = m[:, 5 * D_MODEL:6 * D_MODEL]
        g3 = gn_ref[0, 3:4, :]

        def body(rb, carry):
            rows = pl.ds(pl.multiple_of(rb * NORM_RB, NORM_RB), NORM_RB)
            o_ref[rows, :] = x_ref[rows, :] + gate2 * _rms_rows(acc_ref[rows, :], g3)
            return carry
        lax.fori_loop(0, TM // NORM_RB, body, 0, unroll=2)


def _ffn(l, x, mod4, g_norm, ffn_up, ffn_conv_w, ffn_conv_b3, ffn_down):
    return _layer_call(
        _ffn_body,
        grid=(N_TILES, N_FF_STEPS),
        in_specs=[pl.BlockSpec((TM, D_MODEL), lambda i, c, l: (i, 0)),
                  pl.BlockSpec((1, MOD_ROWS, 1, 6 * D_MODEL), lambda i, c, l: (l[0], 0, 0, 0)),
                  pl.BlockSpec((1, 4, D_MODEL), lambda i, c, l: (l[0], 0, 0)),
                  pl.BlockSpec((1, D_MODEL, FF_CW), lambda i, c, l: (l[0], 0, c)),
                  pl.BlockSpec((1, D_MODEL, FF_CW), lambda i, c, l: (l[0], 0, N_FF_STEPS + c)),
                  pl.BlockSpec((1, FFN_CONV, FF_CW), lambda i, c, l: (l[0], 0, c)),
                  pl.BlockSpec((1, FFN_CONV, FF_CW), lambda i, c, l: (l[0], 0, N_FF_STEPS + c)),
                  pl.BlockSpec((1, 1, FF_CW), lambda i, c, l: (l[0], 0, c)),
                  pl.BlockSpec((1, 1, FF_CW), lambda i, c, l: (l[0], 0, N_FF_STEPS + c)),
                  pl.BlockSpec((1, FF_CW, D_MODEL), lambda i, c, l: (l[0], c, 0))],
        out_specs=pl.BlockSpec((TM, D_MODEL), lambda i, c, l: (i, 0)),
        out_shape=jax.ShapeDtypeStruct((N_TOK, D_MODEL), F32),
        scratch_shapes=[pltpu.VMEM((TM, D_MODEL), BF16),
                        pltpu.VMEM((TM, D_MODEL), F32),
                        pltpu.VMEM((TM + 2 * FFN_HALO * SUBLANES, FF_CW), F32),
                        pltpu.VMEM((TM + 2 * FFN_HALO * SUBLANES, FF_CW), F32),
                        pltpu.VMEM((TM, FF_CW), BF16),
                        pltpu.VMEM((D_MODEL, FF_CW), BF16),
                        pltpu.VMEM((D_MODEL, FF_CW), BF16),
                        pltpu.VMEM((FF_CW, D_MODEL), BF16),
                        pltpu.VMEM((2 * (FFN_CONV + 1) * SUBLANES, FF_CW), F32)],
        name="ffn")(l, x, mod4, g_norm, ffn_up, ffn_up, ffn_conv_w, ffn_conv_w,
                    ffn_conv_b3, ffn_conv_b3, ffn_down)


def _pos_table():
    t = jnp.arange(DEC_SEQ)
    r = (t // GRID_W).astype(F32)
    col = (t % GRID_W).astype(F32)
    q = D_MODEL // 4
    omega = 1.0 / (POS_BASE ** (jnp.arange(q, dtype=F32) / q))

    def emb(p):
        ang = p[:, None] * omega[None, :]
        return jnp.concatenate([jnp.sin(ang), jnp.cos(ang)], axis=-1)
    return jnp.concatenate([emb(r), emb(col)], axis=-1).astype(F32)


def _block_diag_gates(rg_w):
    per_tile = MXU_DIM // BS_RNN
    n_col = H_RNN // per_tile
    w = rg_w.reshape(DEPTH, 2, 2, n_col, per_tile, BS_RNN, BS_RNN)
    eye = jnp.eye(per_tile, dtype=rg_w.dtype)
    tiles = jnp.einsum('ldkcaij,ab->ldkcaibj', w, eye)
    return tiles.reshape(DEPTH, 2 * 2 * n_col, MXU_DIM, MXU_DIM).astype(BF16)


def kernel(x_prompt, x_sample, state_rglru, c, c_ctx, w_mod, b_mod, g_norm, w_in, b_in,
           rnn_conv_w, rnn_conv_b, rg_w, rg_b, rg_lambda, sg_norm_g, sg_w, sg_b,
           cf_conv_w, cf_conv_b, cf_ln_g, cf_ln_b, w_branch, w_out, ffn_up, ffn_conv_w,
           ffn_conv_b, ffn_down):
    x = _interleave(_prep(x_prompt.reshape(BATCH * SEQ, D_MODEL),
                          x_sample.reshape(DEC_BATCH * DEC_SEQ, D_MODEL), _pos_table()))

    cond = jnp.zeros((MOD_ROWS, D_MODEL), F32).at[:DEC_BATCH].set(c).at[CTX_MOD_ROW].set(c_ctx)
    mod4 = _mod(cond, w_mod, b_mod).reshape(DEPTH, MOD_ROWS, 1, 6 * D_MODEL)

    h0_all = jnp.zeros((DEPTH, N_TILES, SUBLANES, D_RNN), F32)
    h0_all = h0_all.at[:, CTX_TILES:, 0:2, :].set(jnp.transpose(state_rglru.astype(F32), (1, 0, 2, 3)))

    rgw = _block_diag_gates(0.5 * rg_w)
    b_in3 = b_in.reshape(DEPTH, 1, N_IN)
    rnn_conv_b3 = rnn_conv_b.reshape(DEPTH, 1, D_RNN)
    rg_b4 = 0.5 * rg_b.reshape(DEPTH, 4, D_RNN)
    sg_norm_g3 = sg_norm_g.reshape(DEPTH, 1, D_SG)
    sg_bt = jnp.transpose(sg_b, (0, 2, 1))
    cf_conv_b3 = cf_conv_b.reshape(DEPTH, 1, D_CF)
    cf_ln_g3 = cf_ln_g.reshape(DEPTH, 1, D_CF)
    cf_ln_b3 = cf_ln_b.reshape(DEPTH, 1, D_CF)
    ffn_conv_b3 = ffn_conv_b.reshape(DEPTH, 1, 2 * D_FF)

    states = []
    for layer in range(DEPTH):
        l = jnp.full((1,), layer, jnp.int32)
        z = _inproj(l, x, mod4, g_norm, w_in, b_in3)
        ya, st = _rnn(l, z, rnn_conv_w, rnn_conv_b3, rgw, rg_b4, rg_lambda, h0_all)
        yb = _interleave(_sgu(l, z, sg_norm_g3, sg_w, sg_bt))
        yc = _conformer(l, z, cf_conv_w, cf_conv_b3, cf_ln_g3, cf_ln_b3)
        x = _merge(l, x, ya, yb, yc, z, mod4, g_norm, w_branch, w_out)
        x = _ffn(l, x, mod4, g_norm, ffn_up, ffn_conv_w, ffn_conv_b3, ffn_down)
        states.append(st[0, :CTX_TILES])
    states = jnp.stack(states)

    x = _deinterleave(x)
    y_prompt = x[:BATCH * SEQ].reshape(BATCH, SEQ, D_MODEL)
    y_sample = x[BATCH * SEQ:].reshape(DEC_BATCH, DEC_SEQ, D_MODEL)
    new_state = states.reshape(DEPTH, BATCH, 2, D_RNN).transpose(1, 0, 2, 3)
    return (y_prompt, y_sample, new_state)
```
